```python
import math
import jax, jax.numpy as jnp
from jax import lax
import numpy as np

D_MODEL = 2048
BATCH = 1
SEQ = 16384
DEPTH = 1

CHUNK = 64
LEFT_CHUNKS = 8
BAND = (LEFT_CHUNKS + 1) * CHUNK
ATTN_HEADS = 8
ATTN_HEAD_DIM = 128
ATTN_WIDTH = ATTN_HEADS * ATTN_HEAD_DIM
REL_MAX = 256
REL_MIN = -(CHUNK - 1)
REL_SIZE = REL_MAX - REL_MIN + 1
RNN_WIDTH = D_MODEL // 2
RNN_BLOCKS = 8
RNN_BLOCK_W = RNN_WIDTH // RNN_BLOCKS
CONV_W = 4
RG_C = 8.0
LRU_A_MIN = 0.9
LRU_A_MAX = 0.999
SPLIT_SIZES = (ATTN_WIDTH, ATTN_WIDTH, ATTN_WIDTH, RNN_WIDTH, RNN_WIDTH, D_MODEL, D_MODEL)
IN_WIDTH = sum(SPLIT_SIZES)
SPLIT_POINTS = tuple(int(v) for v in np.cumsum(SPLIT_SIZES)[:-1])
N_EXPERTS = 32
TOP_K = 4
D_FF = D_MODEL
SWIGLU_LIMIT = 7.0
SWIGLU_ALPHA = 1.702
EXPERT_BLOCK = 256
EPS = 1e-6

kernel_name = "hybrid_chunked_attn_rglru_moe_block"


def rms_norm(x, g):
    x32 = x.astype(jnp.float32)
    y = x32 * lax.rsqrt(jnp.mean(x32 * x32, axis=-1, keepdims=True) + EPS) * g.astype(jnp.float32)
    return y.astype(x.dtype)


def chunked_band_attention(q, k, v, rel_table):
    B, S, H, Dh = q.shape
    n_chunks = S // CHUNK
    pad = LEFT_CHUNKS * CHUNK
    kp = jnp.pad(k, ((0, 0), (pad, 0), (0, 0), (0, 0)))
    vp = jnp.pad(v, ((0, 0), (pad, 0), (0, 0), (0, 0)))
    qc = q.reshape(B, n_chunks, CHUNK, H, Dh).transpose(1, 0, 2, 3, 4)
    qi = jnp.arange(CHUNK)[:, None]
    kj = jnp.arange(BAND)[None, :]
    rel_idx = jnp.clip(qi - kj + pad, REL_MIN, REL_MAX) - REL_MIN
    bias = rel_table.astype(jnp.float32)[:, rel_idx]
    scale = 1.0 / math.sqrt(Dh)

    def one_chunk(args):
        q_blk, cidx = args
        start = cidx * CHUNK
        kb = lax.dynamic_slice_in_dim(kp, start, BAND, axis=1)
        vb = lax.dynamic_slice_in_dim(vp, start, BAND, axis=1)
        s = jnp.einsum('bqhd,bkhd->bhqk', q_blk, kb).astype(jnp.float32) * scale + bias[None]
        valid = (jnp.arange(BAND) + start - pad) >= 0
        s = jnp.where(valid[None, None, None, :], s, -1e30)
        p = jax.nn.softmax(s, axis=-1).astype(vb.dtype)
        return jnp.einsum('bhqk,bkhd->bqhd', p, vb)

    out = lax.map(one_chunk, (qc, jnp.arange(n_chunks)))
    return out.transpose(1, 0, 2, 3, 4).reshape(B, S, H * Dh)


def causal_depthwise_conv(x, w, b):
    y = lax.conv_general_dilated(
        x, w[:, None, :], window_strides=(1,), padding=[(CONV_W - 1, 0)],
        dimension_numbers=('NWC', 'WIO', 'NWC'), feature_group_count=x.shape[-1])
    return y + b


def rg_lru(x, w_a, b_a, w_x, b_x, lam):
    B, S, W = x.shape
    xh = x.reshape(B, S, RNN_BLOCKS, RNN_BLOCK_W)
    r = jax.nn.sigmoid(jnp.einsum('bshi,hij->bshj', xh, w_a) + b_a).reshape(B, S, W)
    i_gate = jax.nn.sigmoid(jnp.einsum('bshi,hij->bshj', xh, w_x) + b_x).reshape(B, S, W)
    log_a = -RG_C * r.astype(jnp.float32) * jax.nn.softplus(-lam.astype(jnp.float32))
    a = jnp.exp(log_a)
    u = jnp.sqrt(-jnp.expm1(2.0 * log_a)) * (i_gate * x).astype(jnp.float32)

    def combine(left, right):
        a1, b1 = left
        a2, b2 = right
        return a1 * a2, a2 * b1 + b2

    _, h = lax.associative_scan(combine, (a, u), axis=1)
    return h.astype(x.dtype)


def moe_ffn(xn, w_router, b_router, w_gu, b_gu, w_down, b_down):
    B, S, D = xn.shape
    T = B * S
    xt = xn.reshape(T, D)
    logits = xt.astype(jnp.float32) @ w_router.astype(jnp.float32) + b_router.astype(jnp.float32)
    top_val, top_idx = lax.top_k(logits, TOP_K)
    top_w = jax.nn.softmax(top_val, axis=-1)
    e_flat = top_idx.reshape(-1).astype(jnp.int32)
    tok_flat = jnp.repeat(jnp.arange(T, dtype=jnp.int32), TOP_K)
    w_flat = top_w.reshape(-1)
    order = jnp.argsort(e_flat, stable=True)
    e_sorted = e_flat[order]
    counts = jnp.bincount(e_flat, length=N_EXPERTS)
    padded = (counts + EXPERT_BLOCK - 1) // EXPERT_BLOCK * EXPERT_BLOCK
    start = jnp.cumsum(counts) - counts
    pend = jnp.cumsum(padded)
    pstart = pend - padded
    dest = pstart[e_sorted] + (jnp.arange(T * TOP_K, dtype=jnp.int32) - start[e_sorted])
    n_rows = T * TOP_K + N_EXPERTS * EXPERT_BLOCK
    n_blocks = n_rows // EXPERT_BLOCK
    row_tok = jnp.full((n_rows,), T, jnp.int32).at[dest].set(tok_flat[order])
    row_w = jnp.zeros((n_rows,), jnp.float32).at[dest].set(w_flat[order]).astype(xn.dtype)
    blk_e = jnp.minimum(jnp.searchsorted(pend, jnp.arange(n_blocks) * EXPERT_BLOCK, side='right'),
                        N_EXPERTS - 1).astype(jnp.int32)
    x_pad = jnp.concatenate([xt, jnp.zeros((1, D), xt.dtype)], axis=0)
    xb = x_pad[row_tok].reshape(n_blocks, EXPERT_BLOCK, D)

    def expert_block(args):
        x_blk, e = args
        gu = x_blk @ w_gu[e] + b_gu[e]
        gate, up = jnp.split(gu, 2, axis=-1)
        gate = jnp.minimum(gate, SWIGLU_LIMIT)
        up = jnp.clip(up, -SWIGLU_LIMIT, SWIGLU_LIMIT)
        glu = gate * jax.nn.sigmoid(SWIGLU_ALPHA * gate)
        return ((up + 1.0) * glu) @ w_down[e] + b_down[e]

    yb = lax.map(expert_block, (xb, blk_e))
    y_rows = yb.reshape(n_rows, D) * row_w[:, None]
    y = jax.ops.segment_sum(y_rows, row_tok, num_segments=T + 1)[:T]
    return y.reshape(B, S, D)


def setup_inputs(seed: int = 0) -> dict:
    key = jax.random.key(seed)
    ks = jax.random.split(key, 32)
    L, D = DEPTH, D_MODEL

    def nrm(k, shape, s):
        return jax.random.normal(k, shape, jnp.float32) * s

    a_pow = jax.random.uniform(ks[16], (L, RNN_WIDTH), jnp.float32, LRU_A_MIN, LRU_A_MAX)
    a = a_pow ** (1.0 / RG_C)
    rg_lambda = jnp.log(a) - jnp.log1p(-a)
    return {
        "x": nrm(ks[0], (BATCH, SEQ, D), 1.0),
        "c": nrm(ks[1], (BATCH, D), 1.0),
        "ada_w": nrm(ks[2], (L, D, 6 * D), 0.2 * D ** -0.5),
        "ada_b": nrm(ks[3], (L, 6 * D), 0.02),
        "norm1_g": 1.0 + nrm(ks[4], (L, D), 0.02),
        "w_in": nrm(ks[5], (L, D, IN_WIDTH), D ** -0.5),
        "q_norm_g": 1.0 + nrm(ks[6], (L, ATTN_HEAD_DIM), 0.02),
        "k_norm_g": 1.0 + nrm(ks[7], (L, ATTN_HEAD_DIM), 0.02),
        "rel_bias": nrm(ks[8], (L, ATTN_HEADS, REL_SIZE), 0.5),
        "conv_w": nrm(ks[9], (L, CONV_W, RNN_WIDTH), CONV_W ** -0.5),
        "conv_b": nrm(ks[10], (L, RNN_WIDTH), 0.02),
        "rg_a_w": nrm(ks[11], (L, RNN_BLOCKS, RNN_BLOCK_W, RNN_BLOCK_W), RNN_BLOCK_W ** -0.5),
        "rg_a_b": nrm(ks[12], (L, RNN_BLOCKS, RNN_BLOCK_W), 0.02),
        "rg_x_w": nrm(ks[13], (L, RNN_BLOCKS, RNN_BLOCK_W, RNN_BLOCK_W), RNN_BLOCK_W ** -0.5),
        "rg_x_b": nrm(ks[14], (L, RNN_BLOCKS, RNN_BLOCK_W), 0.02),
        "rg_lambda": rg_lambda,
        "w_attn_up": nrm(ks[17], (L, ATTN_WIDTH, D), ATTN_WIDTH ** -0.5),
        "w_rnn_up": nrm(ks[18], (L, RNN_WIDTH, D), RNN_WIDTH ** -0.5),
        "w_out": nrm(ks[19], (L, D, D), D ** -0.5),
        "norm2_g": 1.0 + nrm(ks[20], (L, D), 0.02),
        "w_router": nrm(ks[21], (L, D, N_EXPERTS), D ** -0.5),
        "b_router": nrm(ks[22], (L, N_EXPERTS), 0.01),
        "w_gu": nrm(ks[23], (L, N_EXPERTS, D, 2 * D_FF), D ** -0.5),
        "b_gu": nrm(ks[24], (L, N_EXPERTS, 2 * D_FF), 0.02),
        "w_down": nrm(ks[25], (L, N_EXPERTS, D_FF, D), D_FF ** -0.5),
        "b_down": nrm(ks[26], (L, N_EXPERTS, D), 0.02),
    }


def reference(x, c, ada_w, ada_b, norm1_g, w_in, q_norm_g, k_norm_g, rel_bias, conv_w, conv_b,
              rg_a_w, rg_a_b, rg_x_w, rg_x_b, rg_lambda, w_attn_up, w_rnn_up, w_out, norm2_g,
              w_router, b_router, w_gu, b_gu, w_down, b_down):
    B, S, D = x.shape
    h = x
    c_act = jax.nn.silu(c)
    for l in range(DEPTH):
        mod = c_act @ ada_w[l] + ada_b[l]
        sh1, sc1, g1, sh2, sc2, g2 = jnp.split(mod, 6, axis=-1)

        xn = rms_norm(h, norm1_g[l]) * (1.0 + sc1[:, None, :]) + sh1[:, None, :]
        proj = xn @ w_in[l]
        q, k, v, rx, rgate, ga, gb = jnp.split(proj, SPLIT_POINTS, axis=-1)
        q = rms_norm(q.reshape(B, S, ATTN_HEADS, ATTN_HEAD_DIM), q_norm_g[l])
        k = rms_norm(k.reshape(B, S, ATTN_HEADS, ATTN_HEAD_DIM), k_norm_g[l])
        v = v.reshape(B, S, ATTN_HEADS, ATTN_HEAD_DIM)
        y_a = chunked_band_attention(q, k, v, rel_bias[l])
        xc = causal_depthwise_conv(rx, conv_w[l], conv_b[l])
        y_b = rg_lru(xc, rg_a_w[l], rg_a_b[l], rg_x_w[l], rg_x_b[l], rg_lambda[l]) * jax.nn.gelu(rgate)
        merged = jax.nn.sigmoid(ga) * (y_a @ w_attn_up[l]) + jax.nn.sigmoid(gb) * (y_b @ w_rnn_up[l])
        h = h + g1[:, None, :] * (merged @ w_out[l])

        hn = rms_norm(h, norm2_g[l]) * (1.0 + sc2[:, None, :]) + sh2[:, None, :]
        y_ffn = moe_ffn(hn, w_router[l], b_router[l], w_gu[l], b_gu[l], w_down[l], b_down[l])
        h = h + g2[:, None, :] * y_ffn
    return h
```

```python
import functools
import math

import jax
import jax.numpy as jnp
from jax import lax
from jax.experimental import pallas as pl
from jax.experimental.pallas import tpu as pltpu

F32 = jnp.float32
BF16 = jnp.bfloat16
U32 = jnp.uint32
I32 = jnp.int32

CHUNK = 64
LEFT_CHUNKS = 8
ATTN_HEADS = 8
HEAD_DIM = 128
REL_MAX = 256
REL_MIN = -(CHUNK - 1)
RNN_BLOCKS = 8
CONV_W = 4
RG_C = 8.0
N_EXPERTS = 32
TOP_K = 4
SWIGLU_LIMIT = 7.0
SWIGLU_ALPHA = 1.702
EPS = 1e-6
NEG_BIG = -1e30

LANES = 128
SUBLANES = 8
VMEM_LIMIT = 56 * 1024 * 1024

ROW_BLOCK = 256
ATTN_QB = 4 * CHUNK
ATTN_KB = ATTN_QB + LEFT_CHUNKS * CHUNK


def _cparams(*sem):
    return pltpu.CompilerParams(dimension_semantics=sem, vmem_limit_bytes=VMEM_LIMIT)


def _ada_kernel(c_ref, w_ref, b_ref, o_ref):
    c = c_ref[...]
    ca = c * jax.nn.sigmoid(c)
    o_ref[...] = jnp.dot(ca, w_ref[...], preferred_element_type=F32,
                         precision=lax.Precision.HIGHEST) + b_ref[...]


def _ada(c, ada_w, ada_b):
    d, n = ada_w.shape
    tn = 1024
    c8 = jnp.broadcast_to(c, (SUBLANES, d))
    out = pl.pallas_call(
        _ada_kernel,
        grid=(n // tn,),
        in_specs=[pl.BlockSpec((SUBLANES, d), lambda j: (0, 0)),
                  pl.BlockSpec((d, tn), lambda j: (0, j)),
                  pl.BlockSpec((1, tn), lambda j: (0, j))],
        out_specs=pl.BlockSpec((SUBLANES, tn), lambda j: (0, j)),
        out_shape=jax.ShapeDtypeStruct((SUBLANES, n), F32),
        compiler_params=_cparams("arbitrary"),
        name="ada",
    )(c8, ada_w, ada_b.reshape(1, n))
    return out[0:1]


def _gelu_tanh(x):
    return 0.5 * x * (1.0 + jnp.tanh(math.sqrt(2.0 / math.pi) * (x + 0.044715 * (x * x * x))))


def _head_rms(a, g):
    outs = []
    for hh in range(a.shape[1] // HEAD_DIM):
        s = a[:, hh * HEAD_DIM:(hh + 1) * HEAD_DIM]
        ms = jnp.mean(s * s, axis=-1, keepdims=True)
        outs.append(s * lax.rsqrt(ms + EPS) * g)
    return jnp.concatenate(outs, axis=1)


def _inproj_kernel(x_ref, g_ref, sc_ref, sh_ref, w_ref, qg_ref, kg_ref, o_ref, xn_ref, *, bounds, kinds):
    j = pl.program_id(1)

    @pl.when(j == 0)
    def _():
        x = x_ref[...]
        ms = jnp.mean(x * x, axis=-1, keepdims=True)
        xn = x * lax.rsqrt(ms + EPS) * g_ref[...]
        xn = xn * (1.0 + sc_ref[...]) + sh_ref[...]
        xn_ref[...] = xn.astype(BF16)

    acc = jnp.dot(xn_ref[...], w_ref[...], preferred_element_type=F32)
    lo = 0
    for hi, kind in zip(bounds, kinds):
        @pl.when((j >= lo) & (j < hi))
        def _(kind=kind):
            if kind == "q":
                r = _head_rms(acc, qg_ref[...]) * (1.0 / math.sqrt(HEAD_DIM))
            elif kind == "k":
                r = _head_rms(acc, kg_ref[...])
            elif kind == "gelu":
                r = _gelu_tanh(acc)
            elif kind == "sigmoid":
                r = jax.nn.sigmoid(acc)
            else:
                r = acc
            o_ref[...] = r.astype(o_ref.dtype)
        lo = hi


def _inproj(x2, g, sc, sh, w_bf, qg, kg, *, col0, groups, out_dtype, name):
    t, d = x2.shape
    tm, tn = 1024, 512
    n = sum(wd for wd, _ in groups)
    bounds, acc = [], 0
    for wd, _ in groups:
        acc += wd // tn
        bounds.append(acc)
    kinds = [k for _, k in groups]
    j0 = col0 // tn
    vec = lambda: pl.BlockSpec((1, d), lambda i, j: (0, 0))
    hvec = lambda: pl.BlockSpec((1, HEAD_DIM), lambda i, j: (0, 0))
    return pl.pallas_call(
        functools.partial(_inproj_kernel, bounds=tuple(bounds), kinds=tuple(kinds)),
        grid=(t // tm, n // tn),
        in_specs=[pl.BlockSpec((tm, d), lambda i, j: (i, 0)), vec(), vec(), vec(),
                  pl.BlockSpec((d, tn), lambda i, j: (0, j + j0)), hvec(), hvec()],
        out_specs=pl.BlockSpec((tm, tn), lambda i, j: (i, j)),
        out_shape=jax.ShapeDtypeStruct((t, n), out_dtype),
        scratch_shapes=[pltpu.VMEM((tm, d), BF16)],
        compiler_params=_cparams("arbitrary", "arbitrary"),
        name=name,
    )(x2, g, sc, sh, w_bf, qg, kg)


def _attn_kernel(q_ref, k0_ref, k1_ref, k2_ref, v0_ref, v1_ref, v2_ref, bias_ref, o_ref):
    i = pl.program_id(0)
    col = lax.broadcasted_iota(I32, (ATTN_QB, ATTN_KB), 1)
    in_seq = col >= (LEFT_CHUNKS * CHUNK - i * ATTN_QB)
    for h in range(ATTN_HEADS):
        sl = slice(h * HEAD_DIM, (h + 1) * HEAD_DIM)
        q = q_ref[:, sl]
        k = jnp.concatenate([k0_ref[:, sl], k1_ref[:, sl], k2_ref[:, sl]], axis=0)
        v = jnp.concatenate([v0_ref[:, sl], v1_ref[:, sl], v2_ref[:, sl]], axis=0)
        s = lax.dot_general(q, k, (((1,), (1,)), ((), ())), preferred_element_type=F32)
        s = jnp.where(in_seq, s + bias_ref[h], NEG_BIG)
        m = jnp.max(s, axis=-1, keepdims=True)
        p = jnp.exp(s - m)
        l = jnp.sum(p, axis=-1, keepdims=True)
        o = jnp.dot(p.astype(BF16), v, preferred_element_type=F32) / l
        o_ref[:, sl] = o.astype(o_ref.dtype)


def _attn(qkv, bias_mask):
    t = qkv.shape[0]
    w = ATTN_HEADS * HEAD_DIM
    qb = ATTN_QB
    blk = lambda f: pl.BlockSpec((qb, w), f)
    return pl.pallas_call(
        _attn_kernel,
        grid=(t // qb,),
        in_specs=[blk(lambda i: (i, 0)),
                  blk(lambda i: (jnp.maximum(i - 2, 0), 1)), blk(lambda i: (jnp.maximum(i - 1, 0), 1)),
                  blk(lambda i: (i, 1)),
                  blk(lambda i: (jnp.maximum(i - 2, 0), 2)), blk(lambda i: (jnp.maximum(i - 1, 0), 2)),
                  blk(lambda i: (i, 2)),
                  pl.BlockSpec((ATTN_HEADS, qb, ATTN_KB), lambda i: (0, 0, 0))],
        out_specs=blk(lambda i: (i, 0)),
        out_shape=jax.ShapeDtypeStruct((t, w), BF16),
        compiler_params=_cparams("arbitrary"),
        name="attn",
    )(qkv, qkv, qkv, qkv, qkv, qkv, qkv, bias_mask)


def _attn_bias_mask(rel_bias):
    r = jnp.arange(ATTN_QB)[:, None]
    c = jnp.arange(ATTN_KB)[None, :]
    dist = r + LEFT_CHUNKS * CHUNK - c
    rel_idx = jnp.clip(dist, REL_MIN, REL_MAX) - REL_MIN
    visible = (c // CHUNK >= r // CHUNK) & (c // CHUNK <= r // CHUNK + LEFT_CHUNKS)
    return jnp.where(visible[None], rel_bias[:, rel_idx], NEG_BIG).astype(F32)


def _rglru_kernel(rx_ref, gate_ref, cw_ref, cb_ref, wa_ref, ba_ref, wx_ref, bx_ref, lam_ref,
                  o_ref, xbuf, a_s, u_s, hc, *, tt):
    i = pl.program_id(0)
    width = rx_ref.shape[1]
    bw = width // RNN_BLOCKS

    @pl.when(i == 0)
    def _():
        xbuf[0:SUBLANES, :] = jnp.zeros((SUBLANES, width), F32)
        hc[...] = jnp.zeros(hc.shape, F32)

    xbuf[SUBLANES:SUBLANES + tt, :] = rx_ref[...]
    xc = cb_ref[...] + cw_ref[CONV_W - 1:CONV_W, :] * xbuf[SUBLANES:SUBLANES + tt, :]
    for d in range(1, CONV_W):
        xc = xc + cw_ref[CONV_W - 1 - d:CONV_W - d, :] * xbuf[SUBLANES - d:SUBLANES - d + tt, :]
    tail = xbuf[tt:tt + SUBLANES, :]
    xbuf[0:SUBLANES, :] = tail

    z = -lam_ref[...]
    softplus = jnp.maximum(z, 0.0) + jnp.log1p(jnp.exp(-jnp.abs(z)))
    xcb = xc.astype(BF16)
    for b in range(RNN_BLOCKS):
        sl = slice(b * bw, (b + 1) * bw)
        xb_ = xcb[:, sl]
        r = jax.nn.sigmoid(jnp.dot(xb_, wa_ref[b], preferred_element_type=F32) + ba_ref[:, sl])
        ig = jax.nn.sigmoid(jnp.dot(xb_, wx_ref[b], preferred_element_type=F32) + bx_ref[:, sl])
        log_a = (-RG_C) * r * softplus[:, sl]
        a = jnp.exp(log_a)
        th = jnp.tanh(log_a)
        one_m_a2 = (-2.0) * th / (1.0 - th)
        a_s[:, sl] = a
        u_s[:, sl] = jnp.sqrt(one_m_a2) * (ig * xc[:, sl])

    row = lax.broadcasted_iota(I32, (SUBLANES, width), 0)

    def body(g, h):
        r0 = pl.multiple_of(g * SUBLANES, SUBLANES)
        a = a_s[pl.ds(r0, SUBLANES), :]
        u = u_s[pl.ds(r0, SUBLANES), :]
        for s in (1, 2, 4):
            a_sh = pltpu.roll(a, s, 0)
            u_sh = pltpu.roll(u, s, 0)
            keep = row >= s
            u = jnp.where(keep, a * u_sh + u, u)
            a = jnp.where(keep, a * a_sh, a)
        hh = a * h + u
        u_s[pl.ds(r0, SUBLANES), :] = hh
        return hh[SUBLANES - 1:SUBLANES, :]

    h_last = lax.fori_loop(0, tt // SUBLANES, body, hc[0:1, :])
    hc[0:1, :] = h_last
    o_ref[...] = (u_s[...] * gate_ref[...]).astype(o_ref.dtype)


def _rglru(rest, conv_w, conv_b, wa_bf, ba, wx_bf, bx, lam):
    t = rest.shape[0]
    width = conv_w.shape[1]
    bw = width // RNN_BLOCKS
    tt = 512
    vec = lambda: pl.BlockSpec((1, width), lambda i: (0, 0))
    wblk = lambda: pl.BlockSpec((RNN_BLOCKS, bw, bw), lambda i: (0, 0, 0))
    return pl.pallas_call(
        functools.partial(_rglru_kernel, tt=tt),
        grid=(t // tt,),
        in_specs=[pl.BlockSpec((tt, width), lambda i: (i, 0)),
                  pl.BlockSpec((tt, width), lambda i: (i, 1)),
                  pl.BlockSpec((CONV_W, width), lambda i: (0, 0)), vec(),
                  wblk(), vec(), wblk(), vec(), vec()],
        out_specs=pl.BlockSpec((tt, width), lambda i: (i, 0)),
        out_shape=jax.ShapeDtypeStruct((t, width), BF16),
        scratch_shapes=[pltpu.VMEM((tt + SUBLANES, width), F32),
                        pltpu.VMEM((tt, width), F32),
                        pltpu.VMEM((tt, width), F32),
                        pltpu.VMEM((SUBLANES, width), F32)],
        compiler_params=_cparams("arbitrary"),
        name="rglru",
    )(rest, rest, conv_w, conv_b, wa_bf, ba, wx_bf, bx, lam)


def _mixer_out_kernel(ya_ref, yb_ref, sga_ref, sgb_ref, x_ref, wa_ref, wr_ref, wo_ref,
                      g1_ref, n2_ref, sc2_ref, sh2_ref, wrt_ref, brt_ref,
                      h1_ref, hnp_ref, lg_ref):
    up_a = jnp.dot(ya_ref[...], wa_ref[...], preferred_element_type=F32)
    up_b = jnp.dot(yb_ref[...], wr_ref[...], preferred_element_type=F32)
    merged = sga_ref[...] * up_a + sgb_ref[...] * up_b
    proj = jnp.dot(merged.astype(BF16), wo_ref[...], preferred_element_type=F32)
    h1 = x_ref[...] + g1_ref[...] * proj
    h1_ref[...] = h1
    ms = jnp.mean(h1 * h1, axis=-1, keepdims=True)
    hn = h1 * lax.rsqrt(ms + EPS) * n2_ref[...]
    hn = hn * (1.0 + sc2_ref[...]) + sh2_ref[...]
    lg_ref[...] = jnp.dot(hn, wrt_ref[...], preferred_element_type=F32,
                          precision=lax.Precision.HIGHEST) + brt_ref[...]
    half = hn.shape[1] // 2
    hb = hn.astype(BF16).astype(F32)
    lo = pltpu.bitcast(hb[:, :half], U32) >> 16
    hi = pltpu.bitcast(hb[:, half:], U32) & jnp.uint32(0xFFFF0000)
    hnp_ref[...] = hi | lo


def _mixer_out(ya, yb, rest, x2, wa_bf, wr_bf, wo_bf, g1, n2g, sc2, sh2, w_router, b_router):
    t, d = x2.shape
    aw = ya.shape[1]
    rw = yb.shape[1]
    ne = w_router.shape[1]
    tm = 256
    ga_blk = (2 * rw) // d
    const = lambda shape: pl.BlockSpec(shape, lambda i: (0,) * len(shape), pipeline_mode=pl.Buffered(1))
    return pl.pallas_call(
        _mixer_out_kernel,
        grid=(t // tm,),
        in_specs=[pl.BlockSpec((tm, aw), lambda i: (i, 0)),
                  pl.BlockSpec((tm, rw), lambda i: (i, 0)),
                  pl.BlockSpec((tm, d), lambda i: (i, ga_blk)),
                  pl.BlockSpec((tm, d), lambda i: (i, ga_blk + 1)),
                  pl.BlockSpec((tm, d), lambda i: (i, 0)),
                  const((aw, d)), const((rw, d)), const((d, d)),
                  const((1, d)), const((1, d)), const((1, d)), const((1, d)),
                  const((d, ne)), const((1, ne))],
        out_specs=[pl.BlockSpec((tm, d), lambda i: (i, 0)),
                   pl.BlockSpec((tm, d // 2), lambda i: (i, 0)),
                   pl.BlockSpec((tm, ne), lambda i: (i, 0))],
        out_shape=[jax.ShapeDtypeStruct((t, d), F32),
                   jax.ShapeDtypeStruct((t, d // 2), U32),
                   jax.ShapeDtypeStruct((t, ne), F32)],
        compiler_params=_cparams("arbitrary"),
        name="mixer_out",
    )(ya, yb, rest, rest, x2, wa_bf, wr_bf, wo_bf, g1, n2g, sc2, sh2, w_router, b_router)


def _route_kernel(lg_ref, idx_ref, w_ref, pos_ref, cnt_ref, carry, *, rt):
    i = pl.program_id(0)
    ne = lg_ref.shape[1]

    @pl.when(i == 0)
    def _():
        carry[...] = jnp.zeros(carry.shape, F32)

    l = lg_ref[...]
    lane = lax.broadcasted_iota(I32, (rt, ne), 1).astype(F32)
    vals, idxs, hots = [], [], []
    for _ in range(TOP_K):
        m = jnp.max(l, axis=-1, keepdims=True)
        ik = jnp.min(jnp.where(l == m, lane, float(ne)), axis=-1, keepdims=True)
        hot = lane == ik
        vals.append(m)
        idxs.append(ik)
        hots.append(hot)
        l = jnp.where(hot, -jnp.inf, l)
    es = [jnp.exp(v - vals[0]) for v in vals]
    tot = es[0] + es[1] + es[2] + es[3]
    sel = jnp.zeros((rt, ne), F32)
    for hot in hots:
        sel = sel + hot.astype(F32)
    rr = lax.broadcasted_iota(I32, (rt, rt), 0)
    cc = lax.broadcasted_iota(I32, (rt, rt), 1)
    tri = (cc < rr).astype(BF16)
    before = jnp.dot(tri, sel.astype(BF16), preferred_element_type=F32) + carry[0:1, :]
    out_lane = lax.broadcasted_iota(I32, (rt, LANES), 1)
    idx_o = jnp.zeros((rt, LANES), F32)
    w_o = jnp.zeros((rt, LANES), F32)
    pos_o = jnp.zeros((rt, LANES), F32)
    for k in range(TOP_K):
        pk = jnp.sum(jnp.where(hots[k], before, 0.0), axis=-1, keepdims=True)
        idx_o = jnp.where(out_lane == k, idxs[k], idx_o)
        w_o = jnp.where(out_lane == k, es[k] / tot, w_o)
        pos_o = jnp.where(out_lane == k, pk, pos_o)
    idx_ref[...] = idx_o.astype(I32)
    w_ref[...] = w_o
    pos_ref[...] = pos_o.astype(I32)
    carry[0:1, :] = carry[0:1, :] + jnp.sum(sel, axis=0, keepdims=True)
    cnt_ref[...] = carry[...]


def _route(logits):
    t, ne = logits.shape
    rt = 512
    wide = lambda dt: jax.ShapeDtypeStruct((t, LANES), dt)
    return pl.pallas_call(
        functools.partial(_route_kernel, rt=rt),
        grid=(t // rt,),
        in_specs=[pl.BlockSpec((rt, ne), lambda i: (i, 0))],
        out_specs=[pl.BlockSpec((rt, LANES), lambda i: (i, 0)),
                   pl.BlockSpec((rt, LANES), lambda i: (i, 0)),
                   pl.BlockSpec((rt, LANES), lambda i: (i, 0)),
                   pl.BlockSpec((SUBLANES, ne), lambda i: (0, 0))],
        out_shape=[wide(I32), wide(F32), wide(I32), jax.ShapeDtypeStruct((SUBLANES, ne), F32)],
        scratch_shapes=[pltpu.VMEM((SUBLANES, ne), F32)],
        compiler_params=_cparams("arbitrary"),
        name="route",
    )(logits)


def _dispatch_kernel(zblk_ref, dest_ref, hn_ref, xb_ref, zbuf, sem, zsem, *, dt):
    i = pl.program_id(0)

    @pl.when(i == 0)
    def _():
        zbuf[...] = jnp.zeros(zbuf.shape, zbuf.dtype)

        def zero_copy(j):
            row0 = pl.multiple_of(zblk_ref[j] * ROW_BLOCK, ROW_BLOCK)
            return pltpu.make_async_copy(zbuf, xb_ref.at[pl.ds(row0, ROW_BLOCK)], zsem)

        def start(j, carry):
            @pl.when(zblk_ref[j] >= 0)
            def _():
                zero_copy(j).start()
            return carry

        def wait(j, carry):
            @pl.when(zblk_ref[j] >= 0)
            def _():
                zero_copy(j).wait()
            return carry

        lax.fori_loop(0, 2 * N_EXPERTS, start, 0)
        lax.fori_loop(0, 2 * N_EXPERTS, wait, 0)

    def issue(r, carry):
        for k in range(TOP_K):
            d = dest_ref[0, 0, r * TOP_K + k]
            pltpu.make_async_copy(hn_ref.at[pl.ds(r, 1)], xb_ref.at[pl.ds(d, 1)], sem).start()
        return carry

    lax.fori_loop(0, dt, issue, 0)
    for k in range(TOP_K):
        pltpu.make_async_copy(hn_ref, xb_ref.at[pl.ds(0, dt)], sem).wait()


def _dispatch(hn_packed, dest, zblk, n_rows):
    t, hw = hn_packed.shape
    dt = 128
    dest3 = dest.reshape(t // dt, 1, dt * TOP_K)
    grid_spec = pltpu.PrefetchScalarGridSpec(
        num_scalar_prefetch=1,
        grid=(t // dt,),
        in_specs=[pl.BlockSpec((1, 1, dt * TOP_K), lambda i, zb: (i, 0, 0), memory_space=pltpu.SMEM),
                  pl.BlockSpec((dt, hw), lambda i, zb: (i, 0))],
        out_specs=pl.BlockSpec(memory_space=pl.ANY),
        scratch_shapes=[pltpu.VMEM((ROW_BLOCK, hw), U32),
                        pltpu.SemaphoreType.DMA(()),
                        pltpu.SemaphoreType.DMA(())],
    )
    return pl.pallas_call(
        functools.partial(_dispatch_kernel, dt=dt),
        grid_spec=grid_spec,
        out_shape=jax.ShapeDtypeStruct((n_rows, hw), U32),
        compiler_params=_cparams("arbitrary"),
        name="dispatch",
    )(zblk, dest3, hn_packed)


def _unpack_rows(p):
    lo = pltpu.bitcast(p << 16, F32).astype(BF16)
    hi = pltpu.bitcast(p & jnp.uint32(0xFFFF0000), F32).astype(BF16)
    return jnp.concatenate([lo, hi], axis=1)


def _first_of_expert(be_ref, m):
    prev = be_ref[jnp.maximum(m - 1, 0)]
    return (m == 0) | (be_ref[m] != prev)


def _moe_gu_kernel(be_ref, na_ref, x_ref, wg_ref, wu_ref, bg_ref, bu_ref, o_ref, wgb, wub):
    m = pl.program_id(1)
    active = m < na_ref[0]

    @pl.when(active & _first_of_expert(be_ref, m))
    def _():
        wgb[...] = wg_ref[...].astype(BF16)
        wub[...] = wu_ref[...].astype(BF16)

    @pl.when(active)
    def _():
        x = _unpack_rows(x_ref[...])
        g = jnp.dot(x, wgb[...], preferred_element_type=F32) + bg_ref[...]
        u = jnp.dot(x, wub[...], preferred_element_type=F32) + bu_ref[...]
        g = jnp.minimum(g, SWIGLU_LIMIT)
        u = jnp.clip(u, -SWIGLU_LIMIT, SWIGLU_LIMIT)
        glu = g * jax.nn.sigmoid(SWIGLU_ALPHA * g)
        o_ref[...] = ((u + 1.0) * glu).astype(o_ref.dtype)

    @pl.when(jnp.logical_not(active))
    def _():
        o_ref[...] = jnp.zeros(o_ref.shape, o_ref.dtype)


def _moe_gu(xb, w_gu, b_gu, blk_e, n_active):
    n_rows, hw = xb.shape
    ne, d, f2 = w_gu.shape
    f = f2 // 2
    tn = 512
    nf = f // tn
    nb = n_rows // ROW_BLOCK
    mc = lambda m, na: jnp.minimum(m, na[0] - 1)
    grid_spec = pltpu.PrefetchScalarGridSpec(
        num_scalar_prefetch=2,
        grid=(nf, nb),
        in_specs=[pl.BlockSpec((ROW_BLOCK, hw), lambda n, m, be, na: (mc(m, na), 0)),
                  pl.BlockSpec((None, d, tn), lambda n, m, be, na: (be[mc(m, na)], 0, n)),
                  pl.BlockSpec((None, d, tn), lambda n, m, be, na: (be[mc(m, na)], 0, n + nf)),
                  pl.BlockSpec((None, 1, tn), lambda n, m, be, na: (be[mc(m, na)], 0, n)),
                  pl.BlockSpec((None, 1, tn), lambda n, m, be, na: (be[mc(m, na)], 0, n + nf))],
        out_specs=pl.BlockSpec((ROW_BLOCK, tn), lambda n, m, be, na: (m, n)),
        scratch_shapes=[pltpu.VMEM((d, tn), BF16), pltpu.VMEM((d, tn), BF16)],
    )
    b3 = b_gu.reshape(ne, 1, f2)
    return pl.pallas_call(
        _moe_gu_kernel,
        grid_spec=grid_spec,
        out_shape=jax.ShapeDtypeStruct((n_rows, f), BF16),
        compiler_params=_cparams("arbitrary", "arbitrary"),
        name="moe_gu",
    )(blk_e, n_active, xb, w_gu, w_gu, b3, b3)


def _moe_down_kernel(be_ref, na_ref, a_ref, w_ref, b_ref, o_ref, wb):
    m = pl.program_id(1)
    active = m < na_ref[0]

    @pl.when(active & _first_of_expert(be_ref, m))
    def _():
        wb[...] = w_ref[...].astype(BF16)

    @pl.when(active)
    def _():
        o_ref[...] = jnp.dot(a_ref[...], wb[...], preferred_element_type=F32) + b_ref[...]

    @pl.when(jnp.logical_not(active))
    def _():
        o_ref[...] = jnp.zeros(o_ref.shape, o_ref.dtype)


def _moe_down(act, w_down, b_down, blk_e, n_active):
    n_rows, f = act.shape
    ne, _, d = w_down.shape
    tn = 1024
    nd = d // tn
    nb = n_rows // ROW_BLOCK
    mc = lambda m, na: jnp.minimum(m, na[0] - 1)
    grid_spec = pltpu.PrefetchScalarGridSpec(
        num_scalar_prefetch=2,
        grid=(nd, nb),
        in_specs=[pl.BlockSpec((ROW_BLOCK, f), lambda n, m, be, na: (mc(m, na), 0)),
                  pl.BlockSpec((None, f, tn), lambda n, m, be, na: (be[mc(m, na)], 0, n)),
                  pl.BlockSpec((None, 1, tn), lambda n, m, be, na: (be[mc(m, na)], 0, n))],
        out_specs=pl.BlockSpec((ROW_BLOCK, tn), lambda n, m, be, na: (m, n)),
        scratch_shapes=[pltpu.VMEM((f, tn), BF16)],
    )
    return pl.pallas_call(
        _moe_down_kernel,
        grid_spec=grid_spec,
        out_shape=jax.ShapeDtypeStruct((n_rows, d), F32),
        compiler_params=_cparams("arbitrary", "arbitrary"),
        name="moe_down",
    )(blk_e, n_active, act, w_down, b_down.reshape(ne, 1, d))


def _combine_kernel(dest_ref, w_ref, h1_ref, g2_ref, yb_ref, o_ref, buf, sem, *, ct):
    def issue(r, carry):
        for k in range(TOP_K):
            d = dest_ref[0, 0, r * TOP_K + k]
            pltpu.make_async_copy(yb_ref.at[pl.ds(d, 1)], buf.at[k, pl.ds(r, 1)], sem).start()
        return carry

    lax.fori_loop(0, ct, issue, 0)
    for k in range(TOP_K):
        pltpu.make_async_copy(yb_ref.at[pl.ds(0, ct)], buf.at[k], sem).wait()
    w = w_ref[...]
    acc = buf[0] * w[:, 0:1]
    for k in range(1, TOP_K):
        acc = acc + buf[k] * w[:, k:k + 1]
    o_ref[...] = h1_ref[...] + g2_ref[...] * acc


def _combine(dest, top_w, h1, g2, yb):
    t, d = h1.shape
    ct = 128
    dest3 = dest.reshape(t // ct, 1, ct * TOP_K)
    return pl.pallas_call(
        functools.partial(_combine_kernel, ct=ct),
        grid=(t // ct,),
        in_specs=[pl.BlockSpec((1, 1, ct * TOP_K), lambda i: (i, 0, 0), memory_space=pltpu.SMEM),
                  pl.BlockSpec((ct, LANES), lambda i: (i, 0)),
                  pl.BlockSpec((ct, d), lambda i: (i, 0)),
                  pl.BlockSpec((1, d), lambda i: (0, 0)),
                  pl.BlockSpec(memory_space=pl.ANY)],
        out_specs=pl.BlockSpec((ct, d), lambda i: (i, 0)),
        out_shape=jax.ShapeDtypeStruct((t, d), F32),
        scratch_shapes=[pltpu.VMEM((TOP_K, ct, d), F32), pltpu.SemaphoreType.DMA(())],
        compiler_params=_cparams("arbitrary"),
        name="combine",
    )(dest3, top_w, h1, g2, yb)


def kernel(x, c, ada_w, ada_b, norm1_g, w_in, q_norm_g, k_norm_g, rel_bias, conv_w, conv_b,
           rg_a_w, rg_a_b, rg_x_w, rg_x_b, rg_lambda, w_attn_up, w_rnn_up, w_out, norm2_g,
           w_router, b_router, w_gu, b_gu, w_down, b_down):
    b, s, d = x.shape
    assert b == 1 and ada_w.shape[0] == 1, "single batch row, single layer"
    t = s
    x2 = x.reshape(t, d)
    aw = ATTN_HEADS * HEAD_DIM
    rw = conv_w.shape[2]

    mod = _ada(c, ada_w[0], ada_b[0])
    sh1, sc1, g1, sh2, sc2, g2 = [mod[:, k * d:(k + 1) * d] for k in range(6)]

    w_in_bf = w_in[0].astype(BF16)
    qg = q_norm_g[0].reshape(1, HEAD_DIM)
    kg = k_norm_g[0].reshape(1, HEAD_DIM)
    n1g = norm1_g[0].reshape(1, d)
    qkv = _inproj(x2, n1g, sc1, sh1, w_in_bf, qg, kg, col0=0,
                  groups=[(aw, "q"), (aw, "k"), (aw, "v")], out_dtype=BF16, name="inproj_qkv")
    rest = _inproj(x2, n1g, sc1, sh1, w_in_bf, qg, kg, col0=3 * aw,
                   groups=[(rw, "id"), (rw, "gelu"), (2 * d, "sigmoid")], out_dtype=F32,
                   name="inproj_rest")

    ya = _attn(qkv, _attn_bias_mask(rel_bias[0]))
    yb = _rglru(rest, conv_w[0], conv_b[0].reshape(1, rw),
                rg_a_w[0].astype(BF16), rg_a_b[0].reshape(1, rw),
                rg_x_w[0].astype(BF16), rg_x_b[0].reshape(1, rw), rg_lambda[0].reshape(1, rw))

    h1, hn_packed, logits = _mixer_out(
        ya, yb, rest, x2, w_attn_up[0].astype(BF16), w_rnn_up[0].astype(BF16), w_out[0].astype(BF16),
        g1, norm2_g[0].reshape(1, d), sc2, sh2, w_router[0], b_router[0].reshape(1, N_EXPERTS))

    idx_w, topw_w, pos_w, cnt = _route(logits)
    idx = idx_w[:, :TOP_K]
    pos = pos_w[:, :TOP_K]
    counts = cnt[0].astype(I32)
    padded = (counts + ROW_BLOCK - 1) // ROW_BLOCK * ROW_BLOCK
    pend = jnp.cumsum(padded)
    pstart = pend - padded
    dest = (pstart[idx] + pos).reshape(-1)
    n_blocks = (t * TOP_K) // ROW_BLOCK + N_EXPERTS
    n_rows = n_blocks * ROW_BLOCK
    blk_e = jnp.minimum(jnp.searchsorted(pend, jnp.arange(n_blocks, dtype=I32) * ROW_BLOCK, side="right"),
                        N_EXPERTS - 1).astype(I32)
    n_active = (pend[-1:] // ROW_BLOCK).astype(I32)
    last_blk = jnp.where(padded > 0, pend // ROW_BLOCK - 1, -1)
    spare = n_active[0] + jnp.arange(N_EXPERTS, dtype=I32)
    zblk = jnp.concatenate([last_blk, jnp.where(spare < n_blocks, spare, -1)]).astype(I32)

    xb = _dispatch(hn_packed, dest, zblk, n_rows)
    act = _moe_gu(xb, w_gu[0], b_gu[0], blk_e, n_active)
    y_rows = _moe_down(act, w_down[0], b_down[0], blk_e, n_active)
    out = _combine(dest, topw_w, h1, g2, y_rows)
    return out.reshape(b, s, d)
```

```python
import functools
import math

import jax
import jax.numpy as jnp
from jax import lax
from jax.experimental import pallas as pl
from jax.experimental.pallas import tpu as pltpu

F32 = jnp.float32
BF16 = jnp.bfloat16
U32 = jnp.uint32
I32 = jnp.int32

CHUNK = 64
LEFT_CHUNKS = 8
ATTN_HEADS = 8
HEAD_DIM = 128
REL_MAX = 256
REL_MIN = -(CHUNK - 1)
RNN_BLOCKS = 8
CONV_W = 4
RG_C = 8.0
N_EXPERTS = 32
TOP_K = 4
SWIGLU_LIMIT = 7.0
SWIGLU_ALPHA = 1.702
EPS = 1e-6
NEG_BIG = -1e30

LANES = 128
SUBLANES = 8
VMEM_LIMIT = 56 * 1024 * 1024

ROW_BLOCK = 512
ATTN_QB = 4 * CHUNK
ATTN_KB = ATTN_QB + LEFT_CHUNKS * CHUNK
ATTN_TAB = ATTN_KB + ATTN_QB
CHUNK_SHIFT = CHUNK.bit_length() - 1
assert 1 << CHUNK_SHIFT == CHUNK


def _cparams(*sem):
    return pltpu.CompilerParams(dimension_semantics=sem, vmem_limit_bytes=VMEM_LIMIT)


def _ada_kernel(c_ref, w_ref, b_ref, o_ref):
    c = c_ref[...]
    ca = c * jax.nn.sigmoid(c)
    o_ref[...] = jnp.dot(ca, w_ref[...], preferred_element_type=F32,
                         precision=lax.Precision.HIGHEST) + b_ref[...]


def _ada(c, ada_w, ada_b):
    d, n = ada_w.shape
    tn = 1024
    c8 = jnp.broadcast_to(c, (SUBLANES, d))
    out = pl.pallas_call(
        _ada_kernel,
        grid=(n // tn,),
        in_specs=[pl.BlockSpec((SUBLANES, d), lambda j: (0, 0)),
                  pl.BlockSpec((d, tn), lambda j: (0, j)),
                  pl.BlockSpec((1, tn), lambda j: (0, j))],
        out_specs=pl.BlockSpec((SUBLANES, tn), lambda j: (0, j)),
        out_shape=jax.ShapeDtypeStruct((SUBLANES, n), F32),
        compiler_params=_cparams("arbitrary"),
        name="ada",
    )(c8, ada_w, ada_b.reshape(1, n))
    return out[0:1]


def _gelu_tanh(x):
    return 0.5 * x * (1.0 + jnp.tanh(math.sqrt(2.0 / math.pi) * (x + 0.044715 * (x * x * x))))


def _head_rms(a, g):
    outs = []
    for hh in range(a.shape[1] // HEAD_DIM):
        s = a[:, hh * HEAD_DIM:(hh + 1) * HEAD_DIM]
        ms = jnp.mean(s * s, axis=-1, keepdims=True)
        outs.append(s * lax.rsqrt(ms + EPS) * g)
    return jnp.concatenate(outs, axis=1)


def _inproj_kernel(x_ref, g_ref, sc_ref, sh_ref, w_ref, qg_ref, kg_ref, o_ref, xn_ref, *, bounds, kinds):
    j = pl.program_id(1)

    @pl.when(j == 0)
    def _():
        x = x_ref[...]
        ms = jnp.mean(x * x, axis=-1, keepdims=True)
        xn = x * lax.rsqrt(ms + EPS) * g_ref[...]
        xn = xn * (1.0 + sc_ref[...]) + sh_ref[...]
        xn_ref[...] = xn.astype(BF16)

    acc = jnp.dot(xn_ref[...], w_ref[...], preferred_element_type=F32)
    lo = 0
    for hi, kind in zip(bounds, kinds):
        @pl.when((j >= lo) & (j < hi))
        def _(kind=kind):
            if kind == "q":
                r = _head_rms(acc, qg_ref[...]) * (1.0 / math.sqrt(HEAD_DIM))
            elif kind == "k":
                r = _head_rms(acc, kg_ref[...])
            elif kind == "gelu":
                r = _gelu_tanh(acc)
            elif kind == "sigmoid":
                r = jax.nn.sigmoid(acc)
            else:
                r = acc
            o_ref[...] = r.astype(o_ref.dtype)
        lo = hi


def _inproj(x2, g, sc, sh, w_bf, qg, kg, *, col0, groups, out_dtype, name):
    t, d = x2.shape
    tm, tn = 1024, 512
    n = sum(wd for wd, _ in groups)
    bounds, acc = [], 0
    for wd, _ in groups:
        acc += wd // tn
        bounds.append(acc)
    kinds = [k for _, k in groups]
    j0 = col0 // tn
    vec = lambda: pl.BlockSpec((1, d), lambda i, j: (0, 0))
    hvec = lambda: pl.BlockSpec((1, HEAD_DIM), lambda i, j: (0, 0))
    return pl.pallas_call(
        functools.partial(_inproj_kernel, bounds=tuple(bounds), kinds=tuple(kinds)),
        grid=(t // tm, n // tn),
        in_specs=[pl.BlockSpec((tm, d), lambda i, j: (i, 0)), vec(), vec(), vec(),
                  pl.BlockSpec((d, tn), lambda i, j: (0, j + j0)), hvec(), hvec()],
        out_specs=pl.BlockSpec((tm, tn), lambda i, j: (i, j)),
        out_shape=jax.ShapeDtypeStruct((t, n), out_dtype),
        scratch_shapes=[pltpu.VMEM((tm, d), BF16)],
        compiler_params=_cparams("arbitrary", "arbitrary"),
        name=name,
    )(x2, g, sc, sh, w_bf, qg, kg)


def _attn_kernel(q_ref, k0_ref, k1_ref, k2_ref, v0_ref, v1_ref, v2_ref, ftab_ref, o_ref, bias_ref):
    i = pl.program_id(0)
    col = lax.broadcasted_iota(I32, (ATTN_QB, ATTN_KB), 1)

    @pl.when(i == 0)
    def _():
        row = lax.broadcasted_iota(I32, (ATTN_QB, ATTN_KB), 0)
        cq = row >> CHUNK_SHIFT
        ck = col >> CHUNK_SHIFT
        visible = (ck >= cq) & (ck <= cq + LEFT_CHUNKS)
        for h in range(ATTN_HEADS):
            g = jnp.broadcast_to(ftab_ref[h:h + 1, :], (ATTN_QB, ATTN_TAB))
            y = pltpu.roll(g, 0, 1, stride=1, stride_axis=0)
            bias_ref[h] = jnp.where(visible, y[:, :ATTN_KB], NEG_BIG)

    in_seq = col >= (LEFT_CHUNKS * CHUNK - i * ATTN_QB)
    for h in range(ATTN_HEADS):
        sl = slice(h * HEAD_DIM, (h + 1) * HEAD_DIM)
        q = q_ref[:, sl]
        k = jnp.concatenate([k0_ref[:, sl], k1_ref[:, sl], k2_ref[:, sl]], axis=0)
        v = jnp.concatenate([v0_ref[:, sl], v1_ref[:, sl], v2_ref[:, sl]], axis=0)
        s = lax.dot_general(q, k, (((1,), (1,)), ((), ())), preferred_element_type=F32)
        s = jnp.where(in_seq, s + bias_ref[h], NEG_BIG)
        m = jnp.max(s, axis=-1, keepdims=True)
        p = jnp.exp(s - m)
        l = jnp.sum(p, axis=-1, keepdims=True)
        o = jnp.dot(p.astype(BF16), v, preferred_element_type=F32) / l
        o_ref[:, sl] = o.astype(o_ref.dtype)


def _attn(qkv, ftab):
    t = qkv.shape[0]
    w = ATTN_HEADS * HEAD_DIM
    qb = ATTN_QB
    blk = lambda f: pl.BlockSpec((qb, w), f)
    return pl.pallas_call(
        _attn_kernel,
        grid=(t // qb,),
        in_specs=[blk(lambda i: (i, 0)),
                  blk(lambda i: (jnp.maximum(i - 2, 0), 1)), blk(lambda i: (jnp.maximum(i - 1, 0), 1)),
                  blk(lambda i: (i, 1)),
                  blk(lambda i: (jnp.maximum(i - 2, 0), 2)), blk(lambda i: (jnp.maximum(i - 1, 0), 2)),
                  blk(lambda i: (i, 2)),
                  pl.BlockSpec((ATTN_HEADS, ATTN_TAB), lambda i: (0, 0))],
        out_specs=blk(lambda i: (i, 0)),
        out_shape=jax.ShapeDtypeStruct((t, w), BF16),
        scratch_shapes=[pltpu.VMEM((ATTN_HEADS, qb, ATTN_KB), F32)],
        compiler_params=_cparams("arbitrary"),
        name="attn",
    )(qkv, qkv, qkv, qkv, qkv, qkv, qkv, ftab)


def _attn_distance_table(rel_bias):
    pad = LEFT_CHUNKS * CHUNK
    n_far = pad - REL_MAX + 1
    n_mid = REL_MAX - REL_MIN
    n_ahead = ATTN_KB - n_far - n_mid
    far = rel_bias[:, -1:]
    return jnp.concatenate([jnp.broadcast_to(far, (ATTN_HEADS, n_far)),
                            rel_bias[:, :n_mid][:, ::-1],
                            jnp.broadcast_to(rel_bias[:, :1], (ATTN_HEADS, n_ahead)),
                            jnp.broadcast_to(far, (ATTN_HEADS, ATTN_TAB - ATTN_KB))], axis=1)


def _rglru_kernel(rx_ref, gate_ref, cw_ref, cb_ref, wa_ref, ba_ref, wx_ref, bx_ref, lam_ref,
                  o_ref, xbuf, a_s, u_s, hc, *, tt):
    i = pl.program_id(0)
    width = rx_ref.shape[1]
    bw = width // RNN_BLOCKS

    @pl.when(i == 0)
    def _():
        xbuf[0:SUBLANES, :] = jnp.zeros((SUBLANES, width), F32)
        hc[...] = jnp.zeros(hc.shape, F32)

    xbuf[SUBLANES:SUBLANES + tt, :] = rx_ref[...]
    xc = cb_ref[...] + cw_ref[CONV_W - 1:CONV_W, :] * xbuf[SUBLANES:SUBLANES + tt, :]
    for d in range(1, CONV_W):
        xc = xc + cw_ref[CONV_W - 1 - d:CONV_W - d, :] * xbuf[SUBLANES - d:SUBLANES - d + tt, :]
    tail = xbuf[tt:tt + SUBLANES, :]
    xbuf[0:SUBLANES, :] = tail

    z = -lam_ref[...]
    softplus = jnp.maximum(z, 0.0) + jnp.log1p(jnp.exp(-jnp.abs(z)))
    xcb = xc.astype(BF16)
    for b in range(RNN_BLOCKS):
        sl = slice(b * bw, (b + 1) * bw)
        xb_ = xcb[:, sl]
        r = jax.nn.sigmoid(jnp.dot(xb_, wa_ref[b], preferred_element_type=F32) + ba_ref[:, sl])
        ig = jax.nn.sigmoid(jnp.dot(xb_, wx_ref[b], preferred_element_type=F32) + bx_ref[:, sl])
        log_a = (-RG_C) * r * softplus[:, sl]
        a = jnp.exp(log_a)
        th = jnp.tanh(log_a)
        one_m_a2 = (-2.0) * th / (1.0 - th)
        a_s[:, sl] = a
        u_s[:, sl] = jnp.sqrt(one_m_a2) * (ig * xc[:, sl])

    row = lax.broadcasted_iota(I32, (SUBLANES, width), 0)

    def body(g, h):
        r0 = pl.multiple_of(g * SUBLANES, SUBLANES)
        a = a_s[pl.ds(r0, SUBLANES), :]
        u = u_s[pl.ds(r0, SUBLANES), :]
        for s in (1, 2, 4):
            a_sh = pltpu.roll(a, s, 0)
            u_sh = pltpu.roll(u, s, 0)
            keep = row >= s
            u = jnp.where(keep, a * u_sh + u, u)
            a = jnp.where(keep, a * a_sh, a)
        hh = a * h + u
        u_s[pl.ds(r0, SUBLANES), :] = hh
        return hh[SUBLANES - 1:SUBLANES, :]

    h_last = lax.fori_loop(0, tt // SUBLANES, body, hc[0:1, :])
    hc[0:1, :] = h_last
    o_ref[...] = (u_s[...] * gate_ref[...]).astype(o_ref.dtype)


def _rglru(rest, conv_w, conv_b, wa_bf, ba, wx_bf, bx, lam):
    t = rest.shape[0]
    width = conv_w.shape[1]
    bw = width // RNN_BLOCKS
    tt = 512
    vec = lambda: pl.BlockSpec((1, width), lambda i: (0, 0))
    wblk = lambda: pl.BlockSpec((RNN_BLOCKS, bw, bw), lambda i: (0, 0, 0))
    return pl.pallas_call(
        functools.partial(_rglru_kernel, tt=tt),
        grid=(t // tt,),
        in_specs=[pl.BlockSpec((tt, width), lambda i: (i, 0)),
                  pl.BlockSpec((tt, width), lambda i: (i, 1)),
                  pl.BlockSpec((CONV_W, width), lambda i: (0, 0)), vec(),
                  wblk(), vec(), wblk(), vec(), vec()],
        out_specs=pl.BlockSpec((tt, width), lambda i: (i, 0)),
        out_shape=jax.ShapeDtypeStruct((t, width), BF16),
        scratch_shapes=[pltpu.VMEM((tt + SUBLANES, width), F32),
                        pltpu.VMEM((tt, width), F32),
                        pltpu.VMEM((tt, width), F32),
                        pltpu.VMEM((SUBLANES, width), F32)],
        compiler_params=_cparams("arbitrary"),
        name="rglru",
    )(rest, rest, conv_w, conv_b, wa_bf, ba, wx_bf, bx, lam)


def _mixer_out_kernel(ya_ref, yb_ref, sga_ref, sgb_ref, x_ref, wa_ref, wr_ref, wo_ref,
                      g1_ref, n2_ref, sc2_ref, sh2_ref, wrt_ref, brt_ref,
                      h1_ref, hnp_ref, lg_ref):
    up_a = jnp.dot(ya_ref[...], wa_ref[...], preferred_element_type=F32)
    up_b = jnp.dot(yb_ref[...], wr_ref[...], preferred_element_type=F32)
    merged = sga_ref[...] * up_a + sgb_ref[...] * up_b
    proj = jnp.dot(merged.astype(BF16), wo_ref[...], preferred_element_type=F32)
    h1 = x_ref[...] + g1_ref[...] * proj
    h1_ref[...] = h1
    ms = jnp.mean(h1 * h1, axis=-1, keepdims=True)
    hn = h1 * lax.rsqrt(ms + EPS) * n2_ref[...]
    hn = hn * (1.0 + sc2_ref[...]) + sh2_ref[...]
    lg_ref[...] = jnp.dot(hn, wrt_ref[...], preferred_element_type=F32,
                          precision=lax.Precision.HIGHEST) + brt_ref[...]
    half = hn.shape[1] // 2
    hb = hn.astype(BF16).astype(F32)
    lo = pltpu.bitcast(hb[:, :half], U32) >> 16
    hi = pltpu.bitcast(hb[:, half:], U32) & jnp.uint32(0xFFFF0000)
    hnp_ref[...] = hi | lo


def _mixer_out(ya, yb, rest, x2, wa_bf, wr_bf, wo_bf, g1, n2g, sc2, sh2, w_router, b_router):
    t, d = x2.shape
    aw = ya.shape[1]
    rw = yb.shape[1]
    ne = w_router.shape[1]
    tm = 256
    ga_blk = (2 * rw) // d
    const = lambda shape: pl.BlockSpec(shape, lambda i: (0,) * len(shape), pipeline_mode=pl.Buffered(1))
    return pl.pallas_call(
        _mixer_out_kernel,
        grid=(t // tm,),
        in_specs=[pl.BlockSpec((tm, aw), lambda i: (i, 0)),
                  pl.BlockSpec((tm, rw), lambda i: (i, 0)),
                  pl.BlockSpec((tm, d), lambda i: (i, ga_blk)),
                  pl.BlockSpec((tm, d), lambda i: (i, ga_blk + 1)),
                  pl.BlockSpec((tm, d), lambda i: (i, 0)),
                  const((aw, d)), const((rw, d)), const((d, d)),
                  const((1, d)), const((1, d)), const((1, d)), const((1, d)),
                  const((d, ne)), const((1, ne))],
        out_specs=[pl.BlockSpec((tm, d), lambda i: (i, 0)),
                   pl.BlockSpec((tm, d // 2), lambda i: (i, 0)),
                   pl.BlockSpec((tm, ne), lambda i: (i, 0))],
        out_shape=[jax.ShapeDtypeStruct((t, d), F32),
                   jax.ShapeDtypeStruct((t, d // 2), U32),
                   jax.ShapeDtypeStruct((t, ne), F32)],
        compiler_params=_cparams("arbitrary"),
        name="mixer_out",
    )(ya, yb, rest, rest, x2, wa_bf, wr_bf, wo_bf, g1, n2g, sc2, sh2, w_router, b_router)


def _route_kernel(lg_ref, idx_ref, w_ref, pos_ref, cnt_ref, carry, *, rt):
    i = pl.program_id(0)
    ne = lg_ref.shape[1]

    @pl.when(i == 0)
    def _():
        carry[...] = jnp.zeros(carry.shape, F32)

    l = lg_ref[...]
    lane = lax.broadcasted_iota(I32, (rt, ne), 1).astype(F32)
    vals, idxs, hots = [], [], []
    for _ in range(TOP_K):
        m = jnp.max(l, axis=-1, keepdims=True)
        ik = jnp.min(jnp.where(l == m, lane, float(ne)), axis=-1, keepdims=True)
        hot = lane == ik
        vals.append(m)
        idxs.append(ik)
        hots.append(hot)
        l = jnp.where(hot, -jnp.inf, l)
    es = [jnp.exp(v - vals[0]) for v in vals]
    tot = es[0] + es[1] + es[2] + es[3]
    sel = jnp.zeros((rt, ne), F32)
    for hot in hots:
        sel = sel + hot.astype(F32)
    rr = lax.broadcasted_iota(I32, (rt, rt), 0)
    cc = lax.broadcasted_iota(I32, (rt, rt), 1)
    tri = (cc < rr).astype(BF16)
    before = jnp.dot(tri, sel.astype(BF16), preferred_element_type=F32) + carry[0:1, :]
    out_lane = lax.broadcasted_iota(I32, (rt, LANES), 1)
    idx_o = jnp.zeros((rt, LANES), F32)
    w_o = jnp.zeros((rt, LANES), F32)
    pos_o = jnp.zeros((rt, LANES), F32)
    for k in range(TOP_K):
        pk = jnp.sum(jnp.where(hots[k], before, 0.0), axis=-1, keepdims=True)
        idx_o = jnp.where(out_lane == k, idxs[k], idx_o)
        w_o = jnp.where(out_lane == k, es[k] / tot, w_o)
        pos_o = jnp.where(out_lane == k, pk, pos_o)
    idx_ref[...] = idx_o.astype(I32)
    w_ref[...] = w_o
    pos_ref[...] = pos_o.astype(I32)
    carry[0:1, :] = carry[0:1, :] + jnp.sum(sel, axis=0, keepdims=True)
    cnt_ref[...] = carry[...]


def _route(logits):
    t, ne = logits.shape
    rt = 512
    wide = lambda dt: jax.ShapeDtypeStruct((t, LANES), dt)
    return pl.pallas_call(
        functools.partial(_route_kernel, rt=rt),
        grid=(t // rt,),
        in_specs=[pl.BlockSpec((rt, ne), lambda i: (i, 0))],
        out_specs=[pl.BlockSpec((rt, LANES), lambda i: (i, 0)),
                   pl.BlockSpec((rt, LANES), lambda i: (i, 0)),
                   pl.BlockSpec((rt, LANES), lambda i: (i, 0)),
                   pl.BlockSpec((SUBLANES, ne), lambda i: (0, 0))],
        out_shape=[wide(I32), wide(F32), wide(I32), jax.ShapeDtypeStruct((SUBLANES, ne), F32)],
        scratch_shapes=[pltpu.VMEM((SUBLANES, ne), F32)],
        compiler_params=_cparams("arbitrary"),
        name="route",
    )(logits)


def _dispatch_kernel(zblk_ref, dest_ref, hn_ref, xb_ref, zbuf, sem, zsem, *, dt):
    i = pl.program_id(0)

    @pl.when(i == 0)
    def _():
        zbuf[...] = jnp.zeros(zbuf.shape, zbuf.dtype)

        def zero_copy(j):
            row0 = pl.multiple_of(zblk_ref[j] * ROW_BLOCK, ROW_BLOCK)
            return pltpu.make_async_copy(zbuf, xb_ref.at[pl.ds(row0, ROW_BLOCK)], zsem)

        def start(j, carry):
            @pl.when(zblk_ref[j] >= 0)
            def _():
                zero_copy(j).start()
            return carry

        def wait(j, carry):
            @pl.when(zblk_ref[j] >= 0)
            def _():
                zero_copy(j).wait()
            return carry

        lax.fori_loop(0, 2 * N_EXPERTS, start, 0)
        lax.fori_loop(0, 2 * N_EXPERTS, wait, 0)

    def issue(r, carry):
        for k in range(TOP_K):
            d = dest_ref[0, 0, r * TOP_K + k]
            pltpu.make_async_copy(hn_ref.at[pl.ds(r, 1)], xb_ref.at[pl.ds(d, 1)], sem).start()
        return carry

    lax.fori_loop(0, dt, issue, 0)
    for k in range(TOP_K):
        pltpu.make_async_copy(hn_ref, xb_ref.at[pl.ds(0, dt)], sem).wait()


def _dispatch(hn_packed, dest, zblk, n_rows):
    t, hw = hn_packed.shape
    dt = 128
    dest3 = dest.reshape(t // dt, 1, dt * TOP_K)
    grid_spec = pltpu.PrefetchScalarGridSpec(
        num_scalar_prefetch=1,
        grid=(t // dt,),
        in_specs=[pl.BlockSpec((1, 1, dt * TOP_K), lambda i, zb: (i, 0, 0), memory_space=pltpu.SMEM),
                  pl.BlockSpec((dt, hw), lambda i, zb: (i, 0))],
        out_specs=pl.BlockSpec(memory_space=pl.ANY),
        scratch_shapes=[pltpu.VMEM((ROW_BLOCK, hw), U32),
                        pltpu.SemaphoreType.DMA(()),
                        pltpu.SemaphoreType.DMA(())],
    )
    return pl.pallas_call(
        functools.partial(_dispatch_kernel, dt=dt),
        grid_spec=grid_spec,
        out_shape=jax.ShapeDtypeStruct((n_rows, hw), U32),
        compiler_params=_cparams("arbitrary"),
        name="dispatch",
    )(zblk, dest3, hn_packed)


def _unpack_rows(p):
    lo = pltpu.bitcast(p << 16, F32).astype(BF16)
    hi = pltpu.bitcast(p & jnp.uint32(0xFFFF0000), F32).astype(BF16)
    return jnp.concatenate([lo, hi], axis=1)


def _first_of_expert(be_ref, m):
    prev = be_ref[jnp.maximum(m - 1, 0)]
    return (m == 0) | (be_ref[m] != prev)


def _weight_stream(be_ref, na_ref, end_ref, n_tiles, copies, convert):
    n = pl.program_id(0)
    m = pl.program_id(1)
    na = na_ref[0]
    active = m < na

    @pl.when((n == 0) & (m == 0))
    def _():
        for cp in copies(be_ref[0], 0):
            cp.start()

    @pl.when(active & _first_of_expert(be_ref, m))
    def _():
        e = be_ref[m]
        for cp in copies(e, n):
            cp.wait()
        convert()
        run_end = end_ref[e]
        wrap = run_end >= na
        n2 = jnp.where(wrap, n + 1, n)
        m2 = jnp.where(wrap, 0, run_end)

        @pl.when(n2 < n_tiles)
        def _():
            for cp in copies(be_ref[m2], n2):
                cp.start()

    return active


def _moe_gu_kernel(be_ref, na_ref, end_ref, x_ref, w_hbm, bg_ref, bu_ref, o_ref,
                   stg_g, stg_u, wgb, wub, sem, *, tn, nf):
    def copies(e, n):
        c0 = pl.multiple_of(n * tn, tn)
        c1 = pl.multiple_of((n + nf) * tn, tn)
        return [pltpu.make_async_copy(w_hbm.at[e, :, pl.ds(c0, tn)], stg_g, sem.at[0]),
                pltpu.make_async_copy(w_hbm.at[e, :, pl.ds(c1, tn)], stg_u, sem.at[1])]

    def convert():
        wgb[...] = stg_g[...].astype(BF16)
        wub[...] = stg_u[...].astype(BF16)

    active = _weight_stream(be_ref, na_ref, end_ref, nf, copies, convert)

    @pl.when(active)
    def _():
        x = _unpack_rows(x_ref[...])
        g = jnp.dot(x, wgb[...], preferred_element_type=F32) + bg_ref[...]
        u = jnp.dot(x, wub[...], preferred_element_type=F32) + bu_ref[...]
        g = jnp.minimum(g, SWIGLU_LIMIT)
        u = jnp.clip(u, -SWIGLU_LIMIT, SWIGLU_LIMIT)
        glu = g * jax.nn.sigmoid(SWIGLU_ALPHA * g)
        o_ref[...] = ((u + 1.0) * glu).astype(o_ref.dtype)

    @pl.when(jnp.logical_not(active))
    def _():
        o_ref[...] = jnp.zeros(o_ref.shape, o_ref.dtype)


def _moe_gu(xb, w_gu, b_gu, blk_e, n_active, run_end):
    n_rows, hw = xb.shape
    ne, d, f2 = w_gu.shape
    f = f2 // 2
    tn = 1024
    nf = f // tn
    nb = n_rows // ROW_BLOCK
    mc = lambda m, na: jnp.minimum(m, na[0] - 1)
    grid_spec = pltpu.PrefetchScalarGridSpec(
        num_scalar_prefetch=3,
        grid=(nf, nb),
        in_specs=[pl.BlockSpec((ROW_BLOCK, hw), lambda n, m, be, na, en: (mc(m, na), 0)),
                  pl.BlockSpec(memory_space=pl.ANY),
                  pl.BlockSpec((None, 1, tn), lambda n, m, be, na, en: (be[mc(m, na)], 0, n)),
                  pl.BlockSpec((None, 1, tn), lambda n, m, be, na, en: (be[mc(m, na)], 0, n + nf))],
        out_specs=pl.BlockSpec((ROW_BLOCK, tn), lambda n, m, be, na, en: (m, n)),
        scratch_shapes=[pltpu.VMEM((d, tn), F32), pltpu.VMEM((d, tn), F32),
                        pltpu.VMEM((d, tn), BF16), pltpu.VMEM((d, tn), BF16),
                        pltpu.SemaphoreType.DMA((2,))],
    )
    b3 = b_gu.reshape(ne, 1, f2)
    return pl.pallas_call(
        functools.partial(_moe_gu_kernel, tn=tn, nf=nf),
        grid_spec=grid_spec,
        out_shape=jax.ShapeDtypeStruct((n_rows, f), BF16),
        compiler_params=_cparams("arbitrary", "arbitrary"),
        name="moe_gu",
    )(blk_e, n_active, run_end, xb, w_gu, b3, b3)


def _moe_down_kernel(be_ref, na_ref, end_ref, a_ref, w_hbm, b_ref, o_ref, stg, wb, sem, *, tn, nd):
    def copies(e, n):
        c0 = pl.multiple_of(n * tn, tn)
        return [pltpu.make_async_copy(w_hbm.at[e, :, pl.ds(c0, tn)], stg, sem.at[0])]

    def convert():
        wb[...] = stg[...].astype(BF16)

    active = _weight_stream(be_ref, na_ref, end_ref, nd, copies, convert)

    @pl.when(active)
    def _():
        o_ref[...] = jnp.dot(a_ref[...], wb[...], preferred_element_type=F32) + b_ref[...]

    @pl.when(jnp.logical_not(active))
    def _():
        o_ref[...] = jnp.zeros(o_ref.shape, o_ref.dtype)


def _moe_down(act, w_down, b_down, blk_e, n_active, run_end):
    n_rows, f = act.shape
    ne, _, d = w_down.shape
    tn = d
    nd = d // tn
    nb = n_rows // ROW_BLOCK
    mc = lambda m, na: jnp.minimum(m, na[0] - 1)
    grid_spec = pltpu.PrefetchScalarGridSpec(
        num_scalar_prefetch=3,
        grid=(nd, nb),
        in_specs=[pl.BlockSpec((ROW_BLOCK, f), lambda n, m, be, na, en: (mc(m, na), 0)),
                  pl.BlockSpec(memory_space=pl.ANY),
                  pl.BlockSpec((None, 1, tn), lambda n, m, be, na, en: (be[mc(m, na)], 0, n))],
        out_specs=pl.BlockSpec((ROW_BLOCK, tn), lambda n, m, be, na, en: (m, n)),
        scratch_shapes=[pltpu.VMEM((f, tn), F32), pltpu.VMEM((f, tn), BF16),
                        pltpu.SemaphoreType.DMA((1,))],
    )
    return pl.pallas_call(
        functools.partial(_moe_down_kernel, tn=tn, nd=nd),
        grid_spec=grid_spec,
        out_shape=jax.ShapeDtypeStruct((n_rows, d), F32),
        compiler_params=_cparams("arbitrary", "arbitrary"),
        name="moe_down",
    )(blk_e, n_active, run_end, act, w_down, b_down.reshape(ne, 1, d))


def _combine_kernel(dest_ref, w_ref, h1_ref, g2_ref, yb_ref, o_ref, buf, sem, *, ct):
    def issue(r, carry):
        for k in range(TOP_K):
            d = dest_ref[0, 0, r * TOP_K + k]
            pltpu.make_async_copy(yb_ref.at[pl.ds(d, 1)], buf.at[k, pl.ds(r, 1)], sem).start()
        return carry

    lax.fori_loop(0, ct, issue, 0)
    for k in range(TOP_K):
        pltpu.make_async_copy(yb_ref.at[pl.ds(0, ct)], buf.at[k], sem).wait()
    w = w_ref[...]
    acc = buf[0] * w[:, 0:1]
    for k in range(1, TOP_K):
        acc = acc + buf[k] * w[:, k:k + 1]
    o_ref[...] = h1_ref[...] + g2_ref[...] * acc


def _combine(dest, top_w, h1, g2, yb):
    t, d = h1.shape
    ct = 128
    dest3 = dest.reshape(t // ct, 1, ct * TOP_K)
    return pl.pallas_call(
        functools.partial(_combine_kernel, ct=ct),
        grid=(t // ct,),
        in_specs=[pl.BlockSpec((1, 1, ct * TOP_K), lambda i: (i, 0, 0), memory_space=pltpu.SMEM),
                  pl.BlockSpec((ct, LANES), lambda i: (i, 0)),
                  pl.BlockSpec((ct, d), lambda i: (i, 0)),
                  pl.BlockSpec((1, d), lambda i: (0, 0)),
                  pl.BlockSpec(memory_space=pl.ANY)],
        out_specs=pl.BlockSpec((ct, d), lambda i: (i, 0)),
        out_shape=jax.ShapeDtypeStruct((t, d), F32),
        scratch_shapes=[pltpu.VMEM((TOP_K, ct, d), F32), pltpu.SemaphoreType.DMA(())],
        compiler_params=_cparams("arbitrary"),
        name="combine",
    )(dest3, top_w, h1, g2, yb)


def kernel(x, c, ada_w, ada_b, norm1_g, w_in, q_norm_g, k_norm_g, rel_bias, conv_w, conv_b,
           rg_a_w, rg_a_b, rg_x_w, rg_x_b, rg_lambda, w_attn_up, w_rnn_up, w_out, norm2_g,
           w_router, b_router, w_gu, b_gu, w_down, b_down):
    b, s, d = x.shape
    assert b == 1 and ada_w.shape[0] == 1, "single batch row, single layer"
    t = s
    x2 = x.reshape(t, d)
    aw = ATTN_HEADS * HEAD_DIM
    rw = conv_w.shape[2]

    mod = _ada(c, ada_w[0], ada_b[0])
    sh1, sc1, g1, sh2, sc2, g2 = [mod[:, k * d:(k + 1) * d] for k in range(6)]

    w_in_bf = w_in[0].astype(BF16)
    qg = q_norm_g[0].reshape(1, HEAD_DIM)
    kg = k_norm_g[0].reshape(1, HEAD_DIM)
    n1g = norm1_g[0].reshape(1, d)
    qkv = _inproj(x2, n1g, sc1, sh1, w_in_bf, qg, kg, col0=0,
                  groups=[(aw, "q"), (aw, "k"), (aw, "v")], out_dtype=BF16, name="inproj_qkv")
    rest = _inproj(x2, n1g, sc1, sh1, w_in_bf, qg, kg, col0=3 * aw,
                   groups=[(rw, "id"), (rw, "gelu"), (2 * d, "sigmoid")], out_dtype=F32,
                   name="inproj_rest")

    ya = _attn(qkv, _attn_distance_table(rel_bias[0]))
    yb = _rglru(rest, conv_w[0], conv_b[0].reshape(1, rw),
                rg_a_w[0].astype(BF16), rg_a_b[0].reshape(1, rw),
                rg_x_w[0].astype(BF16), rg_x_b[0].reshape(1, rw), rg_lambda[0].reshape(1, rw))

    h1, hn_packed, logits = _mixer_out(
        ya, yb, rest, x2, w_attn_up[0].astype(BF16), w_rnn_up[0].astype(BF16), w_out[0].astype(BF16),
        g1, norm2_g[0].reshape(1, d), sc2, sh2, w_router[0], b_router[0].reshape(1, N_EXPERTS))

    idx_w, topw_w, pos_w, cnt = _route(logits)
    idx = idx_w[:, :TOP_K]
    pos = pos_w[:, :TOP_K]
    counts = cnt[0].astype(I32)
    padded = (counts + ROW_BLOCK - 1) // ROW_BLOCK * ROW_BLOCK
    pend = jnp.cumsum(padded)
    pstart = pend - padded
    experts = jnp.arange(N_EXPERTS, dtype=I32)
    dest = (pos + jnp.sum(jnp.where(idx[:, :, None] == experts, pstart, 0), axis=-1)).reshape(-1)
    n_blocks = (t * TOP_K) // ROW_BLOCK + N_EXPERTS
    n_rows = n_blocks * ROW_BLOCK
    blk_row0 = jnp.arange(n_blocks, dtype=I32) * ROW_BLOCK
    blk_e = jnp.minimum(jnp.sum((pend[None, :] <= blk_row0[:, None]).astype(I32), axis=1), N_EXPERTS - 1)
    n_active = (pend[-1:] // ROW_BLOCK).astype(I32)
    run_end = (pend // ROW_BLOCK).astype(I32)
    last_blk = jnp.where(padded > 0, pend // ROW_BLOCK - 1, -1)
    spare = n_active[0] + jnp.arange(N_EXPERTS, dtype=I32)
    zblk = jnp.concatenate([last_blk, jnp.where(spare < n_blocks, spare, -1)]).astype(I32)

    xb = _dispatch(hn_packed, dest, zblk, n_rows)
    act = _moe_gu(xb, w_gu[0], b_gu[0], blk_e, n_active, run_end)
    y_rows = _moe_down(act, w_down[0], b_down[0], blk_e, n_active, run_end)
    out = _combine(dest, topw_w, h1, g2, y_rows)
    return out.reshape(b, s, d)
```

```python
import functools
import math

import jax
import jax.numpy as jnp
from jax import lax
from jax.experimental import pallas as pl
from jax.experimental.pallas import tpu as pltpu

F32 = jnp.float32
BF16 = jnp.bfloat16
U32 = jnp.uint32
I32 = jnp.int32

CHUNK = 64
LEFT_CHUNKS = 8
ATTN_HEADS = 8
HEAD_DIM = 128
REL_MAX = 256
REL_MIN = -(CHUNK - 1)
RNN_BLOCKS = 8
CONV_W = 4
RG_C = 8.0
N_EXPERTS = 32
TOP_K = 4
SWIGLU_LIMIT = 7.0
SWIGLU_ALPHA = 1.702
EPS = 1e-6
NEG_BIG = -1e30

LANES = 128
SUBLANES = 8
VMEM_LIMIT = 56 * 1024 * 1024

ROW_BLOCK = 512
ATTN_QB = 4 * CHUNK
ATTN_KB = ATTN_QB + LEFT_CHUNKS * CHUNK
ATTN_TAB = ATTN_KB + ATTN_QB
CHUNK_SHIFT = CHUNK.bit_length() - 1
assert 1 << CHUNK_SHIFT == CHUNK


def _cparams(*sem):
    return pltpu.CompilerParams(dimension_semantics=sem, vmem_limit_bytes=VMEM_LIMIT)


def _ada_kernel(c_ref, w_ref, b_ref, o_ref):
    c = c_ref[...]
    ca = c * jax.nn.sigmoid(c)
    o_ref[...] = jnp.dot(ca, w_ref[...], preferred_element_type=F32,
                         precision=lax.Precision.HIGHEST) + b_ref[...]


def _ada(c, ada_w, ada_b):
    d, n = ada_w.shape
    tn = 1024
    c8 = jnp.broadcast_to(c, (SUBLANES, d))
    out = pl.pallas_call(
        _ada_kernel,
        grid=(n // tn,),
        in_specs=[pl.BlockSpec((SUBLANES, d), lambda j: (0, 0)),
                  pl.BlockSpec((d, tn), lambda j: (0, j)),
                  pl.BlockSpec((1, tn), lambda j: (0, j))],
        out_specs=pl.BlockSpec((SUBLANES, tn), lambda j: (0, j)),
        out_shape=jax.ShapeDtypeStruct((SUBLANES, n), F32),
        compiler_params=_cparams("arbitrary"),
        name="ada",
    )(c8, ada_w, ada_b.reshape(1, n))
    return out[0:1]


def _gelu_tanh(x):
    return 0.5 * x * (1.0 + jnp.tanh(math.sqrt(2.0 / math.pi) * (x + 0.044715 * (x * x * x))))


def _head_rms(a, g):
    outs = []
    for hh in range(a.shape[1] // HEAD_DIM):
        s = a[:, hh * HEAD_DIM:(hh + 1) * HEAD_DIM]
        ms = jnp.mean(s * s, axis=-1, keepdims=True)
        outs.append(s * lax.rsqrt(ms + EPS) * g)
    return jnp.concatenate(outs, axis=1)


def _inproj_kernel(x_ref, g_ref, sc_ref, sh_ref, w_ref, qg_ref, kg_ref, o_ref, xn_ref, *, bounds, kinds):
    j = pl.program_id(1)

    @pl.when(j == 0)
    def _():
        x = x_ref[...]
        ms = jnp.mean(x * x, axis=-1, keepdims=True)
        xn = x * lax.rsqrt(ms + EPS) * g_ref[...]
        xn = xn * (1.0 + sc_ref[...]) + sh_ref[...]
        xn_ref[...] = xn.astype(BF16)

    acc = jnp.dot(xn_ref[...], w_ref[...], preferred_element_type=F32)
    lo = 0
    for hi, kind in zip(bounds, kinds):
        @pl.when((j >= lo) & (j < hi))
        def _(kind=kind):
            if kind == "q":
                r = _head_rms(acc, qg_ref[...]) * (1.0 / math.sqrt(HEAD_DIM))
            elif kind == "k":
                r = _head_rms(acc, kg_ref[...])
            elif kind == "gelu":
                r = _gelu_tanh(acc)
            elif kind == "sigmoid":
                r = jax.nn.sigmoid(acc)
            else:
                r = acc
            o_ref[...] = r.astype(o_ref.dtype)
        lo = hi


def _inproj(x2, g, sc, sh, w_bf, qg, kg, *, col0, groups, out_dtype, name):
    t, d = x2.shape
    tm, tn = 1024, 512
    n = sum(wd for wd, _ in groups)
    bounds, acc = [], 0
    for wd, _ in groups:
        acc += wd // tn
        bounds.append(acc)
    kinds = [k for _, k in groups]
    j0 = col0 // tn
    vec = lambda: pl.BlockSpec((1, d), lambda i, j: (0, 0))
    hvec = lambda: pl.BlockSpec((1, HEAD_DIM), lambda i, j: (0, 0))
    return pl.pallas_call(
        functools.partial(_inproj_kernel, bounds=tuple(bounds), kinds=tuple(kinds)),
        grid=(t // tm, n // tn),
        in_specs=[pl.BlockSpec((tm, d), lambda i, j: (i, 0)), vec(), vec(), vec(),
                  pl.BlockSpec((d, tn), lambda i, j: (0, j + j0)), hvec(), hvec()],
        out_specs=pl.BlockSpec((tm, tn), lambda i, j: (i, j)),
        out_shape=jax.ShapeDtypeStruct((t, n), out_dtype),
        scratch_shapes=[pltpu.VMEM((tm, d), BF16)],
        compiler_params=_cparams("arbitrary", "arbitrary"),
        name=name,
    )(x2, g, sc, sh, w_bf, qg, kg)


def _attn_kernel(q_ref, k0_ref, k1_ref, k2_ref, v0_ref, v1_ref, v2_ref, ftab_ref, o_ref, bias_ref):
    i = pl.program_id(0)
    col = lax.broadcasted_iota(I32, (ATTN_QB, ATTN_KB), 1)

    @pl.when(i == 0)
    def _():
        row = lax.broadcasted_iota(I32, (ATTN_QB, ATTN_KB), 0)
        cq = row >> CHUNK_SHIFT
        ck = col >> CHUNK_SHIFT
        visible = (ck >= cq) & (ck <= cq + LEFT_CHUNKS)
        for h in range(ATTN_HEADS):
            g = jnp.broadcast_to(ftab_ref[h:h + 1, :], (ATTN_QB, ATTN_TAB))
            y = pltpu.roll(g, 0, 1, stride=1, stride_axis=0)
            bias_ref[h] = jnp.where(visible, y[:, :ATTN_KB], NEG_BIG)

    in_seq = col >= (LEFT_CHUNKS * CHUNK - i * ATTN_QB)
    for h in range(ATTN_HEADS):
        sl = slice(h * HEAD_DIM, (h + 1) * HEAD_DIM)
        q = q_ref[:, sl]
        k = jnp.concatenate([k0_ref[:, sl], k1_ref[:, sl], k2_ref[:, sl]], axis=0)
        v = jnp.concatenate([v0_ref[:, sl], v1_ref[:, sl], v2_ref[:, sl]], axis=0)
        s = lax.dot_general(q, k, (((1,), (1,)), ((), ())), preferred_element_type=F32)
        s = jnp.where(in_seq, s + bias_ref[h], NEG_BIG)
        m = jnp.max(s, axis=-1, keepdims=True)
        p = jnp.exp(s - m)
        l = jnp.sum(p, axis=-1, keepdims=True)
        o = jnp.dot(p.astype(BF16), v, preferred_element_type=F32) / l
        o_ref[:, sl] = o.astype(o_ref.dtype)


def _attn(qkv, ftab):
    t = qkv.shape[0]
    w = ATTN_HEADS * HEAD_DIM
    qb = ATTN_QB
    blk = lambda f: pl.BlockSpec((qb, w), f)
    return pl.pallas_call(
        _attn_kernel,
        grid=(t // qb,),
        in_specs=[blk(lambda i: (i, 0)),
                  blk(lambda i: (jnp.maximum(i - 2, 0), 1)), blk(lambda i: (jnp.maximum(i - 1, 0), 1)),
                  blk(lambda i: (i, 1)),
                  blk(lambda i: (jnp.maximum(i - 2, 0), 2)), blk(lambda i: (jnp.maximum(i - 1, 0), 2)),
                  blk(lambda i: (i, 2)),
                  pl.BlockSpec((ATTN_HEADS, ATTN_TAB), lambda i: (0, 0))],
        out_specs=blk(lambda i: (i, 0)),
        out_shape=jax.ShapeDtypeStruct((t, w), BF16),
        scratch_shapes=[pltpu.VMEM((ATTN_HEADS, qb, ATTN_KB), F32)],
        compiler_params=_cparams("arbitrary"),
        name="attn",
    )(qkv, qkv, qkv, qkv, qkv, qkv, qkv, ftab)


def _attn_distance_table(rel_bias):
    pad = LEFT_CHUNKS * CHUNK
    n_far = pad - REL_MAX + 1
    n_mid = REL_MAX - REL_MIN
    n_ahead = ATTN_KB - n_far - n_mid
    far = rel_bias[:, -1:]
    return jnp.concatenate([jnp.broadcast_to(far, (ATTN_HEADS, n_far)),
                            rel_bias[:, :n_mid][:, ::-1],
                            jnp.broadcast_to(rel_bias[:, :1], (ATTN_HEADS, n_ahead)),
                            jnp.broadcast_to(far, (ATTN_HEADS, ATTN_TAB - ATTN_KB))], axis=1)


def _rglru_kernel(rx_ref, gate_ref, cw_ref, cb_ref, wa_ref, ba_ref, wx_ref, bx_ref, lam_ref,
                  o_ref, xbuf, a_s, u_s, hc, *, tt):
    i = pl.program_id(0)
    width = rx_ref.shape[1]
    bw = width // RNN_BLOCKS

    @pl.when(i == 0)
    def _():
        xbuf[0:SUBLANES, :] = jnp.zeros((SUBLANES, width), F32)
        hc[...] = jnp.zeros(hc.shape, F32)

    xbuf[SUBLANES:SUBLANES + tt, :] = rx_ref[...]
    xc = cb_ref[...] + cw_ref[CONV_W - 1:CONV_W, :] * xbuf[SUBLANES:SUBLANES + tt, :]
    for d in range(1, CONV_W):
        xc = xc + cw_ref[CONV_W - 1 - d:CONV_W - d, :] * xbuf[SUBLANES - d:SUBLANES - d + tt, :]
    tail = xbuf[tt:tt + SUBLANES, :]
    xbuf[0:SUBLANES, :] = tail

    z = -lam_ref[...]
    softplus = jnp.maximum(z, 0.0) + jnp.log1p(jnp.exp(-jnp.abs(z)))
    xcb = xc.astype(BF16)
    for b in range(RNN_BLOCKS):
        sl = slice(b * bw, (b + 1) * bw)
        xb_ = xcb[:, sl]
        r = jax.nn.sigmoid(jnp.dot(xb_, wa_ref[b], preferred_element_type=F32) + ba_ref[:, sl])
        ig = jax.nn.sigmoid(jnp.dot(xb_, wx_ref[b], preferred_element_type=F32) + bx_ref[:, sl])
        log_a = (-RG_C) * r * softplus[:, sl]
        a = jnp.exp(log_a)
        th = jnp.tanh(log_a)
        one_m_a2 = (-2.0) * th / (1.0 - th)
        a_s[:, sl] = a
        u_s[:, sl] = jnp.sqrt(one_m_a2) * (ig * xc[:, sl])

    row = lax.broadcasted_iota(I32, (SUBLANES, width), 0)

    def body(g, h):
        r0 = pl.multiple_of(g * SUBLANES, SUBLANES)
        a = a_s[pl.ds(r0, SUBLANES), :]
        u = u_s[pl.ds(r0, SUBLANES), :]
        for s in (1, 2, 4):
            a_sh = pltpu.roll(a, s, 0)
            u_sh = pltpu.roll(u, s, 0)
            keep = row >= s
            u = jnp.where(keep, a * u_sh + u, u)
            a = jnp.where(keep, a * a_sh, a)
        hh = a * h + u
        u_s[pl.ds(r0, SUBLANES), :] = hh
        return hh[SUBLANES - 1:SUBLANES, :]

    h_last = lax.fori_loop(0, tt // SUBLANES, body, hc[0:1, :])
    hc[0:1, :] = h_last
    o_ref[...] = (u_s[...] * gate_ref[...]).astype(o_ref.dtype)


def _rglru(rest, conv_w, conv_b, wa_bf, ba, wx_bf, bx, lam):
    t = rest.shape[0]
    width = conv_w.shape[1]
    bw = width // RNN_BLOCKS
    tt = 512
    vec = lambda: pl.BlockSpec((1, width), lambda i: (0, 0))
    wblk = lambda: pl.BlockSpec((RNN_BLOCKS, bw, bw), lambda i: (0, 0, 0))
    return pl.pallas_call(
        functools.partial(_rglru_kernel, tt=tt),
        grid=(t // tt,),
        in_specs=[pl.BlockSpec((tt, width), lambda i: (i, 0)),
                  pl.BlockSpec((tt, width), lambda i: (i, 1)),
                  pl.BlockSpec((CONV_W, width), lambda i: (0, 0)), vec(),
                  wblk(), vec(), wblk(), vec(), vec()],
        out_specs=pl.BlockSpec((tt, width), lambda i: (i, 0)),
        out_shape=jax.ShapeDtypeStruct((t, width), BF16),
        scratch_shapes=[pltpu.VMEM((tt + SUBLANES, width), F32),
                        pltpu.VMEM((tt, width), F32),
                        pltpu.VMEM((tt, width), F32),
                        pltpu.VMEM((SUBLANES, width), F32)],
        compiler_params=_cparams("arbitrary"),
        name="rglru",
    )(rest, rest, conv_w, conv_b, wa_bf, ba, wx_bf, bx, lam)


def _mixer_out_kernel(ya_ref, yb_ref, sga_ref, sgb_ref, x_ref, wa_ref, wr_ref, wo_ref,
                      g1_ref, n2_ref, sc2_ref, sh2_ref, wrt_ref, brt_ref,
                      h1_ref, hnp_ref, lg_ref):
    up_a = jnp.dot(ya_ref[...], wa_ref[...], preferred_element_type=F32)
    up_b = jnp.dot(yb_ref[...], wr_ref[...], preferred_element_type=F32)
    merged = sga_ref[...] * up_a + sgb_ref[...] * up_b
    proj = jnp.dot(merged.astype(BF16), wo_ref[...], preferred_element_type=F32)
    h1 = x_ref[...] + g1_ref[...] * proj
    h1_ref[...] = h1
    ms = jnp.mean(h1 * h1, axis=-1, keepdims=True)
    hn = h1 * lax.rsqrt(ms + EPS) * n2_ref[...]
    hn = hn * (1.0 + sc2_ref[...]) + sh2_ref[...]
    lg_ref[...] = jnp.dot(hn, wrt_ref[...], preferred_element_type=F32,
                          precision=lax.Precision.HIGHEST) + brt_ref[...]
    half = hn.shape[1] // 2
    hb = hn.astype(BF16).astype(F32)
    lo = pltpu.bitcast(hb[:, :half], U32) >> 16
    hi = pltpu.bitcast(hb[:, half:], U32) & jnp.uint32(0xFFFF0000)
    hnp_ref[...] = hi | lo


def _mixer_out(ya, yb, rest, x2, wa_bf, wr_bf, wo_bf, g1, n2g, sc2, sh2, w_router, b_router):
    t, d = x2.shape
    aw = ya.shape[1]
    rw = yb.shape[1]
    ne = w_router.shape[1]
    tm = 256
    ga_blk = (2 * rw) // d
    const = lambda shape: pl.BlockSpec(shape, lambda i: (0,) * len(shape), pipeline_mode=pl.Buffered(1))
    return pl.pallas_call(
        _mixer_out_kernel,
        grid=(t // tm,),
        in_specs=[pl.BlockSpec((tm, aw), lambda i: (i, 0)),
                  pl.BlockSpec((tm, rw), lambda i: (i, 0)),
                  pl.BlockSpec((tm, d), lambda i: (i, ga_blk)),
                  pl.BlockSpec((tm, d), lambda i: (i, ga_blk + 1)),
                  pl.BlockSpec((tm, d), lambda i: (i, 0)),
                  const((aw, d)), const((rw, d)), const((d, d)),
                  const((1, d)), const((1, d)), const((1, d)), const((1, d)),
                  const((d, ne)), const((1, ne))],
        out_specs=[pl.BlockSpec((tm, d), lambda i: (i, 0)),
                   pl.BlockSpec((tm, d // 2), lambda i: (i, 0)),
                   pl.BlockSpec((tm, ne), lambda i: (i, 0))],
        out_shape=[jax.ShapeDtypeStruct((t, d), F32),
                   jax.ShapeDtypeStruct((t, d // 2), U32),
                   jax.ShapeDtypeStruct((t, ne), F32)],
        compiler_params=_cparams("arbitrary"),
        name="mixer_out",
    )(ya, yb, rest, rest, x2, wa_bf, wr_bf, wo_bf, g1, n2g, sc2, sh2, w_router, b_router)


def _route_kernel(lg_ref, idx_ref, w_ref, pos_ref, cnt_ref, carry, *, rt):
    i = pl.program_id(0)
    ne = lg_ref.shape[1]

    @pl.when(i == 0)
    def _():
        carry[...] = jnp.zeros(carry.shape, F32)

    l = lg_ref[...]
    lane = lax.broadcasted_iota(I32, (rt, ne), 1).astype(F32)
    vals, idxs, hots = [], [], []
    for _ in range(TOP_K):
        m = jnp.max(l, axis=-1, keepdims=True)
        ik = jnp.min(jnp.where(l == m, lane, float(ne)), axis=-1, keepdims=True)
        hot = lane == ik
        vals.append(m)
        idxs.append(ik)
        hots.append(hot)
        l = jnp.where(hot, -jnp.inf, l)
    es = [jnp.exp(v - vals[0]) for v in vals]
    tot = es[0] + es[1] + es[2] + es[3]
    sel = jnp.zeros((rt, ne), F32)
    for hot in hots:
        sel = sel + hot.astype(F32)
    rr = lax.broadcasted_iota(I32, (rt, rt), 0)
    cc = lax.broadcasted_iota(I32, (rt, rt), 1)
    tri = (cc < rr).astype(BF16)
    before = jnp.dot(tri, sel.astype(BF16), preferred_element_type=F32) + carry[0:1, :]
    out_lane = lax.broadcasted_iota(I32, (rt, LANES), 1)
    idx_o = jnp.zeros((rt, LANES), F32)
    w_o = jnp.zeros((rt, LANES), F32)
    pos_o = jnp.zeros((rt, LANES), F32)
    for k in range(TOP_K):
        pk = jnp.sum(jnp.where(hots[k], before, 0.0), axis=-1, keepdims=True)
        idx_o = jnp.where(out_lane == k, idxs[k], idx_o)
        w_o = jnp.where(out_lane == k, es[k] / tot, w_o)
        pos_o = jnp.where(out_lane == k, pk, pos_o)
    idx_ref[...] = idx_o.astype(I32)
    w_ref[...] = w_o
    pos_ref[...] = pos_o.astype(I32)
    carry[0:1, :] = carry[0:1, :] + jnp.sum(sel, axis=0, keepdims=True)
    cnt_ref[...] = carry[...]


def _route(logits):
    t, ne = logits.shape
    rt = 512
    wide = lambda dt: jax.ShapeDtypeStruct((t, LANES), dt)
    return pl.pallas_call(
        functools.partial(_route_kernel, rt=rt),
        grid=(t // rt,),
        in_specs=[pl.BlockSpec((rt, ne), lambda i: (i, 0))],
        out_specs=[pl.BlockSpec((rt, LANES), lambda i: (i, 0)),
                   pl.BlockSpec((rt, LANES), lambda i: (i, 0)),
                   pl.BlockSpec((rt, LANES), lambda i: (i, 0)),
                   pl.BlockSpec((SUBLANES, ne), lambda i: (0, 0))],
        out_shape=[wide(I32), wide(F32), wide(I32), jax.ShapeDtypeStruct((SUBLANES, ne), F32)],
        scratch_shapes=[pltpu.VMEM((SUBLANES, ne), F32)],
        compiler_params=_cparams("arbitrary"),
        name="route",
    )(logits)


def _unpack_rows(p):
    lo = pltpu.bitcast(p << 16, F32).astype(BF16)
    hi = pltpu.bitcast(p & jnp.uint32(0xFFFF0000), F32).astype(BF16)
    return jnp.concatenate([lo, hi], axis=1)


def _first_of_expert(be_ref, m, na):
    mm = jnp.minimum(m, na - 1)
    prev = be_ref[jnp.maximum(mm - 1, 0)]
    return (mm == 0) | (be_ref[mm] != prev)


def _weight_stream(be_ref, na_ref, end_ref, n_tiles, copies, convert):
    n = pl.program_id(0)
    m = pl.program_id(1)
    na = na_ref[0]
    active = m < na

    @pl.when((n == 0) & (m == 0))
    def _():
        for cp in copies(be_ref[0], 0):
            cp.start()

    @pl.when(active & _first_of_expert(be_ref, m, na))
    def _():
        e = be_ref[m]
        for cp in copies(e, n):
            cp.wait()
        convert()
        run_end = end_ref[e]
        wrap = run_end >= na
        n2 = jnp.where(wrap, n + 1, n)
        m2 = jnp.where(wrap, 0, run_end)

        @pl.when(n2 < n_tiles)
        def _():
            for cp in copies(be_ref[m2], n2):
                cp.start()

    return active


def _moe_gu_kernel(be_ref, na_ref, end_ref, tok0_ref, tokn_ref, hn_hbm, w_hbm, bg_ref, bu_ref, o_ref,
                   stg_g, stg_u, wgb, wub, xbuf0, xbuf1, sem, gsem, *, tn, nf):
    n = pl.program_id(0)
    m = pl.program_id(1)
    na = na_ref[0]
    slot = (n * na + m) & 1
    xbufs = (xbuf0, xbuf1)

    def gather(tok_ref, s):
        for r in range(ROW_BLOCK):
            pltpu.make_async_copy(hn_hbm.at[pl.ds(tok_ref[0, 0, r], 1)], xbufs[s].at[pl.ds(r, 1)],
                                  gsem.at[s]).start()

    def gather_wait(s):
        pltpu.make_async_copy(hn_hbm.at[pl.ds(0, ROW_BLOCK)], xbufs[s], gsem.at[s]).wait()

    @pl.when((n == 0) & (m == 0))
    def _():
        gather(tok0_ref, 0)

    def copies(e, n):
        c0 = pl.multiple_of(n * tn, tn)
        c1 = pl.multiple_of((n + nf) * tn, tn)
        return [pltpu.make_async_copy(w_hbm.at[e, :, pl.ds(c0, tn)], stg_g, sem.at[0]),
                pltpu.make_async_copy(w_hbm.at[e, :, pl.ds(c1, tn)], stg_u, sem.at[1])]

    def convert():
        wgb[...] = stg_g[...].astype(BF16)
        wub[...] = stg_u[...].astype(BF16)

    active = _weight_stream(be_ref, na_ref, end_ref, nf, copies, convert)

    def step(s):
        gather_wait(s)
        gather(tokn_ref, 1 - s)
        x = _unpack_rows(xbufs[s][...])
        g = jnp.dot(x, wgb[...], preferred_element_type=F32) + bg_ref[...]
        u = jnp.dot(x, wub[...], preferred_element_type=F32) + bu_ref[...]
        g = jnp.minimum(g, SWIGLU_LIMIT)
        u = jnp.clip(u, -SWIGLU_LIMIT, SWIGLU_LIMIT)
        glu = g * jax.nn.sigmoid(SWIGLU_ALPHA * g)
        o_ref[...] = ((u + 1.0) * glu).astype(o_ref.dtype)

    last = (n == nf - 1) & (m == na - 1)
    for s in range(2):
        @pl.when(active & (slot == s))
        def _(s=s):
            step(s)

        @pl.when(active & last & (slot == s))
        def _(s=s):
            gather_wait(1 - s)

    @pl.when(jnp.logical_not(active))
    def _():
        o_ref[...] = jnp.zeros(o_ref.shape, o_ref.dtype)


def _moe_gu(hn_packed, src_tok, w_gu, b_gu, blk_e, n_active, run_end):
    hw = hn_packed.shape[1]
    nb = src_tok.shape[0]
    ne, d, f2 = w_gu.shape
    f = f2 // 2
    tn = 1024
    nf = f // tn
    mc = lambda m, na: jnp.minimum(m, na[0] - 1)
    tok_blk = lambda f_: pl.BlockSpec((1, 1, ROW_BLOCK), f_, memory_space=pltpu.SMEM)
    grid_spec = pltpu.PrefetchScalarGridSpec(
        num_scalar_prefetch=3,
        grid=(nf, nb),
        in_specs=[tok_blk(lambda n, m, be, na, en: (0, 0, 0)),
                  tok_blk(lambda n, m, be, na, en: (jnp.where(m + 1 < na[0], m + 1, 0), 0, 0)),
                  pl.BlockSpec(memory_space=pl.ANY),
                  pl.BlockSpec(memory_space=pl.ANY),
                  pl.BlockSpec((None, 1, tn), lambda n, m, be, na, en: (be[mc(m, na)], 0, n)),
                  pl.BlockSpec((None, 1, tn), lambda n, m, be, na, en: (be[mc(m, na)], 0, n + nf))],
        out_specs=pl.BlockSpec((ROW_BLOCK, tn), lambda n, m, be, na, en: (m, n)),
        scratch_shapes=[pltpu.VMEM((d, tn), F32), pltpu.VMEM((d, tn), F32),
                        pltpu.VMEM((d, tn), BF16), pltpu.VMEM((d, tn), BF16),
                        pltpu.VMEM((ROW_BLOCK, hw), U32), pltpu.VMEM((ROW_BLOCK, hw), U32),
                        pltpu.SemaphoreType.DMA((2,)), pltpu.SemaphoreType.DMA((2,))],
    )
    b3 = b_gu.reshape(ne, 1, f2)
    return pl.pallas_call(
        functools.partial(_moe_gu_kernel, tn=tn, nf=nf),
        grid_spec=grid_spec,
        out_shape=jax.ShapeDtypeStruct((nb * ROW_BLOCK, f), BF16),
        compiler_params=_cparams("arbitrary", "arbitrary"),
        name="moe_gu",
    )(blk_e, n_active, run_end, src_tok, src_tok, hn_packed, w_gu, b3, b3)


def _moe_down_kernel(be_ref, na_ref, end_ref, slot_ref, a_ref, w_hbm, b_ref, y_hbm,
                     stg, wb, ybuf0, ybuf1, sem, ssem, *, tn):
    m = pl.program_id(1)
    na = na_ref[0]
    p = m & 1
    ybufs = (ybuf0, ybuf1)

    def scatter(s):
        for r in range(ROW_BLOCK):
            pltpu.make_async_copy(ybufs[s].at[pl.ds(r, 1)], y_hbm.at[pl.ds(slot_ref[0, 0, r], 1)],
                                  ssem.at[s]).start()

    def scatter_wait(s):
        pltpu.make_async_copy(ybufs[s], y_hbm.at[pl.ds(0, ROW_BLOCK)], ssem.at[s]).wait()

    @pl.when(m == 0)
    def _():
        ybuf1[...] = jnp.zeros(ybuf1.shape, ybuf1.dtype)

    for s in range(2):
        @pl.when((m >= 1) & (m <= na + 1) & (p == s))
        def _(s=s):
            scatter_wait(s)

    def copies(e, n):
        return [pltpu.make_async_copy(w_hbm.at[e], stg, sem.at[0])]

    def convert():
        wb[...] = stg[...].astype(BF16)

    active = _weight_stream(be_ref, na_ref, end_ref, 1, copies, convert)

    for s in range(2):
        @pl.when(active & (p == s))
        def _(s=s):
            scatter(1 - s)
            ybufs[s][...] = jnp.dot(a_ref[...], wb[...], preferred_element_type=F32) + b_ref[...]

        @pl.when((m == na) & (p == s))
        def _(s=s):
            scatter(1 - s)


def _moe_down(act, slot_prev, w_down, b_down, blk_e, n_active, run_end, n_slots):
    n_rows, f = act.shape
    ne, _, d = w_down.shape
    nb = n_rows // ROW_BLOCK
    mc = lambda m, na: jnp.minimum(m, na[0] - 1)
    grid_spec = pltpu.PrefetchScalarGridSpec(
        num_scalar_prefetch=3,
        grid=(1, nb + 2),
        in_specs=[pl.BlockSpec((1, 1, ROW_BLOCK), lambda n, m, be, na, en: (jnp.minimum(m, nb), 0, 0),
                               memory_space=pltpu.SMEM),
                  pl.BlockSpec((ROW_BLOCK, f), lambda n, m, be, na, en: (mc(m, na), 0)),
                  pl.BlockSpec(memory_space=pl.ANY),
                  pl.BlockSpec((None, 1, d), lambda n, m, be, na, en: (be[mc(m, na)], 0, 0))],
        out_specs=pl.BlockSpec(memory_space=pl.ANY),
        scratch_shapes=[pltpu.VMEM((f, d), F32), pltpu.VMEM((f, d), BF16),
                        pltpu.VMEM((ROW_BLOCK, d), F32), pltpu.VMEM((ROW_BLOCK, d), F32),
                        pltpu.SemaphoreType.DMA((1,)), pltpu.SemaphoreType.DMA((2,))],
    )
    return pl.pallas_call(
        functools.partial(_moe_down_kernel, tn=d),
        grid_spec=grid_spec,
        out_shape=jax.ShapeDtypeStruct((n_slots + ROW_BLOCK, d), F32),
        compiler_params=_cparams("arbitrary", "arbitrary"),
        name="moe_down",
    )(blk_e, n_active, run_end, slot_prev, act, w_down, b_down.reshape(ne, 1, d))


def _combine_kernel(w_ref, h1_ref, g2_ref, *refs):
    y_refs, o_ref = refs[:TOP_K], refs[TOP_K]
    w = w_ref[...]
    acc = y_refs[0][...] * w[:, 0:1]
    for k in range(1, TOP_K):
        acc = acc + y_refs[k][...] * w[:, k:k + 1]
    o_ref[...] = h1_ref[...] + g2_ref[...] * acc


def _combine(top_w, h1, g2, y_slots):
    t, d = h1.shape
    tm = 256
    nt = t // tm
    return pl.pallas_call(
        _combine_kernel,
        grid=(nt,),
        in_specs=[pl.BlockSpec((tm, LANES), lambda i: (i, 0)),
                  pl.BlockSpec((tm, d), lambda i: (i, 0)),
                  pl.BlockSpec((1, d), lambda i: (0, 0))]
                 + [pl.BlockSpec((tm, d), lambda i, k=k: (k * nt + i, 0)) for k in range(TOP_K)],
        out_specs=pl.BlockSpec((tm, d), lambda i: (i, 0)),
        out_shape=jax.ShapeDtypeStruct((t, d), F32),
        compiler_params=_cparams("arbitrary"),
        name="combine",
    )(top_w, h1, g2, *([y_slots] * TOP_K))


def kernel(x, c, ada_w, ada_b, norm1_g, w_in, q_norm_g, k_norm_g, rel_bias, conv_w, conv_b,
           rg_a_w, rg_a_b, rg_x_w, rg_x_b, rg_lambda, w_attn_up, w_rnn_up, w_out, norm2_g,
           w_router, b_router, w_gu, b_gu, w_down, b_down):
    b, s, d = x.shape
    assert b == 1 and ada_w.shape[0] == 1, "single batch row, single layer"
    t = s
    x2 = x.reshape(t, d)
    aw = ATTN_HEADS * HEAD_DIM
    rw = conv_w.shape[2]

    mod = _ada(c, ada_w[0], ada_b[0])
    sh1, sc1, g1, sh2, sc2, g2 = [mod[:, k * d:(k + 1) * d] for k in range(6)]

    w_in_bf = w_in[0].astype(BF16)
    qg = q_norm_g[0].reshape(1, HEAD_DIM)
    kg = k_norm_g[0].reshape(1, HEAD_DIM)
    n1g = norm1_g[0].reshape(1, d)
    qkv = _inproj(x2, n1g, sc1, sh1, w_in_bf, qg, kg, col0=0,
                  groups=[(aw, "q"), (aw, "k"), (aw, "v")], out_dtype=BF16, name="inproj_qkv")
    rest = _inproj(x2, n1g, sc1, sh1, w_in_bf, qg, kg, col0=3 * aw,
                   groups=[(rw, "id"), (rw, "gelu"), (2 * d, "sigmoid")], out_dtype=F32,
                   name="inproj_rest")

    ya = _attn(qkv, _attn_distance_table(rel_bias[0]))
    yb = _rglru(rest, conv_w[0], conv_b[0].reshape(1, rw),
                rg_a_w[0].astype(BF16), rg_a_b[0].reshape(1, rw),
                rg_x_w[0].astype(BF16), rg_x_b[0].reshape(1, rw), rg_lambda[0].reshape(1, rw))

    h1, hn_packed, logits = _mixer_out(
        ya, yb, rest, x2, w_attn_up[0].astype(BF16), w_rnn_up[0].astype(BF16), w_out[0].astype(BF16),
        g1, norm2_g[0].reshape(1, d), sc2, sh2, w_router[0], b_router[0].reshape(1, N_EXPERTS))

    idx_w, topw_w, pos_w, cnt = _route(logits)
    idx = idx_w[:, :TOP_K]
    pos = pos_w[:, :TOP_K]
    counts = cnt[0].astype(I32)
    padded = (counts + ROW_BLOCK - 1) // ROW_BLOCK * ROW_BLOCK
    pend = jnp.cumsum(padded)
    pstart = pend - padded
    experts = jnp.arange(N_EXPERTS, dtype=I32)
    dest = (pos + jnp.sum(jnp.where(idx[:, :, None] == experts, pstart, 0), axis=-1)).reshape(-1)
    n_blocks = (t * TOP_K) // ROW_BLOCK + N_EXPERTS
    n_rows = n_blocks * ROW_BLOCK
    blk_row0 = jnp.arange(n_blocks, dtype=I32) * ROW_BLOCK
    blk_e = jnp.minimum(jnp.sum((pend[None, :] <= blk_row0[:, None]).astype(I32), axis=1), N_EXPERTS - 1)
    n_active = (pend[-1:] // ROW_BLOCK).astype(I32)
    run_end = (pend // ROW_BLOCK).astype(I32)
    n_slots = t * TOP_K
    slot_ids = (jnp.arange(t, dtype=I32)[:, None] + jnp.arange(TOP_K, dtype=I32)[None, :] * t).reshape(-1)
    spare = n_slots + jnp.arange(ROW_BLOCK, dtype=I32)
    row_slot = jnp.tile(spare, n_blocks).at[dest].set(slot_ids)
    src_tok = jnp.where(row_slot < n_slots, row_slot % t, 0).reshape(n_blocks, 1, ROW_BLOCK)
    slot_prev = jnp.concatenate([spare, row_slot]).reshape(n_blocks + 1, 1, ROW_BLOCK)

    act = _moe_gu(hn_packed, src_tok, w_gu[0], b_gu[0], blk_e, n_active, run_end)
    y_slots = _moe_down(act, slot_prev, w_down[0], b_down[0], blk_e, n_active, run_end, n_slots)
    out = _combine(topw_w, h1, g2, y_slots)
    return out.reshape(b, s, d)
```

```python
import functools
import math

import jax
import jax.numpy as jnp
from jax import lax
from jax.experimental import pallas as pl
from jax.experimental.pallas import tpu as pltpu

F32 = jnp.float32
BF16 = jnp.bfloat16
U32 = jnp.uint32
I32 = jnp.int32

CHUNK = 64
LEFT_CHUNKS = 8
ATTN_HEADS = 8
HEAD_DIM = 128
REL_MAX = 256
REL_MIN = -(CHUNK - 1)
RNN_BLOCKS = 8
CONV_W = 4
RG_C = 8.0
N_EXPERTS = 32
TOP_K = 4
SWIGLU_LIMIT = 7.0
SWIGLU_ALPHA = 1.702
EPS = 1e-6
NEG_BIG = -1e30

LANES = 128
SUBLANES = 8
VMEM_LIMIT = 56 * 1024 * 1024

ROW_BLOCK = 512
MOE_BUFS = 3
PROJ_TM = 1024
PROJ_TN = 1024
ATTN_QB = 4 * CHUNK
ATTN_KB = ATTN_QB + LEFT_CHUNKS * CHUNK
ATTN_TAB = ATTN_KB + ATTN_QB
CHUNK_SHIFT = CHUNK.bit_length() - 1
assert 1 << CHUNK_SHIFT == CHUNK


def _cparams(*sem):
    return pltpu.CompilerParams(dimension_semantics=sem, vmem_limit_bytes=VMEM_LIMIT)


def _ada_kernel(c_ref, w_ref, b_ref, o_ref):
    c = c_ref[...]
    ca = c * jax.nn.sigmoid(c)
    o_ref[...] = jnp.dot(ca, w_ref[...], preferred_element_type=F32,
                         precision=lax.Precision.HIGHEST) + b_ref[...]


def _ada(c, ada_w, ada_b):
    d, n = ada_w.shape
    tn = 1024
    c8 = jnp.broadcast_to(c, (SUBLANES, d))
    out = pl.pallas_call(
        _ada_kernel,
        grid=(n // tn,),
        in_specs=[pl.BlockSpec((SUBLANES, d), lambda j: (0, 0)),
                  pl.BlockSpec((d, tn), lambda j: (0, j)),
                  pl.BlockSpec((1, tn), lambda j: (0, j))],
        out_specs=pl.BlockSpec((SUBLANES, tn), lambda j: (0, j)),
        out_shape=jax.ShapeDtypeStruct((SUBLANES, n), F32),
        compiler_params=_cparams("arbitrary"),
        name="ada",
    )(c8, ada_w, ada_b.reshape(1, n))
    return out[0:1]


def _gelu_tanh(x):
    return 0.5 * x * (1.0 + jnp.tanh(math.sqrt(2.0 / math.pi) * (x + 0.044715 * (x * x * x))))


def _head_rms(a, g):
    outs = []
    for hh in range(a.shape[1] // HEAD_DIM):
        s = a[:, hh * HEAD_DIM:(hh + 1) * HEAD_DIM]
        ms = jnp.mean(s * s, axis=-1, keepdims=True)
        outs.append(s * lax.rsqrt(ms + EPS) * g)
    return jnp.concatenate(outs, axis=1)


def _norm_mod_kernel(x_ref, g_ref, sc_ref, sh_ref, o_ref):
    x = x_ref[...]
    ms = jnp.mean(x * x, axis=-1, keepdims=True)
    xn = x * lax.rsqrt(ms + EPS) * g_ref[...]
    o_ref[...] = (xn * (1.0 + sc_ref[...]) + sh_ref[...]).astype(o_ref.dtype)


def _norm_mod(x2, g, sc, sh):
    t, d = x2.shape
    tm = 512
    vec = lambda: pl.BlockSpec((1, d), lambda i: (0, 0))
    return pl.pallas_call(
        _norm_mod_kernel,
        grid=(t // tm,),
        in_specs=[pl.BlockSpec((tm, d), lambda i: (i, 0)), vec(), vec(), vec()],
        out_specs=pl.BlockSpec((tm, d), lambda i: (i, 0)),
        out_shape=jax.ShapeDtypeStruct((t, d), BF16),
        compiler_params=_cparams("arbitrary"),
        name="norm1",
    )(x2, g, sc, sh)


def _proj_kernel(x_ref, w_ref, hg_ref, o_ref, *, kind):
    acc = jnp.dot(x_ref[...], w_ref[...], preferred_element_type=F32)
    if kind == "q":
        r = _head_rms(acc, hg_ref[...]) * (1.0 / math.sqrt(HEAD_DIM))
    elif kind == "k":
        r = _head_rms(acc, hg_ref[...])
    elif kind == "gelu":
        r = _gelu_tanh(acc)
    elif kind == "sigmoid":
        r = jax.nn.sigmoid(acc)
    else:
        r = acc
    o_ref[...] = r.astype(o_ref.dtype)


def _proj(xn, w_bf, head_gain, *, col0, width, kind, out_dtype, name):
    t, d = xn.shape
    tm, tn = PROJ_TM, PROJ_TN
    j0 = col0 // tn
    return pl.pallas_call(
        functools.partial(_proj_kernel, kind=kind),
        grid=(t // tm, width // tn),
        in_specs=[pl.BlockSpec((tm, d), lambda i, j: (i, 0)),
                  pl.BlockSpec((d, tn), lambda i, j: (0, j + j0)),
                  pl.BlockSpec((1, HEAD_DIM), lambda i, j: (0, 0))],
        out_specs=pl.BlockSpec((tm, tn), lambda i, j: (i, j)),
        out_shape=jax.ShapeDtypeStruct((t, width), out_dtype),
        compiler_params=_cparams("arbitrary", "arbitrary"),
        name=name,
    )(xn, w_bf, head_gain)


def _attn_kernel(q_ref, k0_ref, k1_ref, k2_ref, v0_ref, v1_ref, v2_ref, ftab_ref, o_ref, bias_ref):
    i = pl.program_id(0)
    col = lax.broadcasted_iota(I32, (ATTN_QB, ATTN_KB), 1)

    @pl.when(i == 0)
    def _():
        row = lax.broadcasted_iota(I32, (ATTN_QB, ATTN_KB), 0)
        cq = row >> CHUNK_SHIFT
        ck = col >> CHUNK_SHIFT
        visible = (ck >= cq) & (ck <= cq + LEFT_CHUNKS)
        for h in range(ATTN_HEADS):
            g = jnp.broadcast_to(ftab_ref[h:h + 1, :], (ATTN_QB, ATTN_TAB))
            y = pltpu.roll(g, 0, 1, stride=1, stride_axis=0)
            bias_ref[h] = jnp.where(visible, y[:, :ATTN_KB], NEG_BIG)

    in_seq = col >= (LEFT_CHUNKS * CHUNK - i * ATTN_QB)
    for h in range(ATTN_HEADS):
        sl = slice(h * HEAD_DIM, (h + 1) * HEAD_DIM)
        q = q_ref[:, sl]
        k = jnp.concatenate([k0_ref[:, sl], k1_ref[:, sl], k2_ref[:, sl]], axis=0)
        v = jnp.concatenate([v0_ref[:, sl], v1_ref[:, sl], v2_ref[:, sl]], axis=0)
        s = lax.dot_general(q, k, (((1,), (1,)), ((), ())), preferred_element_type=F32)
        s = jnp.where(in_seq, s + bias_ref[h], NEG_BIG)
        m = jnp.max(s, axis=-1, keepdims=True)
        p = jnp.exp(s - m)
        l = jnp.sum(p, axis=-1, keepdims=True)
        o = jnp.dot(p.astype(BF16), v, preferred_element_type=F32) / l
        o_ref[:, sl] = o.astype(o_ref.dtype)


def _attn(q, k, v, ftab):
    t = q.shape[0]
    w = ATTN_HEADS * HEAD_DIM
    qb = ATTN_QB
    blk = lambda f: pl.BlockSpec((qb, w), f)
    return pl.pallas_call(
        _attn_kernel,
        grid=(t // qb,),
        in_specs=[blk(lambda i: (i, 0)),
                  blk(lambda i: (jnp.maximum(i - 2, 0), 0)), blk(lambda i: (jnp.maximum(i - 1, 0), 0)),
                  blk(lambda i: (i, 0)),
                  blk(lambda i: (jnp.maximum(i - 2, 0), 0)), blk(lambda i: (jnp.maximum(i - 1, 0), 0)),
                  blk(lambda i: (i, 0)),
                  pl.BlockSpec((ATTN_HEADS, ATTN_TAB), lambda i: (0, 0))],
        out_specs=blk(lambda i: (i, 0)),
        out_shape=jax.ShapeDtypeStruct((t, w), BF16),
        scratch_shapes=[pltpu.VMEM((ATTN_HEADS, qb, ATTN_KB), F32)],
        compiler_params=_cparams("arbitrary"),
        name="attn",
    )(q, k, k, k, v, v, v, ftab)


def _attn_distance_table(rel_bias):
    pad = LEFT_CHUNKS * CHUNK
    n_far = pad - REL_MAX + 1
    n_mid = REL_MAX - REL_MIN
    n_ahead = ATTN_KB - n_far - n_mid
    far = rel_bias[:, -1:]
    return jnp.concatenate([jnp.broadcast_to(far, (ATTN_HEADS, n_far)),
                            rel_bias[:, :n_mid][:, ::-1],
                            jnp.broadcast_to(rel_bias[:, :1], (ATTN_HEADS, n_ahead)),
                            jnp.broadcast_to(far, (ATTN_HEADS, ATTN_TAB - ATTN_KB))], axis=1)


def _rglru_kernel(rx_ref, gate_ref, cw_ref, cb_ref, wa_ref, ba_ref, wx_ref, bx_ref, lam_ref,
                  o_ref, xbuf, a_s, u_s, hc, *, tt):
    i = pl.program_id(0)
    width = rx_ref.shape[1]
    bw = width // RNN_BLOCKS

    @pl.when(i == 0)
    def _():
        xbuf[0:SUBLANES, :] = jnp.zeros((SUBLANES, width), F32)
        hc[...] = jnp.zeros(hc.shape, F32)

    xbuf[SUBLANES:SUBLANES + tt, :] = rx_ref[...]
    xc = cb_ref[...] + cw_ref[CONV_W - 1:CONV_W, :] * xbuf[SUBLANES:SUBLANES + tt, :]
    for d in range(1, CONV_W):
        xc = xc + cw_ref[CONV_W - 1 - d:CONV_W - d, :] * xbuf[SUBLANES - d:SUBLANES - d + tt, :]
    tail = xbuf[tt:tt + SUBLANES, :]
    xbuf[0:SUBLANES, :] = tail

    z = -lam_ref[...]
    softplus = jnp.maximum(z, 0.0) + jnp.log1p(jnp.exp(-jnp.abs(z)))
    xcb = xc.astype(BF16)
    for b in range(RNN_BLOCKS):
        sl = slice(b * bw, (b + 1) * bw)
        xb_ = xcb[:, sl]
        r = jax.nn.sigmoid(jnp.dot(xb_, wa_ref[b], preferred_element_type=F32) + ba_ref[:, sl])
        ig = jax.nn.sigmoid(jnp.dot(xb_, wx_ref[b], preferred_element_type=F32) + bx_ref[:, sl])
        log_a = (-RG_C) * r * softplus[:, sl]
        a = jnp.exp(log_a)
        th = jnp.tanh(log_a)
        one_m_a2 = (-2.0) * th / (1.0 - th)
        a_s[:, sl] = a
        u_s[:, sl] = jnp.sqrt(one_m_a2) * (ig * xc[:, sl])

    row = lax.broadcasted_iota(I32, (SUBLANES, width), 0)

    def body(g, h):
        r0 = pl.multiple_of(g * SUBLANES, SUBLANES)
        a = a_s[pl.ds(r0, SUBLANES), :]
        u = u_s[pl.ds(r0, SUBLANES), :]
        for s in (1, 2, 4):
            a_sh = pltpu.roll(a, s, 0)
            u_sh = pltpu.roll(u, s, 0)
            keep = row >= s
            u = jnp.where(keep, a * u_sh + u, u)
            a = jnp.where(keep, a * a_sh, a)
        hh = a * h + u
        u_s[pl.ds(r0, SUBLANES), :] = hh
        return hh[SUBLANES - 1:SUBLANES, :]

    h_last = lax.fori_loop(0, tt // SUBLANES, body, hc[0:1, :])
    hc[0:1, :] = h_last
    o_ref[...] = (u_s[...] * gate_ref[...]).astype(o_ref.dtype)


def _rglru(rx, gate, conv_w, conv_b, wa_bf, ba, wx_bf, bx, lam):
    t = rx.shape[0]
    width = conv_w.shape[1]
    bw = width // RNN_BLOCKS
    tt = 512
    vec = lambda: pl.BlockSpec((1, width), lambda i: (0, 0))
    wblk = lambda: pl.BlockSpec((RNN_BLOCKS, bw, bw), lambda i: (0, 0, 0))
    return pl.pallas_call(
        functools.partial(_rglru_kernel, tt=tt),
        grid=(t // tt,),
        in_specs=[pl.BlockSpec((tt, width), lambda i: (i, 0)),
                  pl.BlockSpec((tt, width), lambda i: (i, 0)),
                  pl.BlockSpec((CONV_W, width), lambda i: (0, 0)), vec(),
                  wblk(), vec(), wblk(), vec(), vec()],
        out_specs=pl.BlockSpec((tt, width), lambda i: (i, 0)),
        out_shape=jax.ShapeDtypeStruct((t, width), BF16),
        scratch_shapes=[pltpu.VMEM((tt + SUBLANES, width), F32),
                        pltpu.VMEM((tt, width), F32),
                        pltpu.VMEM((tt, width), F32),
                        pltpu.VMEM((SUBLANES, width), F32)],
        compiler_params=_cparams("arbitrary"),
        name="rglru",
    )(rx, gate, conv_w, conv_b, wa_bf, ba, wx_bf, bx, lam)


def _upmerge_kernel(ya_ref, yb_ref, sga_ref, sgb_ref, wa_ref, wr_ref, o_ref):
    up_a = jnp.dot(ya_ref[...], wa_ref[...], preferred_element_type=F32)
    up_b = jnp.dot(yb_ref[...], wr_ref[...], preferred_element_type=F32)
    o_ref[...] = (sga_ref[...] * up_a + sgb_ref[...] * up_b).astype(o_ref.dtype)


def _upmerge(ya, yb, sg, wa_bf, wr_bf):
    t, aw = ya.shape
    rw = yb.shape[1]
    d = wa_bf.shape[1]
    tm, tn = PROJ_TM, PROJ_TN
    nj = d // tn
    return pl.pallas_call(
        _upmerge_kernel,
        grid=(t // tm, nj),
        in_specs=[pl.BlockSpec((tm, aw), lambda i, j: (i, 0)),
                  pl.BlockSpec((tm, rw), lambda i, j: (i, 0)),
                  pl.BlockSpec((tm, tn), lambda i, j: (i, j)),
                  pl.BlockSpec((tm, tn), lambda i, j: (i, j + nj)),
                  pl.BlockSpec((aw, tn), lambda i, j: (0, j)),
                  pl.BlockSpec((rw, tn), lambda i, j: (0, j))],
        out_specs=pl.BlockSpec((tm, tn), lambda i, j: (i, j)),
        out_shape=jax.ShapeDtypeStruct((t, d), BF16),
        compiler_params=_cparams("arbitrary", "arbitrary"),
        name="upmerge",
    )(ya, yb, sg, sg, wa_bf, wr_bf)


def _mixer_out_kernel(mg_ref, x_ref, wo_ref, g1_ref, n2_ref, sc2_ref, sh2_ref, wrh_ref, brt_ref,
                      h1_ref, hnp_ref, lg_ref):
    proj = jnp.dot(mg_ref[...], wo_ref[...], preferred_element_type=F32)
    h1 = x_ref[...] + g1_ref[...] * proj
    h1_ref[...] = h1
    ms = jnp.mean(h1 * h1, axis=-1, keepdims=True)
    hn = h1 * lax.rsqrt(ms + EPS) * n2_ref[...]
    hn = hn * (1.0 + sc2_ref[...]) + sh2_ref[...]
    hb = hn.astype(BF16)
    hb32 = hb.astype(F32)
    hl = (hn - hb32).astype(BF16)
    ne = lg_ref.shape[1]
    hh = jnp.dot(hb, wrh_ref[...], preferred_element_type=F32)
    lh = jnp.dot(hl, wrh_ref[:, :ne], preferred_element_type=F32)
    lg_ref[...] = hh[:, :ne] + hh[:, ne:] + lh + brt_ref[...]
    half = hn.shape[1] // 2
    lo = pltpu.bitcast(hb32[:, :half], U32) >> 16
    hi = pltpu.bitcast(hb32[:, half:], U32) & jnp.uint32(0xFFFF0000)
    hnp_ref[...] = hi | lo


def _mixer_out(merged, x2, wo_bf, g1, n2g, sc2, sh2, w_router, b_router):
    t, d = x2.shape
    ne = w_router.shape[1]
    tm = 512
    w_hi = w_router.astype(BF16)
    w_lo = (w_router - w_hi.astype(F32)).astype(BF16)
    w_split = jnp.concatenate([w_hi, w_lo], axis=1)
    const = lambda shape: pl.BlockSpec(shape, lambda i: (0,) * len(shape), pipeline_mode=pl.Buffered(1))
    return pl.pallas_call(
        _mixer_out_kernel,
        grid=(t // tm,),
        in_specs=[pl.BlockSpec((tm, d), lambda i: (i, 0)),
                  pl.BlockSpec((tm, d), lambda i: (i, 0)),
                  const((d, d)),
                  const((1, d)), const((1, d)), const((1, d)), const((1, d)),
                  const((d, 2 * ne)), const((1, ne))],
        out_specs=[pl.BlockSpec((tm, d), lambda i: (i, 0)),
                   pl.BlockSpec((tm, d // 2), lambda i: (i, 0)),
                   pl.BlockSpec((tm, ne), lambda i: (i, 0))],
        out_shape=[jax.ShapeDtypeStruct((t, d), F32),
                   jax.ShapeDtypeStruct((t, d // 2), U32),
                   jax.ShapeDtypeStruct((t, ne), F32)],
        compiler_params=_cparams("arbitrary"),
        name="mixer_out",
    )(merged, x2, wo_bf, g1, n2g, sc2, sh2, w_split, b_router)


def _route_kernel(lg_ref, idx_ref, w_ref, pos_ref, cnt_ref, carry, *, rt):
    i = pl.program_id(0)
    ne = lg_ref.shape[1]

    @pl.when(i == 0)
    def _():
        carry[...] = jnp.zeros(carry.shape, F32)

    l = lg_ref[...]
    lane = lax.broadcasted_iota(I32, (rt, ne), 1).astype(F32)
    vals, idxs, hots = [], [], []
    for _ in range(TOP_K):
        m = jnp.max(l, axis=-1, keepdims=True)
        ik = jnp.min(jnp.where(l == m, lane, float(ne)), axis=-1, keepdims=True)
        hot = lane == ik
        vals.append(m)
        idxs.append(ik)
        hots.append(hot)
        l = jnp.where(hot, -jnp.inf, l)
    es = [jnp.exp(v - vals[0]) for v in vals]
    tot = es[0] + es[1] + es[2] + es[3]
    sel = jnp.zeros((rt, ne), F32)
    for hot in hots:
        sel = sel + hot.astype(F32)
    rr = lax.broadcasted_iota(I32, (rt, rt), 0)
    cc = lax.broadcasted_iota(I32, (rt, rt), 1)
    tri = (cc < rr).astype(BF16)
    before = jnp.dot(tri, sel.astype(BF16), preferred_element_type=F32) + carry[0:1, :]
    out_lane = lax.broadcasted_iota(I32, (rt, LANES), 1)
    idx_o = jnp.zeros((rt, LANES), F32)
    w_o = jnp.zeros((rt, LANES), F32)
    pos_o = jnp.zeros((rt, LANES), F32)
    for k in range(TOP_K):
        pk = jnp.sum(jnp.where(hots[k], before, 0.0), axis=-1, keepdims=True)
        idx_o = jnp.where(out_lane == k, idxs[k], idx_o)
        w_o = jnp.where(out_lane == k, es[k] / tot, w_o)
        pos_o = jnp.where(out_lane == k, pk, pos_o)
    idx_ref[...] = idx_o.astype(I32)
    w_ref[...] = w_o
    pos_ref[...] = pos_o.astype(I32)
    carry[0:1, :] = carry[0:1, :] + jnp.sum(sel, axis=0, keepdims=True)
    cnt_ref[...] = carry[...]


def _route(logits):
    t, ne = logits.shape
    rt = 512
    wide = lambda dt: jax.ShapeDtypeStruct((t, LANES), dt)
    return pl.pallas_call(
        functools.partial(_route_kernel, rt=rt),
        grid=(t // rt,),
        in_specs=[pl.BlockSpec((rt, ne), lambda i: (i, 0))],
        out_specs=[pl.BlockSpec((rt, LANES), lambda i: (i, 0)),
                   pl.BlockSpec((rt, LANES), lambda i: (i, 0)),
                   pl.BlockSpec((rt, LANES), lambda i: (i, 0)),
                   pl.BlockSpec((SUBLANES, ne), lambda i: (0, 0))],
        out_shape=[wide(I32), wide(F32), wide(I32), jax.ShapeDtypeStruct((SUBLANES, ne), F32)],
        scratch_shapes=[pltpu.VMEM((SUBLANES, ne), F32)],
        compiler_params=_cparams("arbitrary"),
        name="route",
    )(logits)


def _unpack_rows(p):
    lo = pltpu.bitcast(p << 16, F32).astype(BF16)
    hi = pltpu.bitcast(p & jnp.uint32(0xFFFF0000), F32).astype(BF16)
    return jnp.concatenate([lo, hi], axis=1)


def _first_of_expert(be_ref, m, na):
    mm = jnp.minimum(m, na - 1)
    prev = be_ref[jnp.maximum(mm - 1, 0)]
    return (mm == 0) | (be_ref[mm] != prev)


def _weight_stream(be_ref, na_ref, end_ref, n_tiles, copies, convert):
    n = pl.program_id(0)
    m = pl.program_id(1)
    na = na_ref[0]
    active = m < na

    @pl.when((n == 0) & (m == 0))
    def _():
        for cp in copies(be_ref[0], 0):
            cp.start()

    @pl.when(active & _first_of_expert(be_ref, m, na))
    def _():
        e = be_ref[m]
        for cp in copies(e, n):
            cp.wait()
        convert()
        run_end = end_ref[e]
        wrap = run_end >= na
        n2 = jnp.where(wrap, n + 1, n)
        m2 = jnp.where(wrap, 0, run_end)

        @pl.when(n2 < n_tiles)
        def _():
            for cp in copies(be_ref[m2], n2):
                cp.start()

    return active


def _moe_gu_kernel(be_ref, na_ref, end_ref, tok0_ref, tok1_ref, tok2_ref, hn_hbm, w_hbm, bg_ref, bu_ref,
                   o_ref, stg_g, stg_u, wgb, wub, xbuf0, xbuf1, xbuf2, sem, gsem, *, tn, nf):
    n = pl.program_id(0)
    m = pl.program_id(1)
    na = na_ref[0]
    slot = (n * na + m) % MOE_BUFS
    xbufs = (xbuf0, xbuf1, xbuf2)

    def gather(tok_ref, s):
        for r in range(ROW_BLOCK):
            pltpu.make_async_copy(hn_hbm.at[pl.ds(tok_ref[0, 0, r], 1)], xbufs[s].at[pl.ds(r, 1)],
                                  gsem.at[s]).start()

    def gather_wait(s):
        pltpu.make_async_copy(hn_hbm.at[pl.ds(0, ROW_BLOCK)], xbufs[s], gsem.at[s]).wait()

    @pl.when((n == 0) & (m == 0))
    def _():
        gather(tok0_ref, 0)
        gather(tok1_ref, 1)

    def copies(e, n):
        c0 = pl.multiple_of(n * tn, tn)
        c1 = pl.multiple_of((n + nf) * tn, tn)
        return [pltpu.make_async_copy(w_hbm.at[e, :, pl.ds(c0, tn)], stg_g, sem.at[0]),
                pltpu.make_async_copy(w_hbm.at[e, :, pl.ds(c1, tn)], stg_u, sem.at[1])]

    def convert():
        wgb[...] = stg_g[...].astype(BF16)
        wub[...] = stg_u[...].astype(BF16)

    active = _weight_stream(be_ref, na_ref, end_ref, nf, copies, convert)

    def step(s):
        gather_wait(s)
        gather(tok2_ref, (s + 2) % MOE_BUFS)
        x = _unpack_rows(xbufs[s][...])
        g = jnp.dot(x, wgb[...], preferred_element_type=F32) + bg_ref[...]
        u = jnp.dot(x, wub[...], preferred_element_type=F32) + bu_ref[...]
        g = jnp.minimum(g, SWIGLU_LIMIT)
        u = jnp.clip(u, -SWIGLU_LIMIT, SWIGLU_LIMIT)
        glu = g * jax.nn.sigmoid(SWIGLU_ALPHA * g)
        o_ref[...] = ((u + 1.0) * glu).astype(o_ref.dtype)

    last = (n == nf - 1) & (m == na - 1)
    for s in range(MOE_BUFS):
        @pl.when(active & (slot == s))
        def _(s=s):
            step(s)

        @pl.when(active & last & (slot == s))
        def _(s=s):
            gather_wait((s + 1) % MOE_BUFS)
            gather_wait((s + 2) % MOE_BUFS)

    @pl.when(jnp.logical_not(active))
    def _():
        o_ref[...] = jnp.zeros(o_ref.shape, o_ref.dtype)


def _moe_gu(hn_packed, src_tok, w_gu, b_gu, blk_e, n_active, run_end):
    hw = hn_packed.shape[1]
    nb = src_tok.shape[0]
    ne, d, f2 = w_gu.shape
    f = f2 // 2
    tn = 1024
    nf = f // tn
    mc = lambda m, na: jnp.minimum(m, na[0] - 1)
    tok_blk = lambda f_: pl.BlockSpec((1, 1, ROW_BLOCK), f_, memory_space=pltpu.SMEM)
    grid_spec = pltpu.PrefetchScalarGridSpec(
        num_scalar_prefetch=3,
        grid=(nf, nb),
        in_specs=[tok_blk(lambda n, m, be, na, en: (0, 0, 0)),
                  tok_blk(lambda n, m, be, na, en: (1 % na[0], 0, 0)),
                  tok_blk(lambda n, m, be, na, en: ((m + 2) % na[0], 0, 0)),
                  pl.BlockSpec(memory_space=pl.ANY),
                  pl.BlockSpec(memory_space=pl.ANY),
                  pl.BlockSpec((None, 1, tn), lambda n, m, be, na, en: (be[mc(m, na)], 0, n)),
                  pl.BlockSpec((None, 1, tn), lambda n, m, be, na, en: (be[mc(m, na)], 0, n + nf))],
        out_specs=pl.BlockSpec((ROW_BLOCK, tn), lambda n, m, be, na, en: (m, n)),
        scratch_shapes=[pltpu.VMEM((d, tn), F32), pltpu.VMEM((d, tn), F32),
                        pltpu.VMEM((d, tn), BF16), pltpu.VMEM((d, tn), BF16),
                        *[pltpu.VMEM((ROW_BLOCK, hw), U32)] * MOE_BUFS,
                        pltpu.SemaphoreType.DMA((2,)), pltpu.SemaphoreType.DMA((MOE_BUFS,))],
    )
    b3 = b_gu.reshape(ne, 1, f2)
    return pl.pallas_call(
        functools.partial(_moe_gu_kernel, tn=tn, nf=nf),
        grid_spec=grid_spec,
        out_shape=jax.ShapeDtypeStruct((nb * ROW_BLOCK, f), BF16),
        compiler_params=_cparams("arbitrary", "arbitrary"),
        name="moe_gu",
    )(blk_e, n_active, run_end, src_tok, src_tok, src_tok, hn_packed, w_gu, b3, b3)


def _moe_down_kernel(be_ref, na_ref, end_ref, slot_ref, a_ref, w_hbm, b_ref, y_hbm,
                     stg, wb, ybuf0, ybuf1, ybuf2, sem, ssem, *, tn):
    m = pl.program_id(1)
    na = na_ref[0]
    p = m % MOE_BUFS
    ybufs = (ybuf0, ybuf1, ybuf2)

    def scatter(s):
        for r in range(ROW_BLOCK):
            pltpu.make_async_copy(ybufs[s].at[pl.ds(r, 1)], y_hbm.at[pl.ds(slot_ref[0, 0, r], 1)],
                                  ssem.at[s]).start()

    def scatter_wait(s):
        pltpu.make_async_copy(ybufs[s], y_hbm.at[pl.ds(0, ROW_BLOCK)], ssem.at[s]).wait()

    @pl.when(m == 0)
    def _():
        ybuf2[...] = jnp.zeros(ybuf2.shape, ybuf2.dtype)

    for s in range(MOE_BUFS):
        @pl.when((m >= 2) & (m <= na + 2) & (p == s))
        def _(s=s):
            scatter_wait(s)

    def copies(e, n):
        return [pltpu.make_async_copy(w_hbm.at[e], stg, sem.at[0])]

    def convert():
        wb[...] = stg[...].astype(BF16)

    active = _weight_stream(be_ref, na_ref, end_ref, 1, copies, convert)

    for s in range(MOE_BUFS):
        @pl.when(active & (p == s))
        def _(s=s):
            scatter((s + MOE_BUFS - 1) % MOE_BUFS)
            ybufs[s][...] = jnp.dot(a_ref[...], wb[...], preferred_element_type=F32) + b_ref[...]

        @pl.when((m == na) & (p == s))
        def _(s=s):
            scatter((s + MOE_BUFS - 1) % MOE_BUFS)


def _moe_down(act, slot_prev, w_down, b_down, blk_e, n_active, run_end, n_slots):
    n_rows, f = act.shape
    ne, _, d = w_down.shape
    nb = n_rows // ROW_BLOCK
    mc = lambda m, na: jnp.minimum(m, na[0] - 1)
    grid_spec = pltpu.PrefetchScalarGridSpec(
        num_scalar_prefetch=3,
        grid=(1, nb + MOE_BUFS),
        in_specs=[pl.BlockSpec((1, 1, ROW_BLOCK), lambda n, m, be, na, en: (jnp.minimum(m, nb), 0, 0),
                               memory_space=pltpu.SMEM),
                  pl.BlockSpec((ROW_BLOCK, f), lambda n, m, be, na, en: (mc(m, na), 0)),
                  pl.BlockSpec(memory_space=pl.ANY),
                  pl.BlockSpec((None, 1, d), lambda n, m, be, na, en: (be[mc(m, na)], 0, 0))],
        out_specs=pl.BlockSpec(memory_space=pl.ANY),
        scratch_shapes=[pltpu.VMEM((f, d), F32), pltpu.VMEM((f, d), BF16),
                        *[pltpu.VMEM((ROW_BLOCK, d), F32)] * MOE_BUFS,
                        pltpu.SemaphoreType.DMA((1,)), pltpu.SemaphoreType.DMA((MOE_BUFS,))],
    )
    return pl.pallas_call(
        functools.partial(_moe_down_kernel, tn=d),
        grid_spec=grid_spec,
        out_shape=jax.ShapeDtypeStruct((n_slots + ROW_BLOCK, d), F32),
        compiler_params=_cparams("arbitrary", "arbitrary"),
        name="moe_down",
    )(blk_e, n_active, run_end, slot_prev, act, w_down, b_down.reshape(ne, 1, d))


def _combine_kernel(w_ref, h1_ref, g2_ref, *refs):
    y_refs, o_ref = refs[:TOP_K], refs[TOP_K]
    w = w_ref[...]
    acc = y_refs[0][...] * w[:, 0:1]
    for k in range(1, TOP_K):
        acc = acc + y_refs[k][...] * w[:, k:k + 1]
    o_ref[...] = h1_ref[...] + g2_ref[...] * acc


def _combine(top_w, h1, g2, y_slots):
    t, d = h1.shape
    tm = 256
    nt = t // tm
    return pl.pallas_call(
        _combine_kernel,
        grid=(nt,),
        in_specs=[pl.BlockSpec((tm, LANES), lambda i: (i, 0)),
                  pl.BlockSpec((tm, d), lambda i: (i, 0)),
                  pl.BlockSpec((1, d), lambda i: (0, 0))]
                 + [pl.BlockSpec((tm, d), lambda i, k=k: (k * nt + i, 0)) for k in range(TOP_K)],
        out_specs=pl.BlockSpec((tm, d), lambda i: (i, 0)),
        out_shape=jax.ShapeDtypeStruct((t, d), F32),
        compiler_params=_cparams("arbitrary"),
        name="combine",
    )(top_w, h1, g2, *([y_slots] * TOP_K))


def kernel(x, c, ada_w, ada_b, norm1_g, w_in, q_norm_g, k_norm_g, rel_bias, conv_w, conv_b,
           rg_a_w, rg_a_b, rg_x_w, rg_x_b, rg_lambda, w_attn_up, w_rnn_up, w_out, norm2_g,
           w_router, b_router, w_gu, b_gu, w_down, b_down):
    b, s, d = x.shape
    assert b == 1 and ada_w.shape[0] == 1, "single batch row, single layer"
    t = s
    x2 = x.reshape(t, d)
    aw = ATTN_HEADS * HEAD_DIM
    rw = conv_w.shape[2]

    mod = _ada(c, ada_w[0], ada_b[0])
    sh1, sc1, g1, sh2, sc2, g2 = [mod[:, k * d:(k + 1) * d] for k in range(6)]

    w_in_bf = w_in[0].astype(BF16)
    qg = q_norm_g[0].reshape(1, HEAD_DIM)
    kg = k_norm_g[0].reshape(1, HEAD_DIM)
    n1g = norm1_g[0].reshape(1, d)
    xn = _norm_mod(x2, n1g, sc1, sh1)
    q = _proj(xn, w_in_bf, qg, col0=0, width=aw, kind="q", out_dtype=BF16, name="proj_q")
    k = _proj(xn, w_in_bf, kg, col0=aw, width=aw, kind="k", out_dtype=BF16, name="proj_k")
    v = _proj(xn, w_in_bf, kg, col0=2 * aw, width=aw, kind="id", out_dtype=BF16, name="proj_v")
    rx = _proj(xn, w_in_bf, kg, col0=3 * aw, width=rw, kind="id", out_dtype=F32, name="proj_rx")
    gate = _proj(xn, w_in_bf, kg, col0=3 * aw + rw, width=rw, kind="gelu", out_dtype=F32, name="proj_gate")
    sg = _proj(xn, w_in_bf, kg, col0=3 * aw + 2 * rw, width=2 * d, kind="sigmoid", out_dtype=F32,
               name="proj_sg")

    ya = _attn(q, k, v, _attn_distance_table(rel_bias[0]))
    yb = _rglru(rx, gate, conv_w[0], conv_b[0].reshape(1, rw),
                rg_a_w[0].astype(BF16), rg_a_b[0].reshape(1, rw),
                rg_x_w[0].astype(BF16), rg_x_b[0].reshape(1, rw), rg_lambda[0].reshape(1, rw))

    merged = _upmerge(ya, yb, sg, w_attn_up[0].astype(BF16), w_rnn_up[0].astype(BF16))
    h1, hn_packed, logits = _mixer_out(
        merged, x2, w_out[0].astype(BF16), g1, norm2_g[0].reshape(1, d), sc2, sh2,
        w_router[0], b_router[0].reshape(1, N_EXPERTS))

    idx_w, topw_w, pos_w, cnt = _route(logits)
    idx = idx_w[:, :TOP_K]
    pos = pos_w[:, :TOP_K]
    counts = cnt[0].astype(I32)
    padded = (counts + ROW_BLOCK - 1) // ROW_BLOCK * ROW_BLOCK
    pend = jnp.cumsum(padded)
    pstart = pend - padded
    experts = jnp.arange(N_EXPERTS, dtype=I32)
    dest = (pos + jnp.sum(jnp.where(idx[:, :, None] == experts, pstart, 0), axis=-1)).reshape(-1)
    n_blocks = (t * TOP_K) // ROW_BLOCK + N_EXPERTS
    n_rows = n_blocks * ROW_BLOCK
    blk_row0 = jnp.arange(n_blocks, dtype=I32) * ROW_BLOCK
    blk_e = jnp.minimum(jnp.sum((pend[None, :] <= blk_row0[:, None]).astype(I32), axis=1), N_EXPERTS - 1)
    n_active = (pend[-1:] // ROW_BLOCK).astype(I32)
    run_end = (pend // ROW_BLOCK).astype(I32)
    n_slots = t * TOP_K
    slot_ids = (jnp.arange(t, dtype=I32)[:, None] + jnp.arange(TOP_K, dtype=I32)[None, :] * t).reshape(-1)
    spare = n_slots + jnp.arange(ROW_BLOCK, dtype=I32)
    row_slot = jnp.tile(spare, n_blocks).at[dest].set(slot_ids)
    src_tok = jnp.where(row_slot < n_slots, row_slot % t, 0).reshape(n_blocks, 1, ROW_BLOCK)
    slot_prev = jnp.concatenate([spare, row_slot]).reshape(n_blocks + 1, 1, ROW_BLOCK)

    act = _moe_gu(hn_packed, src_tok, w_gu[0], b_gu[0], blk_e, n_active, run_end)
    y_slots = _moe_down(act, slot_prev, w_down[0], b_down[0], blk_e, n_active, run_end, n_slots)
    out = _combine(topw_w, h1, g2, y_slots)
    return out.reshape(b, s, d)
```

```python
import functools
import math

import jax
import jax.numpy as jnp
from jax import lax
from jax.experimental import pallas as pl
from jax.experimental.pallas import tpu as pltpu

F32 = jnp.float32
BF16 = jnp.bfloat16
U32 = jnp.uint32
I32 = jnp.int32

CHUNK = 64
LEFT_CHUNKS = 8
ATTN_HEADS = 8
HEAD_DIM = 128
REL_MAX = 256
REL_MIN = -(CHUNK - 1)
RNN_BLOCKS = 8
CONV_W = 4
RG_C = 8.0
N_EXPERTS = 32
TOP_K = 4
SWIGLU_LIMIT = 7.0
SWIGLU_ALPHA = 1.702
EPS = 1e-6
NEG_BIG = -1e30

LANES = 128
SUBLANES = 8
VMEM_LIMIT = 56 * 1024 * 1024

ROW_BLOCK = 512
MOE_BUFS = 3
PACK_TILES = 8
PROJ_TM = 1024
PROJ_TN = 1024
ATTN_QB = 4 * CHUNK
ATTN_KB = ATTN_QB + LEFT_CHUNKS * CHUNK
ATTN_TAB = ATTN_KB + ATTN_QB
CHUNK_SHIFT = CHUNK.bit_length() - 1
assert 1 << CHUNK_SHIFT == CHUNK


def _cparams(*sem):
    return pltpu.CompilerParams(dimension_semantics=sem, vmem_limit_bytes=VMEM_LIMIT)


def _ada_kernel(c_ref, w_ref, b_ref, o_ref):
    c = c_ref[...]
    ca = c * jax.nn.sigmoid(c)
    o_ref[...] = jnp.dot(ca, w_ref[...], preferred_element_type=F32,
                         precision=lax.Precision.HIGHEST) + b_ref[...]


def _ada(c, ada_w, ada_b):
    d, n = ada_w.shape
    tn = 1024
    c8 = jnp.broadcast_to(c, (SUBLANES, d))
    out = pl.pallas_call(
        _ada_kernel,
        grid=(n // tn,),
        in_specs=[pl.BlockSpec((SUBLANES, d), lambda j: (0, 0)),
                  pl.BlockSpec((d, tn), lambda j: (0, j)),
                  pl.BlockSpec((1, tn), lambda j: (0, j))],
        out_specs=pl.BlockSpec((SUBLANES, tn), lambda j: (0, j)),
        out_shape=jax.ShapeDtypeStruct((SUBLANES, n), F32),
        compiler_params=_cparams("arbitrary"),
        name="ada",
    )(c8, ada_w, ada_b.reshape(1, n))
    return out[0:1]


def _gelu_tanh(x):
    return 0.5 * x * (1.0 + jnp.tanh(math.sqrt(2.0 / math.pi) * (x + 0.044715 * (x * x * x))))


def _head_rms(a, g):
    outs = []
    for hh in range(a.shape[1] // HEAD_DIM):
        s = a[:, hh * HEAD_DIM:(hh + 1) * HEAD_DIM]
        ms = jnp.mean(s * s, axis=-1, keepdims=True)
        outs.append(s * lax.rsqrt(ms + EPS) * g)
    return jnp.concatenate(outs, axis=1)


def _norm_mod_kernel(x_ref, g_ref, sc_ref, sh_ref, o_ref):
    x = x_ref[...]
    ms = jnp.mean(x * x, axis=-1, keepdims=True)
    xn = x * lax.rsqrt(ms + EPS) * g_ref[...]
    o_ref[...] = (xn * (1.0 + sc_ref[...]) + sh_ref[...]).astype(o_ref.dtype)


def _norm_mod(x2, g, sc, sh):
    t, d = x2.shape
    tm = 512
    vec = lambda: pl.BlockSpec((1, d), lambda i: (0, 0))
    return pl.pallas_call(
        _norm_mod_kernel,
        grid=(t // tm,),
        in_specs=[pl.BlockSpec((tm, d), lambda i: (i, 0)), vec(), vec(), vec()],
        out_specs=pl.BlockSpec((tm, d), lambda i: (i, 0)),
        out_shape=jax.ShapeDtypeStruct((t, d), BF16),
        compiler_params=_cparams("arbitrary"),
        name="norm1",
    )(x2, g, sc, sh)


def _proj_kernel(x_ref, w_ref, hg_ref, o_ref, *, kind):
    acc = jnp.dot(x_ref[...], w_ref[...], preferred_element_type=F32)
    if kind == "q":
        r = _head_rms(acc, hg_ref[...]) * (1.0 / math.sqrt(HEAD_DIM))
    elif kind == "k":
        r = _head_rms(acc, hg_ref[...])
    elif kind == "gelu":
        r = _gelu_tanh(acc)
    elif kind == "sigmoid":
        r = jax.nn.sigmoid(acc)
    else:
        r = acc
    o_ref[...] = r.astype(o_ref.dtype)


def _proj(xn, w_bf, head_gain, *, col0, width, kind, out_dtype, name):
    t, d = xn.shape
    tm, tn = PROJ_TM, PROJ_TN
    j0 = col0 // tn
    return pl.pallas_call(
        functools.partial(_proj_kernel, kind=kind),
        grid=(t // tm, width // tn),
        in_specs=[pl.BlockSpec((tm, d), lambda i, j: (i, 0)),
                  pl.BlockSpec((d, tn), lambda i, j: (0, j + j0)),
                  pl.BlockSpec((1, HEAD_DIM), lambda i, j: (0, 0))],
        out_specs=pl.BlockSpec((tm, tn), lambda i, j: (i, j)),
        out_shape=jax.ShapeDtypeStruct((t, width), out_dtype),
        compiler_params=_cparams("arbitrary", "arbitrary"),
        name=name,
    )(xn, w_bf, head_gain)


def _attn_kernel(q_ref, k0_ref, k1_ref, k2_ref, v0_ref, v1_ref, v2_ref, ftab_ref, o_ref, bias_ref):
    i = pl.program_id(0)
    col = lax.broadcasted_iota(I32, (ATTN_QB, ATTN_KB), 1)

    @pl.when(i == 0)
    def _():
        row = lax.broadcasted_iota(I32, (ATTN_QB, ATTN_KB), 0)
        cq = row >> CHUNK_SHIFT
        ck = col >> CHUNK_SHIFT
        visible = (ck >= cq) & (ck <= cq + LEFT_CHUNKS)
        for h in range(ATTN_HEADS):
            g = jnp.broadcast_to(ftab_ref[h:h + 1, :], (ATTN_QB, ATTN_TAB))
            y = pltpu.roll(g, 0, 1, stride=1, stride_axis=0)
            bias_ref[h] = jnp.where(visible, y[:, :ATTN_KB], NEG_BIG)

    in_seq = col >= (LEFT_CHUNKS * CHUNK - i * ATTN_QB)
    for h in range(ATTN_HEADS):
        sl = slice(h * HEAD_DIM, (h + 1) * HEAD_DIM)
        q = q_ref[:, sl]
        k = jnp.concatenate([k0_ref[:, sl], k1_ref[:, sl], k2_ref[:, sl]], axis=0)
        v = jnp.concatenate([v0_ref[:, sl], v1_ref[:, sl], v2_ref[:, sl]], axis=0)
        s = lax.dot_general(q, k, (((1,), (1,)), ((), ())), preferred_element_type=F32)
        s = jnp.where(in_seq, s + bias_ref[h], NEG_BIG)
        m = jnp.max(s, axis=-1, keepdims=True)
        p = jnp.exp(s - m)
        l = jnp.sum(p, axis=-1, keepdims=True)
        o = jnp.dot(p.astype(BF16), v, preferred_element_type=F32) / l
        o_ref[:, sl] = o.astype(o_ref.dtype)


def _attn(q, k, v, ftab):
    t = q.shape[0]
    w = ATTN_HEADS * HEAD_DIM
    qb = ATTN_QB
    blk = lambda f: pl.BlockSpec((qb, w), f)
    return pl.pallas_call(
        _attn_kernel,
        grid=(t // qb,),
        in_specs=[blk(lambda i: (i, 0)),
                  blk(lambda i: (jnp.maximum(i - 2, 0), 0)), blk(lambda i: (jnp.maximum(i - 1, 0), 0)),
                  blk(lambda i: (i, 0)),
                  blk(lambda i: (jnp.maximum(i - 2, 0), 0)), blk(lambda i: (jnp.maximum(i - 1, 0), 0)),
                  blk(lambda i: (i, 0)),
                  pl.BlockSpec((ATTN_HEADS, ATTN_TAB), lambda i: (0, 0))],
        out_specs=blk(lambda i: (i, 0)),
        out_shape=jax.ShapeDtypeStruct((t, w), BF16),
        scratch_shapes=[pltpu.VMEM((ATTN_HEADS, qb, ATTN_KB), F32)],
        compiler_params=_cparams("arbitrary"),
        name="attn",
    )(q, k, k, k, v, v, v, ftab)


def _attn_distance_table(rel_bias):
    pad = LEFT_CHUNKS * CHUNK
    n_far = pad - REL_MAX + 1
    n_mid = REL_MAX - REL_MIN
    n_ahead = ATTN_KB - n_far - n_mid
    far = rel_bias[:, -1:]
    return jnp.concatenate([jnp.broadcast_to(far, (ATTN_HEADS, n_far)),
                            rel_bias[:, :n_mid][:, ::-1],
                            jnp.broadcast_to(rel_bias[:, :1], (ATTN_HEADS, n_ahead)),
                            jnp.broadcast_to(far, (ATTN_HEADS, ATTN_TAB - ATTN_KB))], axis=1)


def _rglru_kernel(rx_ref, gate_ref, cw_ref, cb_ref, wa_ref, ba_ref, wx_ref, bx_ref, lam_ref,
                  o_ref, xbuf, a_s, u_s, hc, *, tt):
    i = pl.program_id(0)
    width = rx_ref.shape[1]
    bw = width // RNN_BLOCKS

    @pl.when(i == 0)
    def _():
        xbuf[0:SUBLANES, :] = jnp.zeros((SUBLANES, width), F32)
        hc[...] = jnp.zeros(hc.shape, F32)

    xbuf[SUBLANES:SUBLANES + tt, :] = rx_ref[...]
    xc = cb_ref[...] + cw_ref[CONV_W - 1:CONV_W, :] * xbuf[SUBLANES:SUBLANES + tt, :]
    for d in range(1, CONV_W):
        xc = xc + cw_ref[CONV_W - 1 - d:CONV_W - d, :] * xbuf[SUBLANES - d:SUBLANES - d + tt, :]
    tail = xbuf[tt:tt + SUBLANES, :]
    xbuf[0:SUBLANES, :] = tail

    z = -lam_ref[...]
    softplus = jnp.maximum(z, 0.0) + jnp.log1p(jnp.exp(-jnp.abs(z)))
    xcb = xc.astype(BF16)
    for b in range(RNN_BLOCKS):
        sl = slice(b * bw, (b + 1) * bw)
        xb_ = xcb[:, sl]
        r = jax.nn.sigmoid(jnp.dot(xb_, wa_ref[b], preferred_element_type=F32) + ba_ref[:, sl])
        ig = jax.nn.sigmoid(jnp.dot(xb_, wx_ref[b], preferred_element_type=F32) + bx_ref[:, sl])
        log_a = (-RG_C) * r * softplus[:, sl]
        a = jnp.exp(log_a)
        th = jnp.tanh(log_a)
        one_m_a2 = (-2.0) * th / (1.0 - th)
        a_s[:, sl] = a
        u_s[:, sl] = jnp.sqrt(one_m_a2) * (ig * xc[:, sl])

    row = lax.broadcasted_iota(I32, (SUBLANES, width), 0)

    def body(g, h):
        r0 = pl.multiple_of(g * SUBLANES, SUBLANES)
        a = a_s[pl.ds(r0, SUBLANES), :]
        u = u_s[pl.ds(r0, SUBLANES), :]
        for s in (1, 2, 4):
            a_sh = pltpu.roll(a, s, 0)
            u_sh = pltpu.roll(u, s, 0)
            keep = row >= s
            u = jnp.where(keep, a * u_sh + u, u)
            a = jnp.where(keep, a * a_sh, a)
        hh = a * h + u
        u_s[pl.ds(r0, SUBLANES), :] = hh
        return hh[SUBLANES - 1:SUBLANES, :]

    h_last = lax.fori_loop(0, tt // SUBLANES, body, hc[0:1, :])
    hc[0:1, :] = h_last
    o_ref[...] = (u_s[...] * gate_ref[...]).astype(o_ref.dtype)


def _rglru(rx, gate, conv_w, conv_b, wa_bf, ba, wx_bf, bx, lam):
    t = rx.shape[0]
    width = conv_w.shape[1]
    bw = width // RNN_BLOCKS
    tt = 512
    vec = lambda: pl.BlockSpec((1, width), lambda i: (0, 0))
    wblk = lambda: pl.BlockSpec((RNN_BLOCKS, bw, bw), lambda i: (0, 0, 0))
    return pl.pallas_call(
        functools.partial(_rglru_kernel, tt=tt),
        grid=(t // tt,),
        in_specs=[pl.BlockSpec((tt, width), lambda i: (i, 0)),
                  pl.BlockSpec((tt, width), lambda i: (i, 0)),
                  pl.BlockSpec((CONV_W, width), lambda i: (0, 0)), vec(),
                  wblk(), vec(), wblk(), vec(), vec()],
        out_specs=pl.BlockSpec((tt, width), lambda i: (i, 0)),
        out_shape=jax.ShapeDtypeStruct((t, width), BF16),
        scratch_shapes=[pltpu.VMEM((tt + SUBLANES, width), F32),
                        pltpu.VMEM((tt, width), F32),
                        pltpu.VMEM((tt, width), F32),
                        pltpu.VMEM((SUBLANES, width), F32)],
        compiler_params=_cparams("arbitrary"),
        name="rglru",
    )(rx, gate, conv_w, conv_b, wa_bf, ba, wx_bf, bx, lam)


def _upmerge_kernel(ya_ref, yb_ref, sga_ref, sgb_ref, wa_ref, wr_ref, o_ref):
    up_a = jnp.dot(ya_ref[...], wa_ref[...], preferred_element_type=F32)
    up_b = jnp.dot(yb_ref[...], wr_ref[...], preferred_element_type=F32)
    o_ref[...] = (sga_ref[...] * up_a + sgb_ref[...] * up_b).astype(o_ref.dtype)


def _upmerge(ya, yb, sg, wa_bf, wr_bf):
    t, aw = ya.shape
    rw = yb.shape[1]
    d = wa_bf.shape[1]
    tm, tn = PROJ_TM, PROJ_TN
    nj = d // tn
    return pl.pallas_call(
        _upmerge_kernel,
        grid=(t // tm, nj),
        in_specs=[pl.BlockSpec((tm, aw), lambda i, j: (i, 0)),
                  pl.BlockSpec((tm, rw), lambda i, j: (i, 0)),
                  pl.BlockSpec((tm, tn), lambda i, j: (i, j)),
                  pl.BlockSpec((tm, tn), lambda i, j: (i, j + nj)),
                  pl.BlockSpec((aw, tn), lambda i, j: (0, j)),
                  pl.BlockSpec((rw, tn), lambda i, j: (0, j))],
        out_specs=pl.BlockSpec((tm, tn), lambda i, j: (i, j)),
        out_shape=jax.ShapeDtypeStruct((t, d), BF16),
        compiler_params=_cparams("arbitrary", "arbitrary"),
        name="upmerge",
    )(ya, yb, sg, sg, wa_bf, wr_bf)


def _mixer_out_kernel(mg_ref, x_ref, wo_ref, g1_ref, n2_ref, sc2_ref, sh2_ref, wrh_ref, brt_ref,
                      h1_ref, hnp_ref, lg_ref):
    proj = jnp.dot(mg_ref[...], wo_ref[...], preferred_element_type=F32)
    h1 = x_ref[...] + g1_ref[...] * proj
    h1_ref[...] = h1
    ms = jnp.mean(h1 * h1, axis=-1, keepdims=True)
    hn = h1 * lax.rsqrt(ms + EPS) * n2_ref[...]
    hn = hn * (1.0 + sc2_ref[...]) + sh2_ref[...]
    hb = hn.astype(BF16)
    hb32 = hb.astype(F32)
    hl = (hn - hb32).astype(BF16)
    ne = lg_ref.shape[1]
    hh = jnp.dot(hb, wrh_ref[...], preferred_element_type=F32)
    lh = jnp.dot(hl, wrh_ref[:, :ne], preferred_element_type=F32)
    lg_ref[...] = hh[:, :ne] + hh[:, ne:] + lh + brt_ref[...]
    half = hn.shape[1] // 2
    lo = pltpu.bitcast(hb32[:, :half], U32) >> 16
    hi = pltpu.bitcast(hb32[:, half:], U32) & jnp.uint32(0xFFFF0000)
    packed = hi | lo
    tm = packed.shape[0]
    for s in range(PACK_TILES):
        hnp_ref[pl.ds(s, tm, stride=PACK_TILES), :] = packed[:, s * LANES:(s + 1) * LANES]


def _mixer_out(merged, x2, wo_bf, g1, n2g, sc2, sh2, w_router, b_router):
    t, d = x2.shape
    ne = w_router.shape[1]
    tm = 512
    w_hi = w_router.astype(BF16)
    w_lo = (w_router - w_hi.astype(F32)).astype(BF16)
    w_split = jnp.concatenate([w_hi, w_lo], axis=1)
    const = lambda shape: pl.BlockSpec(shape, lambda i: (0,) * len(shape), pipeline_mode=pl.Buffered(1))
    return pl.pallas_call(
        _mixer_out_kernel,
        grid=(t // tm,),
        in_specs=[pl.BlockSpec((tm, d), lambda i: (i, 0)),
                  pl.BlockSpec((tm, d), lambda i: (i, 0)),
                  const((d, d)),
                  const((1, d)), const((1, d)), const((1, d)), const((1, d)),
                  const((d, 2 * ne)), const((1, ne))],
        out_specs=[pl.BlockSpec((tm, d), lambda i: (i, 0)),
                   pl.BlockSpec((tm * PACK_TILES, LANES), lambda i: (i, 0)),
                   pl.BlockSpec((tm, ne), lambda i: (i, 0))],
        out_shape=[jax.ShapeDtypeStruct((t, d), F32),
                   jax.ShapeDtypeStruct((t * PACK_TILES, LANES), U32),
                   jax.ShapeDtypeStruct((t, ne), F32)],
        compiler_params=_cparams("arbitrary"),
        name="mixer_out",
    )(merged, x2, wo_bf, g1, n2g, sc2, sh2, w_split, b_router)


def _route_kernel(lg_ref, idx_ref, w_ref, pos_ref, cnt_ref, carry, *, rt):
    i = pl.program_id(0)
    ne = lg_ref.shape[1]

    @pl.when(i == 0)
    def _():
        carry[...] = jnp.zeros(carry.shape, F32)

    l = lg_ref[...]
    lane = lax.broadcasted_iota(I32, (rt, ne), 1).astype(F32)
    vals, idxs, hots = [], [], []
    for _ in range(TOP_K):
        m = jnp.max(l, axis=-1, keepdims=True)
        ik = jnp.min(jnp.where(l == m, lane, float(ne)), axis=-1, keepdims=True)
        hot = lane == ik
        vals.append(m)
        idxs.append(ik)
        hots.append(hot)
        l = jnp.where(hot, -jnp.inf, l)
    es = [jnp.exp(v - vals[0]) for v in vals]
    tot = es[0] + es[1] + es[2] + es[3]
    sel = jnp.zeros((rt, ne), F32)
    for hot in hots:
        sel = sel + hot.astype(F32)
    rr = lax.broadcasted_iota(I32, (rt, rt), 0)
    cc = lax.broadcasted_iota(I32, (rt, rt), 1)
    tri = (cc < rr).astype(BF16)
    before = jnp.dot(tri, sel.astype(BF16), preferred_element_type=F32) + carry[0:1, :]
    out_lane = lax.broadcasted_iota(I32, (rt, LANES), 1)
    idx_o = jnp.zeros((rt, LANES), F32)
    w_o = jnp.zeros((rt, LANES), F32)
    pos_o = jnp.zeros((rt, LANES), F32)
    for k in range(TOP_K):
        pk = jnp.sum(jnp.where(hots[k], before, 0.0), axis=-1, keepdims=True)
        idx_o = jnp.where(out_lane == k, idxs[k], idx_o)
        w_o = jnp.where(out_lane == k, es[k] / tot, w_o)
        pos_o = jnp.where(out_lane == k, pk, pos_o)
    idx_ref[...] = idx_o.astype(I32)
    w_ref[...] = w_o
    pos_ref[...] = pos_o.astype(I32)
    carry[0:1, :] = carry[0:1, :] + jnp.sum(sel, axis=0, keepdims=True)
    cnt_ref[...] = carry[...]


def _route(logits):
    t, ne = logits.shape
    rt = 512
    wide = lambda dt: jax.ShapeDtypeStruct((t, LANES), dt)
    return pl.pallas_call(
        functools.partial(_route_kernel, rt=rt),
        grid=(t // rt,),
        in_specs=[pl.BlockSpec((rt, ne), lambda i: (i, 0))],
        out_specs=[pl.BlockSpec((rt, LANES), lambda i: (i, 0)),
                   pl.BlockSpec((rt, LANES), lambda i: (i, 0)),
                   pl.BlockSpec((rt, LANES), lambda i: (i, 0)),
                   pl.BlockSpec((SUBLANES, ne), lambda i: (0, 0))],
        out_shape=[wide(I32), wide(F32), wide(I32), jax.ShapeDtypeStruct((SUBLANES, ne), F32)],
        scratch_shapes=[pltpu.VMEM((SUBLANES, ne), F32)],
        compiler_params=_cparams("arbitrary"),
        name="route",
    )(logits)


def _unpack_rows(x_ref):
    los, his = [], []
    for s in range(PACK_TILES):
        p = x_ref[pl.ds(s, ROW_BLOCK, stride=PACK_TILES), :]
        los.append(pltpu.bitcast(p << 16, F32).astype(BF16))
        his.append(pltpu.bitcast(p & jnp.uint32(0xFFFF0000), F32).astype(BF16))
    return jnp.concatenate(los + his, axis=1)


def _first_of_expert(be_ref, m, na):
    mm = jnp.minimum(m, na - 1)
    prev = be_ref[jnp.maximum(mm - 1, 0)]
    return (mm == 0) | (be_ref[mm] != prev)


def _weight_stream(be_ref, na_ref, end_ref, n_tiles, copies, convert):
    n = pl.program_id(0)
    m = pl.program_id(1)
    na = na_ref[0]
    active = m < na

    @pl.when((n == 0) & (m == 0))
    def _():
        for cp in copies(be_ref[0], 0):
            cp.start()

    @pl.when(active & _first_of_expert(be_ref, m, na))
    def _():
        e = be_ref[m]
        for cp in copies(e, n):
            cp.wait()
        convert()
        run_end = end_ref[e]
        wrap = run_end >= na
        n2 = jnp.where(wrap, n + 1, n)
        m2 = jnp.where(wrap, 0, run_end)

        @pl.when(n2 < n_tiles)
        def _():
            for cp in copies(be_ref[m2], n2):
                cp.start()

    return active


def _moe_gu_kernel(be_ref, na_ref, end_ref, tok0_ref, tok1_ref, tok2_ref, hn_hbm, w_hbm, bg_ref, bu_ref,
                   o_ref, stg_g, stg_u, wgb, wub, xbuf0, xbuf1, xbuf2, sem, gsem, *, tn, nf):
    n = pl.program_id(0)
    m = pl.program_id(1)
    na = na_ref[0]
    slot = (n * na + m) % MOE_BUFS
    xbufs = (xbuf0, xbuf1, xbuf2)

    def gather(tok_ref, s):
        for r in range(ROW_BLOCK):
            src = pl.multiple_of(tok_ref[0, 0, r], PACK_TILES)
            pltpu.make_async_copy(hn_hbm.at[pl.ds(src, PACK_TILES)],
                                  xbufs[s].at[pl.ds(r * PACK_TILES, PACK_TILES)], gsem.at[s]).start()

    def gather_wait(s):
        pltpu.make_async_copy(hn_hbm.at[pl.ds(0, ROW_BLOCK * PACK_TILES)], xbufs[s], gsem.at[s]).wait()

    @pl.when((n == 0) & (m == 0))
    def _():
        gather(tok0_ref, 0)
        gather(tok1_ref, 1)

    def copies(e, n):
        c0 = pl.multiple_of(n * tn, tn)
        c1 = pl.multiple_of((n + nf) * tn, tn)
        return [pltpu.make_async_copy(w_hbm.at[e, :, pl.ds(c0, tn)], stg_g, sem.at[0]),
                pltpu.make_async_copy(w_hbm.at[e, :, pl.ds(c1, tn)], stg_u, sem.at[1])]

    def convert():
        wgb[...] = stg_g[...].astype(BF16)
        wub[...] = stg_u[...].astype(BF16)

    active = _weight_stream(be_ref, na_ref, end_ref, nf, copies, convert)

    def step(s):
        gather_wait(s)
        gather(tok2_ref, (s + 2) % MOE_BUFS)
        x = _unpack_rows(xbufs[s])
        g = jnp.dot(x, wgb[...], preferred_element_type=F32) + bg_ref[...]
        u = jnp.dot(x, wub[...], preferred_element_type=F32) + bu_ref[...]
        g = jnp.minimum(g, SWIGLU_LIMIT)
        u = jnp.clip(u, -SWIGLU_LIMIT, SWIGLU_LIMIT)
        glu = g * jax.nn.sigmoid(SWIGLU_ALPHA * g)
        o_ref[...] = ((u + 1.0) * glu).astype(o_ref.dtype)

    last = (n == nf - 1) & (m == na - 1)
    for s in range(MOE_BUFS):
        @pl.when(active & (slot == s))
        def _(s=s):
            step(s)

        @pl.when(active & last & (slot == s))
        def _(s=s):
            gather_wait((s + 1) % MOE_BUFS)
            gather_wait((s + 2) % MOE_BUFS)

    @pl.when(jnp.logical_not(active))
    def _():
        o_ref[...] = jnp.zeros(o_ref.shape, o_ref.dtype)


def _moe_gu(hn_packed, src_tok, w_gu, b_gu, blk_e, n_active, run_end):
    nb = src_tok.shape[0]
    ne, d, f2 = w_gu.shape
    f = f2 // 2
    tn = 1024
    nf = f // tn
    mc = lambda m, na: jnp.minimum(m, na[0] - 1)
    tok_blk = lambda f_: pl.BlockSpec((1, 1, ROW_BLOCK), f_, memory_space=pltpu.SMEM)
    grid_spec = pltpu.PrefetchScalarGridSpec(
        num_scalar_prefetch=3,
        grid=(nf, nb),
        in_specs=[tok_blk(lambda n, m, be, na, en: (0, 0, 0)),
                  tok_blk(lambda n, m, be, na, en: (1 % na[0], 0, 0)),
                  tok_blk(lambda n, m, be, na, en: ((m + 2) % na[0], 0, 0)),
                  pl.BlockSpec(memory_space=pl.ANY),
                  pl.BlockSpec(memory_space=pl.ANY),
                  pl.BlockSpec((None, 1, tn), lambda n, m, be, na, en: (be[mc(m, na)], 0, n)),
                  pl.BlockSpec((None, 1, tn), lambda n, m, be, na, en: (be[mc(m, na)], 0, n + nf))],
        out_specs=pl.BlockSpec((ROW_BLOCK, tn), lambda n, m, be, na, en: (m, n)),
        scratch_shapes=[pltpu.VMEM((d, tn), F32), pltpu.VMEM((d, tn), F32),
                        pltpu.VMEM((d, tn), BF16), pltpu.VMEM((d, tn), BF16),
                        *[pltpu.VMEM((ROW_BLOCK * PACK_TILES, LANES), U32)] * MOE_BUFS,
                        pltpu.SemaphoreType.DMA((2,)), pltpu.SemaphoreType.DMA((MOE_BUFS,))],
    )
    b3 = b_gu.reshape(ne, 1, f2)
    return pl.pallas_call(
        functools.partial(_moe_gu_kernel, tn=tn, nf=nf),
        grid_spec=grid_spec,
        out_shape=jax.ShapeDtypeStruct((nb * ROW_BLOCK, f), BF16),
        compiler_params=_cparams("arbitrary", "arbitrary"),
        name="moe_gu",
    )(blk_e, n_active, run_end, src_tok, src_tok, src_tok, hn_packed, w_gu, b3, b3)


def _moe_down_kernel(be_ref, na_ref, end_ref, slot_ref, a_ref, w_hbm, b_ref, y_hbm,
                     stg, wb, ybuf0, ybuf1, ybuf2, sem, ssem, *, tn):
    m = pl.program_id(1)
    na = na_ref[0]
    p = m % MOE_BUFS
    ybufs = (ybuf0, ybuf1, ybuf2)

    def scatter(s):
        for r in range(ROW_BLOCK):
            pltpu.make_async_copy(ybufs[s].at[pl.ds(r, 1)], y_hbm.at[pl.ds(slot_ref[0, 0, r], 1)],
                                  ssem.at[s]).start()

    def scatter_wait(s):
        pltpu.make_async_copy(ybufs[s], y_hbm.at[pl.ds(0, ROW_BLOCK)], ssem.at[s]).wait()

    @pl.when(m == 0)
    def _():
        ybuf2[...] = jnp.zeros(ybuf2.shape, ybuf2.dtype)

    for s in range(MOE_BUFS):
        @pl.when((m >= 2) & (m <= na + 2) & (p == s))
        def _(s=s):
            scatter_wait(s)

    def copies(e, n):
        return [pltpu.make_async_copy(w_hbm.at[e], stg, sem.at[0])]

    def convert():
        wb[...] = stg[...].astype(BF16)

    active = _weight_stream(be_ref, na_ref, end_ref, 1, copies, convert)

    for s in range(MOE_BUFS):
        @pl.when(active & (p == s))
        def _(s=s):
            scatter((s + MOE_BUFS - 1) % MOE_BUFS)
            ybufs[s][...] = jnp.dot(a_ref[...], wb[...], preferred_element_type=F32) + b_ref[...]

        @pl.when((m == na) & (p == s))
        def _(s=s):
            scatter((s + MOE_BUFS - 1) % MOE_BUFS)


def _moe_down(act, slot_prev, w_down, b_down, blk_e, n_active, run_end, n_slots):
    n_rows, f = act.shape
    ne, _, d = w_down.shape
    nb = n_rows // ROW_BLOCK
    mc = lambda m, na: jnp.minimum(m, na[0] - 1)
    grid_spec = pltpu.PrefetchScalarGridSpec(
        num_scalar_prefetch=3,
        grid=(1, nb + MOE_BUFS),
        in_specs=[pl.BlockSpec((1, 1, ROW_BLOCK), lambda n, m, be, na, en: (jnp.minimum(m, nb), 0, 0),
                               memory_space=pltpu.SMEM),
                  pl.BlockSpec((ROW_BLOCK, f), lambda n, m, be, na, en: (mc(m, na), 0)),
                  pl.BlockSpec(memory_space=pl.ANY),
                  pl.BlockSpec((None, 1, d), lambda n, m, be, na, en: (be[mc(m, na)], 0, 0))],
        out_specs=pl.BlockSpec(memory_space=pl.ANY),
        scratch_shapes=[pltpu.VMEM((f, d), F32), pltpu.VMEM((f, d), BF16),
                        *[pltpu.VMEM((ROW_BLOCK, d), F32)] * MOE_BUFS,
                        pltpu.SemaphoreType.DMA((1,)), pltpu.SemaphoreType.DMA((MOE_BUFS,))],
    )
    return pl.pallas_call(
        functools.partial(_moe_down_kernel, tn=d),
        grid_spec=grid_spec,
        out_shape=jax.ShapeDtypeStruct((n_slots + ROW_BLOCK, d), F32),
        compiler_params=_cparams("arbitrary", "arbitrary"),
        name="moe_down",
    )(blk_e, n_active, run_end, slot_prev, act, w_down, b_down.reshape(ne, 1, d))


def _combine_kernel(w_ref, h1_ref, g2_ref, *refs):
    y_refs, o_ref = refs[:TOP_K], refs[TOP_K]
    w = w_ref[...]
    acc = y_refs[0][...] * w[:, 0:1]
    for k in range(1, TOP_K):
        acc = acc + y_refs[k][...] * w[:, k:k + 1]
    o_ref[...] = h1_ref[...] + g2_ref[...] * acc


def _combine(top_w, h1, g2, y_slots):
    t, d = h1.shape
    tm = 256
    nt = t // tm
    return pl.pallas_call(
        _combine_kernel,
        grid=(nt,),
        in_specs=[pl.BlockSpec((tm, LANES), lambda i: (i, 0)),
                  pl.BlockSpec((tm, d), lambda i: (i, 0)),
                  pl.BlockSpec((1, d), lambda i: (0, 0))]
                 + [pl.BlockSpec((tm, d), lambda i, k=k: (k * nt + i, 0)) for k in range(TOP_K)],
        out_specs=pl.BlockSpec((tm, d), lambda i: (i, 0)),
        out_shape=jax.ShapeDtypeStruct((t, d), F32),
        compiler_params=_cparams("arbitrary"),
        name="combine",
    )(top_w, h1, g2, *([y_slots] * TOP_K))


def kernel(x, c, ada_w, ada_b, norm1_g, w_in, q_norm_g, k_norm_g, rel_bias, conv_w, conv_b,
           rg_a_w, rg_a_b, rg_x_w, rg_x_b, rg_lambda, w_attn_up, w_rnn_up, w_out, norm2_g,
           w_router, b_router, w_gu, b_gu, w_down, b_down):
    b, s, d = x.shape
    assert b == 1 and ada_w.shape[0] == 1, "single batch row, single layer"
    t = s
    x2 = x.reshape(t, d)
    aw = ATTN_HEADS * HEAD_DIM
    rw = conv_w.shape[2]

    mod = _ada(c, ada_w[0], ada_b[0])
    sh1, sc1, g1, sh2, sc2, g2 = [mod[:, k * d:(k + 1) * d] for k in range(6)]

    w_in_bf = w_in[0].astype(BF16)
    qg = q_norm_g[0].reshape(1, HEAD_DIM)
    kg = k_norm_g[0].reshape(1, HEAD_DIM)
    n1g = norm1_g[0].reshape(1, d)
    xn = _norm_mod(x2, n1g, sc1, sh1)
    q = _proj(xn, w_in_bf, qg, col0=0, width=aw, kind="q", out_dtype=BF16, name="proj_q")
    k = _proj(xn, w_in_bf, kg, col0=aw, width=aw, kind="k", out_dtype=BF16, name="proj_k")
    v = _proj(xn, w_in_bf, kg, col0=2 * aw, width=aw, kind="id", out_dtype=BF16, name="proj_v")
    rx = _proj(xn, w_in_bf, kg, col0=3 * aw, width=rw, kind="id", out_dtype=F32, name="proj_rx")
    gate = _proj(xn, w_in_bf, kg, col0=3 * aw + rw, width=rw, kind="gelu", out_dtype=F32, name="proj_gate")
    sg = _proj(xn, w_in_bf, kg, col0=3 * aw + 2 * rw, width=2 * d, kind="sigmoid", out_dtype=F32,
               name="proj_sg")

    ya = _attn(q, k, v, _attn_distance_table(rel_bias[0]))
    yb = _rglru(rx, gate, conv_w[0], conv_b[0].reshape(1, rw),
                rg_a_w[0].astype(BF16), rg_a_b[0].reshape(1, rw),
                rg_x_w[0].astype(BF16), rg_x_b[0].reshape(1, rw), rg_lambda[0].reshape(1, rw))

    merged = _upmerge(ya, yb, sg, w_attn_up[0].astype(BF16), w_rnn_up[0].astype(BF16))
    h1, hn_packed, logits = _mixer_out(
        merged, x2, w_out[0].astype(BF16), g1, norm2_g[0].reshape(1, d), sc2, sh2,
        w_router[0], b_router[0].reshape(1, N_EXPERTS))

    idx_w, topw_w, pos_w, cnt = _route(logits)
    idx = idx_w[:, :TOP_K]
    pos = pos_w[:, :TOP_K]
    counts = cnt[0].astype(I32)
    padded = (counts + ROW_BLOCK - 1) // ROW_BLOCK * ROW_BLOCK
    pend = jnp.cumsum(padded)
    pstart = pend - padded
    experts = jnp.arange(N_EXPERTS, dtype=I32)
    dest = (pos + jnp.sum(jnp.where(idx[:, :, None] == experts, pstart, 0), axis=-1)).reshape(-1)
    n_blocks = (t * TOP_K) // ROW_BLOCK + N_EXPERTS
    n_rows = n_blocks * ROW_BLOCK
    blk_row0 = jnp.arange(n_blocks, dtype=I32) * ROW_BLOCK
    blk_e = jnp.minimum(jnp.sum((pend[None, :] <= blk_row0[:, None]).astype(I32), axis=1), N_EXPERTS - 1)
    n_active = (pend[-1:] // ROW_BLOCK).astype(I32)
    run_end = (pend // ROW_BLOCK).astype(I32)
    n_slots = t * TOP_K
    slot_ids = (jnp.arange(t, dtype=I32)[:, None] + jnp.arange(TOP_K, dtype=I32)[None, :] * t).reshape(-1)
    spare = n_slots + jnp.arange(ROW_BLOCK, dtype=I32)
    row_slot = jnp.tile(spare, n_blocks).at[dest].set(slot_ids)
    src_tok = (jnp.where(row_slot < n_slots, row_slot % t, 0) * PACK_TILES).reshape(n_blocks, 1, ROW_BLOCK)
    slot_prev = jnp.concatenate([spare, row_slot]).reshape(n_blocks + 1, 1, ROW_BLOCK)

    act = _moe_gu(hn_packed, src_tok, w_gu[0], b_gu[0], blk_e, n_active, run_end)
    y_slots = _moe_down(act, slot_prev, w_down[0], b_down[0], blk_e, n_active, run_end, n_slots)
    out = _combine(topw_w, h1, g2, y_slots)
    return out.reshape(b, s, d)
```

```python
import functools
import math

import jax
import jax.numpy as jnp
from jax import lax
from jax.experimental import pallas as pl
from jax.experimental.pallas import tpu as pltpu

F32 = jnp.float32
BF16 = jnp.bfloat16
U32 = jnp.uint32
I32 = jnp.int32

CHUNK = 64
LEFT_CHUNKS = 8
ATTN_HEADS = 8
HEAD_DIM = 128
REL_MAX = 256
REL_MIN = -(CHUNK - 1)
RNN_BLOCKS = 8
CONV_W = 4
RG_C = 8.0
N_EXPERTS = 32
TOP_K = 4
SWIGLU_LIMIT = 7.0
SWIGLU_ALPHA = 1.702
EPS = 1e-6
NEG_BIG = -1e30

LANES = 128
SUBLANES = 8
VMEM_LIMIT = 56 * 1024 * 1024

ROW_BLOCK = 512
MOE_BUFS = 3
PACK_TILES = 8
PROJ_TM = 1024
PROJ_TN = 1024
ATTN_QB = 4 * CHUNK
ATTN_KB = ATTN_QB + LEFT_CHUNKS * CHUNK
ATTN_TAB = ATTN_KB + ATTN_QB
CHUNK_SHIFT = CHUNK.bit_length() - 1
assert 1 << CHUNK_SHIFT == CHUNK


def _cparams(*sem):
    return pltpu.CompilerParams(dimension_semantics=sem, vmem_limit_bytes=VMEM_LIMIT)


def _ada_kernel(c_ref, w_ref, b_ref, o_ref):
    c = c_ref[...]
    ca = c * jax.nn.sigmoid(c)
    o_ref[...] = jnp.dot(ca, w_ref[...], preferred_element_type=F32,
                         precision=lax.Precision.HIGHEST) + b_ref[...]


def _ada(c, ada_w, ada_b):
    d, n = ada_w.shape
    tn = 1024
    c8 = jnp.broadcast_to(c, (SUBLANES, d))
    out = pl.pallas_call(
        _ada_kernel,
        grid=(n // tn,),
        in_specs=[pl.BlockSpec((SUBLANES, d), lambda j: (0, 0)),
                  pl.BlockSpec((d, tn), lambda j: (0, j)),
                  pl.BlockSpec((1, tn), lambda j: (0, j))],
        out_specs=pl.BlockSpec((SUBLANES, tn), lambda j: (0, j)),
        out_shape=jax.ShapeDtypeStruct((SUBLANES, n), F32),
        compiler_params=_cparams("arbitrary"),
        name="ada",
    )(c8, ada_w, ada_b.reshape(1, n))
    return out[0:1]


def _sigmoid_tanh(x):
    return 0.5 * jnp.tanh(0.5 * x) + 0.5


def _gelu_tanh(x):
    return 0.5 * x * (1.0 + jnp.tanh(math.sqrt(2.0 / math.pi) * (x + 0.044715 * (x * x * x))))


def _head_rms(a, g):
    outs = []
    for hh in range(a.shape[1] // HEAD_DIM):
        s = a[:, hh * HEAD_DIM:(hh + 1) * HEAD_DIM]
        ms = jnp.mean(s * s, axis=-1, keepdims=True)
        outs.append(s * lax.rsqrt(ms + EPS) * g)
    return jnp.concatenate(outs, axis=1)


def _norm_mod_kernel(x_ref, g_ref, sc_ref, sh_ref, o_ref):
    x = x_ref[...]
    ms = jnp.mean(x * x, axis=-1, keepdims=True)
    xn = x * lax.rsqrt(ms + EPS) * g_ref[...]
    o_ref[...] = (xn * (1.0 + sc_ref[...]) + sh_ref[...]).astype(o_ref.dtype)


def _norm_mod(x2, g, sc, sh):
    t, d = x2.shape
    tm = 512
    vec = lambda: pl.BlockSpec((1, d), lambda i: (0, 0))
    return pl.pallas_call(
        _norm_mod_kernel,
        grid=(t // tm,),
        in_specs=[pl.BlockSpec((tm, d), lambda i: (i, 0)), vec(), vec(), vec()],
        out_specs=pl.BlockSpec((tm, d), lambda i: (i, 0)),
        out_shape=jax.ShapeDtypeStruct((t, d), BF16),
        compiler_params=_cparams("arbitrary"),
        name="norm1",
    )(x2, g, sc, sh)


def _proj_kernel(x_ref, w_ref, hg_ref, o_ref, wb_ref, *, kind):
    @pl.when(pl.program_id(1) == 0)
    def _():
        wb_ref[...] = w_ref[...].astype(BF16)

    acc = jnp.dot(x_ref[...], wb_ref[...], preferred_element_type=F32)
    if kind == "q":
        r = _head_rms(acc, hg_ref[...]) * (1.0 / math.sqrt(HEAD_DIM))
    elif kind == "k":
        r = _head_rms(acc, hg_ref[...])
    elif kind == "gelu":
        r = _gelu_tanh(acc)
    elif kind == "sigmoid":
        r = _sigmoid_tanh(acc)
    else:
        r = acc
    o_ref[...] = r.astype(o_ref.dtype)


def _proj(xn, w, head_gain, *, col0, width, kind, out_dtype, name):
    t, d = xn.shape
    tm, tn = PROJ_TM, PROJ_TN
    j0 = col0 // tn
    return pl.pallas_call(
        functools.partial(_proj_kernel, kind=kind),
        grid=(width // tn, t // tm),
        in_specs=[pl.BlockSpec((tm, d), lambda j, i: (i, 0)),
                  pl.BlockSpec((d, tn), lambda j, i: (0, j + j0)),
                  pl.BlockSpec((1, HEAD_DIM), lambda j, i: (0, 0))],
        out_specs=pl.BlockSpec((tm, tn), lambda j, i: (i, j)),
        out_shape=jax.ShapeDtypeStruct((t, width), out_dtype),
        scratch_shapes=[pltpu.VMEM((d, tn), BF16)],
        compiler_params=_cparams("arbitrary", "arbitrary"),
        name=name,
    )(xn, w, head_gain)


def _attn_kernel(q_ref, k0_ref, k1_ref, k2_ref, v0_ref, v1_ref, v2_ref, ftab_ref, o_ref, bias_ref):
    i = pl.program_id(0)
    col = lax.broadcasted_iota(I32, (ATTN_QB, ATTN_KB), 1)

    @pl.when(i == 0)
    def _():
        row = lax.broadcasted_iota(I32, (ATTN_QB, ATTN_KB), 0)
        cq = row >> CHUNK_SHIFT
        ck = col >> CHUNK_SHIFT
        visible = (ck >= cq) & (ck <= cq + LEFT_CHUNKS)
        for h in range(ATTN_HEADS):
            g = jnp.broadcast_to(ftab_ref[h:h + 1, :], (ATTN_QB, ATTN_TAB))
            y = pltpu.roll(g, 0, 1, stride=1, stride_axis=0)
            bias_ref[h] = jnp.where(visible, y[:, :ATTN_KB], NEG_BIG)

    in_seq = col >= (LEFT_CHUNKS * CHUNK - i * ATTN_QB)
    for h in range(ATTN_HEADS):
        sl = slice(h * HEAD_DIM, (h + 1) * HEAD_DIM)
        q = q_ref[:, sl]
        k = jnp.concatenate([k0_ref[:, sl], k1_ref[:, sl], k2_ref[:, sl]], axis=0)
        v = jnp.concatenate([v0_ref[:, sl], v1_ref[:, sl], v2_ref[:, sl]], axis=0)
        s = lax.dot_general(q, k, (((1,), (1,)), ((), ())), preferred_element_type=F32)
        s = jnp.where(in_seq, s + bias_ref[h], NEG_BIG)
        m = jnp.max(s, axis=-1, keepdims=True)
        p = jnp.exp(s - m)
        l = jnp.sum(p, axis=-1, keepdims=True)
        o = jnp.dot(p.astype(BF16), v, preferred_element_type=F32) / l
        o_ref[:, sl] = o.astype(o_ref.dtype)


def _attn(q, k, v, ftab):
    t = q.shape[0]
    w = ATTN_HEADS * HEAD_DIM
    qb = ATTN_QB
    blk = lambda f: pl.BlockSpec((qb, w), f)
    return pl.pallas_call(
        _attn_kernel,
        grid=(t // qb,),
        in_specs=[blk(lambda i: (i, 0)),
                  blk(lambda i: (jnp.maximum(i - 2, 0), 0)), blk(lambda i: (jnp.maximum(i - 1, 0), 0)),
                  blk(lambda i: (i, 0)),
                  blk(lambda i: (jnp.maximum(i - 2, 0), 0)), blk(lambda i: (jnp.maximum(i - 1, 0), 0)),
                  blk(lambda i: (i, 0)),
                  pl.BlockSpec((ATTN_HEADS, ATTN_TAB), lambda i: (0, 0))],
        out_specs=blk(lambda i: (i, 0)),
        out_shape=jax.ShapeDtypeStruct((t, w), BF16),
        scratch_shapes=[pltpu.VMEM((ATTN_HEADS, qb, ATTN_KB), F32)],
        compiler_params=_cparams("arbitrary"),
        name="attn",
    )(q, k, k, k, v, v, v, ftab)


def _attn_distance_table(rel_bias):
    pad = LEFT_CHUNKS * CHUNK
    n_far = pad - REL_MAX + 1
    n_mid = REL_MAX - REL_MIN
    n_ahead = ATTN_KB - n_far - n_mid
    far = rel_bias[:, -1:]
    return jnp.concatenate([jnp.broadcast_to(far, (ATTN_HEADS, n_far)),
                            rel_bias[:, :n_mid][:, ::-1],
                            jnp.broadcast_to(rel_bias[:, :1], (ATTN_HEADS, n_ahead)),
                            jnp.broadcast_to(far, (ATTN_HEADS, ATTN_TAB - ATTN_KB))], axis=1)


def _rglru_kernel(rx_ref, gate_ref, cw_ref, cb_ref, wa_ref, ba_ref, wx_ref, bx_ref, lam_ref,
                  o_ref, xbuf, a_s, u_s, hc, *, tt):
    i = pl.program_id(0)
    width = rx_ref.shape[1]
    bw = width // RNN_BLOCKS

    @pl.when(i == 0)
    def _():
        xbuf[0:SUBLANES, :] = jnp.zeros((SUBLANES, width), F32)
        hc[...] = jnp.zeros(hc.shape, F32)

    xbuf[SUBLANES:SUBLANES + tt, :] = rx_ref[...]
    xc = cb_ref[...] + cw_ref[CONV_W - 1:CONV_W, :] * xbuf[SUBLANES:SUBLANES + tt, :]
    for d in range(1, CONV_W):
        xc = xc + cw_ref[CONV_W - 1 - d:CONV_W - d, :] * xbuf[SUBLANES - d:SUBLANES - d + tt, :]
    tail = xbuf[tt:tt + SUBLANES, :]
    xbuf[0:SUBLANES, :] = tail

    z = -lam_ref[...]
    softplus = jnp.maximum(z, 0.0) + jnp.log1p(jnp.exp(-jnp.abs(z)))
    xcb = xc.astype(BF16)
    for b in range(RNN_BLOCKS):
        sl = slice(b * bw, (b + 1) * bw)
        xb_ = xcb[:, sl]
        r = _sigmoid_tanh(jnp.dot(xb_, wa_ref[b], preferred_element_type=F32) + ba_ref[:, sl])
        ig = _sigmoid_tanh(jnp.dot(xb_, wx_ref[b], preferred_element_type=F32) + bx_ref[:, sl])
        log_a = (-RG_C) * r * softplus[:, sl]
        a = jnp.exp(log_a)
        th = jnp.tanh(log_a)
        one_m_a2 = (-2.0) * th / (1.0 - th)
        a_s[:, sl] = a
        u_s[:, sl] = jnp.sqrt(one_m_a2) * (ig * xc[:, sl])

    row = lax.broadcasted_iota(I32, (SUBLANES, width), 0)

    def body(g, h):
        r0 = pl.multiple_of(g * SUBLANES, SUBLANES)
        a = a_s[pl.ds(r0, SUBLANES), :]
        u = u_s[pl.ds(r0, SUBLANES), :]
        for s in (1, 2, 4):
            a_sh = pltpu.roll(a, s, 0)
            u_sh = pltpu.roll(u, s, 0)
            keep = row >= s
            u = jnp.where(keep, a * u_sh + u, u)
            a = jnp.where(keep, a * a_sh, a)
        hh = a * h + u
        u_s[pl.ds(r0, SUBLANES), :] = hh
        return hh[SUBLANES - 1:SUBLANES, :]

    h_last = lax.fori_loop(0, tt // SUBLANES, body, hc[0:1, :])
    hc[0:1, :] = h_last
    o_ref[...] = (u_s[...] * gate_ref[...]).astype(o_ref.dtype)


def _rglru(rx, gate, conv_w, conv_b, wa_bf, ba, wx_bf, bx, lam):
    t = rx.shape[0]
    width = conv_w.shape[1]
    bw = width // RNN_BLOCKS
    tt = 512
    vec = lambda: pl.BlockSpec((1, width), lambda i: (0, 0))
    wblk = lambda: pl.BlockSpec((RNN_BLOCKS, bw, bw), lambda i: (0, 0, 0))
    return pl.pallas_call(
        functools.partial(_rglru_kernel, tt=tt),
        grid=(t // tt,),
        in_specs=[pl.BlockSpec((tt, width), lambda i: (i, 0)),
                  pl.BlockSpec((tt, width), lambda i: (i, 0)),
                  pl.BlockSpec((CONV_W, width), lambda i: (0, 0)), vec(),
                  wblk(), vec(), wblk(), vec(), vec()],
        out_specs=pl.BlockSpec((tt, width), lambda i: (i, 0)),
        out_shape=jax.ShapeDtypeStruct((t, width), BF16),
        scratch_shapes=[pltpu.VMEM((tt + SUBLANES, width), F32),
                        pltpu.VMEM((tt, width), F32),
                        pltpu.VMEM((tt, width), F32),
                        pltpu.VMEM((SUBLANES, width), F32)],
        compiler_params=_cparams("arbitrary"),
        name="rglru",
    )(rx, gate, conv_w, conv_b, wa_bf, ba, wx_bf, bx, lam)


def _upmerge_kernel(ya_ref, yb_ref, sga_ref, sgb_ref, wa_ref, wr_ref, o_ref, wab_ref, wrb_ref):
    @pl.when(pl.program_id(1) == 0)
    def _():
        wab_ref[...] = wa_ref[...].astype(BF16)
        wrb_ref[...] = wr_ref[...].astype(BF16)

    up_a = jnp.dot(ya_ref[...], wab_ref[...], preferred_element_type=F32)
    up_b = jnp.dot(yb_ref[...], wrb_ref[...], preferred_element_type=F32)
    o_ref[...] = (sga_ref[...] * up_a + sgb_ref[...] * up_b).astype(o_ref.dtype)


def _upmerge(ya, yb, sg, wa, wr):
    t, aw = ya.shape
    rw = yb.shape[1]
    d = wa.shape[1]
    tm, tn = PROJ_TM, PROJ_TN
    nj = d // tn
    return pl.pallas_call(
        _upmerge_kernel,
        grid=(nj, t // tm),
        in_specs=[pl.BlockSpec((tm, aw), lambda j, i: (i, 0)),
                  pl.BlockSpec((tm, rw), lambda j, i: (i, 0)),
                  pl.BlockSpec((tm, tn), lambda j, i: (i, j)),
                  pl.BlockSpec((tm, tn), lambda j, i: (i, j + nj)),
                  pl.BlockSpec((aw, tn), lambda j, i: (0, j)),
                  pl.BlockSpec((rw, tn), lambda j, i: (0, j))],
        out_specs=pl.BlockSpec((tm, tn), lambda j, i: (i, j)),
        out_shape=jax.ShapeDtypeStruct((t, d), BF16),
        scratch_shapes=[pltpu.VMEM((aw, tn), BF16), pltpu.VMEM((rw, tn), BF16)],
        compiler_params=_cparams("arbitrary", "arbitrary"),
        name="upmerge",
    )(ya, yb, sg, sg, wa, wr)


def _mixer_out_kernel(mg_ref, x_ref, wo_ref, g1_ref, n2_ref, sc2_ref, sh2_ref, wrh_ref, brt_ref,
                      h1_ref, hnp_ref, lg_ref):
    proj = jnp.dot(mg_ref[...], wo_ref[...], preferred_element_type=F32)
    h1 = x_ref[...] + g1_ref[...] * proj
    h1_ref[...] = h1
    ms = jnp.mean(h1 * h1, axis=-1, keepdims=True)
    hn = h1 * lax.rsqrt(ms + EPS) * n2_ref[...]
    hn = hn * (1.0 + sc2_ref[...]) + sh2_ref[...]
    hb = hn.astype(BF16)
    hb32 = hb.astype(F32)
    hl = (hn - hb32).astype(BF16)
    ne = lg_ref.shape[1]
    hh = jnp.dot(hb, wrh_ref[...], preferred_element_type=F32)
    lh = jnp.dot(hl, wrh_ref[:, :ne], preferred_element_type=F32)
    lg_ref[...] = hh[:, :ne] + hh[:, ne:] + lh + brt_ref[...]
    half = hn.shape[1] // 2
    lo = pltpu.bitcast(hb32[:, :half], U32) >> 16
    hi = pltpu.bitcast(hb32[:, half:], U32) & jnp.uint32(0xFFFF0000)
    packed = hi | lo
    tm = packed.shape[0]
    for s in range(PACK_TILES):
        hnp_ref[pl.ds(s, tm, stride=PACK_TILES), :] = packed[:, s * LANES:(s + 1) * LANES]


def _mixer_out(merged, x2, wo_bf, g1, n2g, sc2, sh2, w_router, b_router):
    t, d = x2.shape
    ne = w_router.shape[1]
    tm = 512
    w_hi = w_router.astype(BF16)
    w_lo = (w_router - w_hi.astype(F32)).astype(BF16)
    w_split = jnp.concatenate([w_hi, w_lo], axis=1)
    const = lambda shape: pl.BlockSpec(shape, lambda i: (0,) * len(shape), pipeline_mode=pl.Buffered(1))
    return pl.pallas_call(
        _mixer_out_kernel,
        grid=(t // tm,),
        in_specs=[pl.BlockSpec((tm, d), lambda i: (i, 0)),
                  pl.BlockSpec((tm, d), lambda i: (i, 0)),
                  const((d, d)),
                  const((1, d)), const((1, d)), const((1, d)), const((1, d)),
                  const((d, 2 * ne)), const((1, ne))],
        out_specs=[pl.BlockSpec((tm, d), lambda i: (i, 0)),
                   pl.BlockSpec((tm * PACK_TILES, LANES), lambda i: (i, 0)),
                   pl.BlockSpec((tm, ne), lambda i: (i, 0))],
        out_shape=[jax.ShapeDtypeStruct((t, d), F32),
                   jax.ShapeDtypeStruct((t * PACK_TILES, LANES), U32),
                   jax.ShapeDtypeStruct((t, ne), F32)],
        compiler_params=_cparams("arbitrary"),
        name="mixer_out",
    )(merged, x2, wo_bf, g1, n2g, sc2, sh2, w_split, b_router)


def _route_kernel(lg_ref, idx_ref, w_ref, pos_ref, cnt_ref, carry, *, rt):
    i = pl.program_id(0)
    ne = lg_ref.shape[1]

    @pl.when(i == 0)
    def _():
        carry[...] = jnp.zeros(carry.shape, F32)

    l = lg_ref[...]
    lane = lax.broadcasted_iota(I32, (rt, ne), 1).astype(F32)
    vals, idxs, hots = [], [], []
    for _ in range(TOP_K):
        m = jnp.max(l, axis=-1, keepdims=True)
        ik = jnp.min(jnp.where(l == m, lane, float(ne)), axis=-1, keepdims=True)
        hot = lane == ik
        vals.append(m)
        idxs.append(ik)
        hots.append(hot)
        l = jnp.where(hot, -jnp.inf, l)
    es = [jnp.exp(v - vals[0]) for v in vals]
    tot = es[0] + es[1] + es[2] + es[3]
    sel = jnp.zeros((rt, ne), F32)
    for hot in hots:
        sel = sel + hot.astype(F32)
    rr = lax.broadcasted_iota(I32, (rt, rt), 0)
    cc = lax.broadcasted_iota(I32, (rt, rt), 1)
    tri = (cc < rr).astype(BF16)
    before = jnp.dot(tri, sel.astype(BF16), preferred_element_type=F32) + carry[0:1, :]
    out_lane = lax.broadcasted_iota(I32, (rt, LANES), 1)
    idx_o = jnp.zeros((rt, LANES), F32)
    w_o = jnp.zeros((rt, LANES), F32)
    pos_o = jnp.zeros((rt, LANES), F32)
    for k in range(TOP_K):
        pk = jnp.sum(jnp.where(hots[k], before, 0.0), axis=-1, keepdims=True)
        idx_o = jnp.where(out_lane == k, idxs[k], idx_o)
        w_o = jnp.where(out_lane == k, es[k] / tot, w_o)
        pos_o = jnp.where(out_lane == k, pk, pos_o)
    idx_ref[...] = idx_o.astype(I32)
    w_ref[...] = w_o
    pos_ref[...] = pos_o.astype(I32)
    carry[0:1, :] = carry[0:1, :] + jnp.sum(sel, axis=0, keepdims=True)
    cnt_ref[...] = carry[...]


def _route(logits):
    t, ne = logits.shape
    rt = 512
    wide = lambda dt: jax.ShapeDtypeStruct((t, LANES), dt)
    return pl.pallas_call(
        functools.partial(_route_kernel, rt=rt),
        grid=(t // rt,),
        in_specs=[pl.BlockSpec((rt, ne), lambda i: (i, 0))],
        out_specs=[pl.BlockSpec((rt, LANES), lambda i: (i, 0)),
                   pl.BlockSpec((rt, LANES), lambda i: (i, 0)),
                   pl.BlockSpec((rt, LANES), lambda i: (i, 0)),
                   pl.BlockSpec((SUBLANES, ne), lambda i: (0, 0))],
        out_shape=[wide(I32), wide(F32), wide(I32), jax.ShapeDtypeStruct((SUBLANES, ne), F32)],
        scratch_shapes=[pltpu.VMEM((SUBLANES, ne), F32)],
        compiler_params=_cparams("arbitrary"),
        name="route",
    )(logits)


def _unpack_rows(x_ref):
    los, his = [], []
    for s in range(PACK_TILES):
        p = x_ref[pl.ds(s, ROW_BLOCK, stride=PACK_TILES), :]
        los.append(pltpu.bitcast(p << 16, F32).astype(BF16))
        his.append(pltpu.bitcast(p & jnp.uint32(0xFFFF0000), F32).astype(BF16))
    return jnp.concatenate(los + his, axis=1)


def _first_of_expert(be_ref, m, na):
    mm = jnp.minimum(m, na - 1)
    prev = be_ref[jnp.maximum(mm - 1, 0)]
    return (mm == 0) | (be_ref[mm] != prev)


def _weight_stream(be_ref, na_ref, end_ref, n_tiles, copies, convert):
    n = pl.program_id(0)
    m = pl.program_id(1)
    na = na_ref[0]
    active = m < na

    @pl.when((n == 0) & (m == 0))
    def _():
        for cp in copies(be_ref[0], 0):
            cp.start(priority=1)

    @pl.when(active & _first_of_expert(be_ref, m, na))
    def _():
        e = be_ref[m]
        for cp in copies(e, n):
            cp.wait()
        convert()
        run_end = end_ref[e]
        wrap = run_end >= na
        n2 = jnp.where(wrap, n + 1, n)
        m2 = jnp.where(wrap, 0, run_end)

        @pl.when(n2 < n_tiles)
        def _():
            for cp in copies(be_ref[m2], n2):
                cp.start(priority=1)

    return active


def _moe_gu_kernel(be_ref, na_ref, end_ref, tok0_ref, tok1_ref, tok2_ref, hn_hbm, w_hbm, bg_ref, bu_ref,
                   o_ref, stg_g, stg_u, wgb, wub, xbuf0, xbuf1, xbuf2, sem, gsem, *, tn, nf):
    n = pl.program_id(0)
    m = pl.program_id(1)
    na = na_ref[0]
    slot = (n * na + m) % MOE_BUFS
    xbufs = (xbuf0, xbuf1, xbuf2)

    def gather(tok_ref, s):
        for r in range(ROW_BLOCK):
            src = pl.multiple_of(tok_ref[0, 0, r], PACK_TILES)
            pltpu.make_async_copy(hn_hbm.at[pl.ds(src, PACK_TILES)],
                                  xbufs[s].at[pl.ds(r * PACK_TILES, PACK_TILES)], gsem.at[s]).start()

    def gather_wait(s):
        pltpu.make_async_copy(hn_hbm.at[pl.ds(0, ROW_BLOCK * PACK_TILES)], xbufs[s], gsem.at[s]).wait()

    @pl.when((n == 0) & (m == 0))
    def _():
        gather(tok0_ref, 0)
        gather(tok1_ref, 1)

    def copies(e, n):
        c0 = pl.multiple_of(n * tn, tn)
        c1 = pl.multiple_of((n + nf) * tn, tn)
        return [pltpu.make_async_copy(w_hbm.at[e, :, pl.ds(c0, tn)], stg_g, sem.at[0]),
                pltpu.make_async_copy(w_hbm.at[e, :, pl.ds(c1, tn)], stg_u, sem.at[1])]

    def convert():
        wgb[...] = stg_g[...].astype(BF16)
        wub[...] = stg_u[...].astype(BF16)

    active = _weight_stream(be_ref, na_ref, end_ref, nf, copies, convert)

    def step(s):
        gather_wait(s)
        gather(tok2_ref, (s + 2) % MOE_BUFS)
        x = _unpack_rows(xbufs[s])
        g = jnp.dot(x, wgb[...], preferred_element_type=F32) + bg_ref[...]
        u = jnp.dot(x, wub[...], preferred_element_type=F32) + bu_ref[...]
        g = jnp.minimum(g, SWIGLU_LIMIT)
        u = jnp.clip(u, -SWIGLU_LIMIT, SWIGLU_LIMIT)
        glu = g * _sigmoid_tanh(SWIGLU_ALPHA * g)
        o_ref[...] = ((u + 1.0) * glu).astype(o_ref.dtype)

    last = (n == nf - 1) & (m == na - 1)
    for s in range(MOE_BUFS):
        @pl.when(active & (slot == s))
        def _(s=s):
            step(s)

        @pl.when(active & last & (slot == s))
        def _(s=s):
            gather_wait((s + 1) % MOE_BUFS)
            gather_wait((s + 2) % MOE_BUFS)

    @pl.when(jnp.logical_not(active))
    def _():
        o_ref[...] = jnp.zeros(o_ref.shape, o_ref.dtype)


def _moe_gu(hn_packed, src_tok, w_gu, b_gu, blk_e, n_active, run_end):
    nb = src_tok.shape[0]
    ne, d, f2 = w_gu.shape
    f = f2 // 2
    tn = 1024
    nf = f // tn
    mc = lambda m, na: jnp.minimum(m, na[0] - 1)
    tok_blk = lambda f_: pl.BlockSpec((1, 1, ROW_BLOCK), f_, memory_space=pltpu.SMEM)
    grid_spec = pltpu.PrefetchScalarGridSpec(
        num_scalar_prefetch=3,
        grid=(nf, nb),
        in_specs=[tok_blk(lambda n, m, be, na, en: (0, 0, 0)),
                  tok_blk(lambda n, m, be, na, en: (1 % na[0], 0, 0)),
                  tok_blk(lambda n, m, be, na, en: ((m + 2) % na[0], 0, 0)),
                  pl.BlockSpec(memory_space=pl.ANY),
                  pl.BlockSpec(memory_space=pl.ANY),
                  pl.BlockSpec((None, 1, tn), lambda n, m, be, na, en: (be[mc(m, na)], 0, n)),
                  pl.BlockSpec((None, 1, tn), lambda n, m, be, na, en: (be[mc(m, na)], 0, n + nf))],
        out_specs=pl.BlockSpec((ROW_BLOCK, tn), lambda n, m, be, na, en: (m, n)),
        scratch_shapes=[pltpu.VMEM((d, tn), F32), pltpu.VMEM((d, tn), F32),
                        pltpu.VMEM((d, tn), BF16), pltpu.VMEM((d, tn), BF16),
                        *[pltpu.VMEM((ROW_BLOCK * PACK_TILES, LANES), U32)] * MOE_BUFS,
                        pltpu.SemaphoreType.DMA((2,)), pltpu.SemaphoreType.DMA((MOE_BUFS,))],
    )
    b3 = b_gu.reshape(ne, 1, f2)
    return pl.pallas_call(
        functools.partial(_moe_gu_kernel, tn=tn, nf=nf),
        grid_spec=grid_spec,
        out_shape=jax.ShapeDtypeStruct((nb * ROW_BLOCK, f), BF16),
        compiler_params=_cparams("arbitrary", "arbitrary"),
        name="moe_gu",
    )(blk_e, n_active, run_end, src_tok, src_tok, src_tok, hn_packed, w_gu, b3, b3)


def _moe_down_kernel(be_ref, na_ref, end_ref, slot_ref, a_ref, w_hbm, b_ref, y_hbm,
                     stg, wb, ybuf0, ybuf1, ybuf2, sem, ssem, *, tn):
    m = pl.program_id(1)
    na = na_ref[0]
    p = m % MOE_BUFS
    ybufs = (ybuf0, ybuf1, ybuf2)

    def scatter(s):
        for r in range(ROW_BLOCK):
            pltpu.make_async_copy(ybufs[s].at[pl.ds(r, 1)], y_hbm.at[pl.ds(slot_ref[0, 0, r], 1)],
                                  ssem.at[s]).start()

    def scatter_wait(s):
        pltpu.make_async_copy(ybufs[s], y_hbm.at[pl.ds(0, ROW_BLOCK)], ssem.at[s]).wait()

    @pl.when(m == 0)
    def _():
        ybuf2[...] = jnp.zeros(ybuf2.shape, ybuf2.dtype)

    for s in range(MOE_BUFS):
        @pl.when((m >= 2) & (m <= na + 2) & (p == s))
        def _(s=s):
            scatter_wait(s)

    def copies(e, n):
        return [pltpu.make_async_copy(w_hbm.at[e], stg, sem.at[0])]

    def convert():
        wb[...] = stg[...].astype(BF16)

    active = _weight_stream(be_ref, na_ref, end_ref, 1, copies, convert)

    for s in range(MOE_BUFS):
        @pl.when(active & (p == s))
        def _(s=s):
            scatter((s + MOE_BUFS - 1) % MOE_BUFS)
            ybufs[s][...] = jnp.dot(a_ref[...], wb[...], preferred_element_type=F32) + b_ref[...]

        @pl.when((m == na) & (p == s))
        def _(s=s):
            scatter((s + MOE_BUFS - 1) % MOE_BUFS)


def _moe_down(act, slot_prev, w_down, b_down, blk_e, n_active, run_end, n_slots):
    n_rows, f = act.shape
    ne, _, d = w_down.shape
    nb = n_rows // ROW_BLOCK
    mc = lambda m, na: jnp.minimum(m, na[0] - 1)
    grid_spec = pltpu.PrefetchScalarGridSpec(
        num_scalar_prefetch=3,
        grid=(1, nb + MOE_BUFS),
        in_specs=[pl.BlockSpec((1, 1, ROW_BLOCK), lambda n, m, be, na, en: (jnp.minimum(m, nb), 0, 0),
                               memory_space=pltpu.SMEM),
                  pl.BlockSpec((ROW_BLOCK, f), lambda n, m, be, na, en: (mc(m, na), 0)),
                  pl.BlockSpec(memory_space=pl.ANY),
                  pl.BlockSpec((None, 1, d), lambda n, m, be, na, en: (be[mc(m, na)], 0, 0))],
        out_specs=pl.BlockSpec(memory_space=pl.ANY),
        scratch_shapes=[pltpu.VMEM((f, d), F32), pltpu.VMEM((f, d), BF16),
                        *[pltpu.VMEM((ROW_BLOCK, d), F32)] * MOE_BUFS,
                        pltpu.SemaphoreType.DMA((1,)), pltpu.SemaphoreType.DMA((MOE_BUFS,))],
    )
    return pl.pallas_call(
        functools.partial(_moe_down_kernel, tn=d),
        grid_spec=grid_spec,
        out_shape=jax.ShapeDtypeStruct((n_slots + ROW_BLOCK, d), F32),
        compiler_params=_cparams("arbitrary", "arbitrary"),
        name="moe_down",
    )(blk_e, n_active, run_end, slot_prev, act, w_down, b_down.reshape(ne, 1, d))


def _combine_kernel(w_ref, h1_ref, g2_ref, *refs):
    y_refs, o_ref = refs[:TOP_K], refs[TOP_K]
    w = w_ref[...]
    acc = y_refs[0][...] * w[:, 0:1]
    for k in range(1, TOP_K):
        acc = acc + y_refs[k][...] * w[:, k:k + 1]
    o_ref[...] = h1_ref[...] + g2_ref[...] * acc


def _combine(top_w, h1, g2, y_slots):
    t, d = h1.shape
    tm = 256
    nt = t // tm
    return pl.pallas_call(
        _combine_kernel,
        grid=(nt,),
        in_specs=[pl.BlockSpec((tm, LANES), lambda i: (i, 0)),
                  pl.BlockSpec((tm, d), lambda i: (i, 0)),
                  pl.BlockSpec((1, d), lambda i: (0, 0))]
                 + [pl.BlockSpec((tm, d), lambda i, k=k: (k * nt + i, 0)) for k in range(TOP_K)],
        out_specs=pl.BlockSpec((tm, d), lambda i: (i, 0)),
        out_shape=jax.ShapeDtypeStruct((t, d), F32),
        compiler_params=_cparams("arbitrary"),
        name="combine",
    )(top_w, h1, g2, *([y_slots] * TOP_K))


def kernel(x, c, ada_w, ada_b, norm1_g, w_in, q_norm_g, k_norm_g, rel_bias, conv_w, conv_b,
           rg_a_w, rg_a_b, rg_x_w, rg_x_b, rg_lambda, w_attn_up, w_rnn_up, w_out, norm2_g,
           w_router, b_router, w_gu, b_gu, w_down, b_down):
    b, s, d = x.shape
    assert b == 1 and ada_w.shape[0] == 1, "single batch row, single layer"
    t = s
    x2 = x.reshape(t, d)
    aw = ATTN_HEADS * HEAD_DIM
    rw = conv_w.shape[2]

    mod = _ada(c, ada_w[0], ada_b[0])
    sh1, sc1, g1, sh2, sc2, g2 = [mod[:, k * d:(k + 1) * d] for k in range(6)]

    w_in0 = w_in[0]
    qg = q_norm_g[0].reshape(1, HEAD_DIM)
    kg = k_norm_g[0].reshape(1, HEAD_DIM)
    n1g = norm1_g[0].reshape(1, d)
    xn = _norm_mod(x2, n1g, sc1, sh1)
    q = _proj(xn, w_in0, qg, col0=0, width=aw, kind="q", out_dtype=BF16, name="proj_q")
    k = _proj(xn, w_in0, kg, col0=aw, width=aw, kind="k", out_dtype=BF16, name="proj_k")
    v = _proj(xn, w_in0, kg, col0=2 * aw, width=aw, kind="id", out_dtype=BF16, name="proj_v")
    rx = _proj(xn, w_in0, kg, col0=3 * aw, width=rw, kind="id", out_dtype=F32, name="proj_rx")
    gate = _proj(xn, w_in0, kg, col0=3 * aw + rw, width=rw, kind="gelu", out_dtype=F32, name="proj_gate")
    sg = _proj(xn, w_in0, kg, col0=3 * aw + 2 * rw, width=2 * d, kind="sigmoid", out_dtype=F32,
               name="proj_sg")

    ya = _attn(q, k, v, _attn_distance_table(rel_bias[0]))
    yb = _rglru(rx, gate, conv_w[0], conv_b[0].reshape(1, rw),
                rg_a_w[0].astype(BF16), rg_a_b[0].reshape(1, rw),
                rg_x_w[0].astype(BF16), rg_x_b[0].reshape(1, rw), rg_lambda[0].reshape(1, rw))

    merged = _upmerge(ya, yb, sg, w_attn_up[0], w_rnn_up[0])
    h1, hn_packed, logits = _mixer_out(
        merged, x2, w_out[0].astype(BF16), g1, norm2_g[0].reshape(1, d), sc2, sh2,
        w_router[0], b_router[0].reshape(1, N_EXPERTS))

    idx_w, topw_w, pos_w, cnt = _route(logits)
    idx = idx_w[:, :TOP_K]
    pos = pos_w[:, :TOP_K]
    counts = cnt[0].astype(I32)
    padded = (counts + ROW_BLOCK - 1) // ROW_BLOCK * ROW_BLOCK
    pend = jnp.cumsum(padded)
    pstart = pend - padded
    experts = jnp.arange(N_EXPERTS, dtype=I32)
    dest = (pos + jnp.sum(jnp.where(idx[:, :, None] == experts, pstart, 0), axis=-1)).reshape(-1)
    n_blocks = (t * TOP_K) // ROW_BLOCK + N_EXPERTS
    n_rows = n_blocks * ROW_BLOCK
    blk_row0 = jnp.arange(n_blocks, dtype=I32) * ROW_BLOCK
    blk_e = jnp.minimum(jnp.sum((pend[None, :] <= blk_row0[:, None]).astype(I32), axis=1), N_EXPERTS - 1)
    n_active = (pend[-1:] // ROW_BLOCK).astype(I32)
    run_end = (pend // ROW_BLOCK).astype(I32)
    n_slots = t * TOP_K
    slot_ids = (jnp.arange(t, dtype=I32)[:, None] + jnp.arange(TOP_K, dtype=I32)[None, :] * t).reshape(-1)
    spare = n_slots + jnp.arange(ROW_BLOCK, dtype=I32)
    row_slot = jnp.tile(spare, n_blocks).at[dest].set(slot_ids)
    src_tok = (jnp.where(row_slot < n_slots, row_slot % t, 0) * PACK_TILES).reshape(n_blocks, 1, ROW_BLOCK)
    slot_prev = jnp.concatenate([spare, row_slot]).reshape(n_blocks + 1, 1, ROW_BLOCK)

    act = _moe_gu(hn_packed, src_tok, w_gu[0], b_gu[0], blk_e, n_active, run_end)
    y_slots = _moe_down(act, slot_prev, w_down[0], b_down[0], blk_e, n_active, run_end, n_slots)
    out = _combine(topw_w, h1, g2, y_slots)
    return out.reshape(b, s, d)
```

```python
import functools
import math

import jax
import jax.numpy as jnp
from jax import lax
from jax.experimental import pallas as pl
from jax.experimental.pallas import tpu as pltpu

F32 = jnp.float32
BF16 = jnp.bfloat16
U32 = jnp.uint32
I32 = jnp.int32

CHUNK = 64
LEFT_CHUNKS = 8
ATTN_HEADS = 8
HEAD_DIM = 128
REL_MAX = 256
REL_MIN = -(CHUNK - 1)
RNN_BLOCKS = 8
CONV_W = 4
RG_C = 8.0
N_EXPERTS = 32
TOP_K = 4
SWIGLU_LIMIT = 7.0
SWIGLU_ALPHA = 1.702
EPS = 1e-6
NEG_BIG = -1e30

LANES = 128
SUBLANES = 8
VMEM_LIMIT = 56 * 1024 * 1024

ROW_BLOCK = 512
MOE_BUFS = 3
MOE_GU_TN = 1024
PACK_TILES = 8
PROJ_TM = 1024
PROJ_TN = 1024
ATTN_QB = 4 * CHUNK
ATTN_KB = ATTN_QB + LEFT_CHUNKS * CHUNK
ATTN_TAB = ATTN_KB + ATTN_QB
CHUNK_SHIFT = CHUNK.bit_length() - 1
assert 1 << CHUNK_SHIFT == CHUNK


def _cparams(*sem):
    return pltpu.CompilerParams(dimension_semantics=sem, vmem_limit_bytes=VMEM_LIMIT)


def _ada_kernel(c_ref, w_ref, b_ref, o_ref):
    c = c_ref[...]
    ca = c * jax.nn.sigmoid(c)
    o_ref[...] = jnp.dot(ca, w_ref[...], preferred_element_type=F32,
                         precision=lax.Precision.HIGHEST) + b_ref[...]


def _ada(c, ada_w, ada_b):
    d, n = ada_w.shape
    tn = 1024
    c8 = jnp.broadcast_to(c, (SUBLANES, d))
    out = pl.pallas_call(
        _ada_kernel,
        grid=(n // tn,),
        in_specs=[pl.BlockSpec((SUBLANES, d), lambda j: (0, 0)),
                  pl.BlockSpec((d, tn), lambda j: (0, j)),
                  pl.BlockSpec((1, tn), lambda j: (0, j))],
        out_specs=pl.BlockSpec((SUBLANES, tn), lambda j: (0, j)),
        out_shape=jax.ShapeDtypeStruct((SUBLANES, n), F32),
        compiler_params=_cparams("arbitrary"),
        name="ada",
    )(c8, ada_w, ada_b.reshape(1, n))
    return out[0:1]


def _sigmoid_tanh(x):
    return 0.5 * jnp.tanh(0.5 * x) + 0.5


def _gelu_tanh(x):
    return 0.5 * x * (1.0 + jnp.tanh(math.sqrt(2.0 / math.pi) * (x + 0.044715 * (x * x * x))))


def _head_rms(a, g):
    outs = []
    for hh in range(a.shape[1] // HEAD_DIM):
        s = a[:, hh * HEAD_DIM:(hh + 1) * HEAD_DIM]
        ms = jnp.mean(s * s, axis=-1, keepdims=True)
        outs.append(s * lax.rsqrt(ms + EPS) * g)
    return jnp.concatenate(outs, axis=1)


def _norm_mod_kernel(x_ref, g_ref, sc_ref, sh_ref, o_ref):
    x = x_ref[...]
    ms = jnp.mean(x * x, axis=-1, keepdims=True)
    xn = x * lax.rsqrt(ms + EPS) * g_ref[...]
    o_ref[...] = (xn * (1.0 + sc_ref[...]) + sh_ref[...]).astype(o_ref.dtype)


def _norm_mod(x2, g, sc, sh):
    t, d = x2.shape
    tm = 512
    vec = lambda: pl.BlockSpec((1, d), lambda i: (0, 0))
    return pl.pallas_call(
        _norm_mod_kernel,
        grid=(t // tm,),
        in_specs=[pl.BlockSpec((tm, d), lambda i: (i, 0)), vec(), vec(), vec()],
        out_specs=pl.BlockSpec((tm, d), lambda i: (i, 0)),
        out_shape=jax.ShapeDtypeStruct((t, d), BF16),
        compiler_params=_cparams("arbitrary"),
        name="norm1",
    )(x2, g, sc, sh)


def _proj_kernel(x_ref, w_ref, hg_ref, o_ref, wb_ref, *, kind):
    @pl.when(pl.program_id(1) == 0)
    def _():
        wb_ref[...] = w_ref[...].astype(BF16)

    acc = jnp.dot(x_ref[...], wb_ref[...], preferred_element_type=F32)
    if kind == "q":
        r = _head_rms(acc, hg_ref[...]) * (1.0 / math.sqrt(HEAD_DIM))
    elif kind == "k":
        r = _head_rms(acc, hg_ref[...])
    elif kind == "gelu":
        r = _gelu_tanh(acc)
    elif kind == "sigmoid":
        r = _sigmoid_tanh(acc)
    else:
        r = acc
    o_ref[...] = r.astype(o_ref.dtype)


def _proj(xn, w, head_gain, *, col0, width, kind, out_dtype, name):
    t, d = xn.shape
    tm, tn = PROJ_TM, PROJ_TN
    j0 = col0 // tn
    return pl.pallas_call(
        functools.partial(_proj_kernel, kind=kind),
        grid=(width // tn, t // tm),
        in_specs=[pl.BlockSpec((tm, d), lambda j, i: (i, 0)),
                  pl.BlockSpec((d, tn), lambda j, i: (0, j + j0)),
                  pl.BlockSpec((1, HEAD_DIM), lambda j, i: (0, 0))],
        out_specs=pl.BlockSpec((tm, tn), lambda j, i: (i, j)),
        out_shape=jax.ShapeDtypeStruct((t, width), out_dtype),
        scratch_shapes=[pltpu.VMEM((d, tn), BF16)],
        compiler_params=_cparams("arbitrary", "arbitrary"),
        name=name,
    )(xn, w, head_gain)


def _attn_kernel(q_ref, k0_ref, k1_ref, k2_ref, v0_ref, v1_ref, v2_ref, ftab_ref, o_ref, bias_ref):
    i = pl.program_id(0)
    col = lax.broadcasted_iota(I32, (ATTN_QB, ATTN_KB), 1)

    @pl.when(i == 0)
    def _():
        row = lax.broadcasted_iota(I32, (ATTN_QB, ATTN_KB), 0)
        cq = row >> CHUNK_SHIFT
        ck = col >> CHUNK_SHIFT
        visible = (ck >= cq) & (ck <= cq + LEFT_CHUNKS)
        for h in range(ATTN_HEADS):
            g = jnp.broadcast_to(ftab_ref[h:h + 1, :], (ATTN_QB, ATTN_TAB))
            y = pltpu.roll(g, 0, 1, stride=1, stride_axis=0)
            bias_ref[h] = jnp.where(visible, y[:, :ATTN_KB], NEG_BIG)

    in_seq = col >= (LEFT_CHUNKS * CHUNK - i * ATTN_QB)
    for h in range(ATTN_HEADS):
        sl = slice(h * HEAD_DIM, (h + 1) * HEAD_DIM)
        q = q_ref[:, sl]
        k = jnp.concatenate([k0_ref[:, sl], k1_ref[:, sl], k2_ref[:, sl]], axis=0)
        v = jnp.concatenate([v0_ref[:, sl], v1_ref[:, sl], v2_ref[:, sl]], axis=0)
        s = lax.dot_general(q, k, (((1,), (1,)), ((), ())), preferred_element_type=F32)
        s = jnp.where(in_seq, s + bias_ref[h], NEG_BIG)
        m = jnp.max(s, axis=-1, keepdims=True)
        p = jnp.exp(s - m)
        l = jnp.sum(p, axis=-1, keepdims=True)
        o = jnp.dot(p.astype(BF16), v, preferred_element_type=F32) / l
        o_ref[:, sl] = o.astype(o_ref.dtype)


def _attn(q, k, v, ftab):
    t = q.shape[0]
    w = ATTN_HEADS * HEAD_DIM
    qb = ATTN_QB
    blk = lambda f: pl.BlockSpec((qb, w), f)
    return pl.pallas_call(
        _attn_kernel,
        grid=(t // qb,),
        in_specs=[blk(lambda i: (i, 0)),
                  blk(lambda i: (jnp.maximum(i - 2, 0), 0)), blk(lambda i: (jnp.maximum(i - 1, 0), 0)),
                  blk(lambda i: (i, 0)),
                  blk(lambda i: (jnp.maximum(i - 2, 0), 0)), blk(lambda i: (jnp.maximum(i - 1, 0), 0)),
                  blk(lambda i: (i, 0)),
                  pl.BlockSpec((ATTN_HEADS, ATTN_TAB), lambda i: (0, 0))],
        out_specs=blk(lambda i: (i, 0)),
        out_shape=jax.ShapeDtypeStruct((t, w), BF16),
        scratch_shapes=[pltpu.VMEM((ATTN_HEADS, qb, ATTN_KB), F32)],
        compiler_params=_cparams("arbitrary"),
        name="attn",
    )(q, k, k, k, v, v, v, ftab)


def _attn_distance_table(rel_bias):
    pad = LEFT_CHUNKS * CHUNK
    n_far = pad - REL_MAX + 1
    n_mid = REL_MAX - REL_MIN
    n_ahead = ATTN_KB - n_far - n_mid
    far = rel_bias[:, -1:]
    return jnp.concatenate([jnp.broadcast_to(far, (ATTN_HEADS, n_far)),
                            rel_bias[:, :n_mid][:, ::-1],
                            jnp.broadcast_to(rel_bias[:, :1], (ATTN_HEADS, n_ahead)),
                            jnp.broadcast_to(far, (ATTN_HEADS, ATTN_TAB - ATTN_KB))], axis=1)


def _rglru_kernel(rx_ref, gate_ref, cw_ref, cb_ref, wa_ref, ba_ref, wx_ref, bx_ref, lam_ref,
                  o_ref, xbuf, a_s, u_s, hc, *, tt):
    i = pl.program_id(0)
    width = rx_ref.shape[1]
    bw = width // RNN_BLOCKS

    @pl.when(i == 0)
    def _():
        xbuf[0:SUBLANES, :] = jnp.zeros((SUBLANES, width), F32)
        hc[...] = jnp.zeros(hc.shape, F32)

    xbuf[SUBLANES:SUBLANES + tt, :] = rx_ref[...]
    xc = cb_ref[...] + cw_ref[CONV_W - 1:CONV_W, :] * xbuf[SUBLANES:SUBLANES + tt, :]
    for d in range(1, CONV_W):
        xc = xc + cw_ref[CONV_W - 1 - d:CONV_W - d, :] * xbuf[SUBLANES - d:SUBLANES - d + tt, :]
    tail = xbuf[tt:tt + SUBLANES, :]
    xbuf[0:SUBLANES, :] = tail

    z = -lam_ref[...]
    softplus = jnp.maximum(z, 0.0) + jnp.log1p(jnp.exp(-jnp.abs(z)))
    xcb = xc.astype(BF16)
    for b in range(RNN_BLOCKS):
        sl = slice(b * bw, (b + 1) * bw)
        xb_ = xcb[:, sl]
        r = _sigmoid_tanh(jnp.dot(xb_, wa_ref[b], preferred_element_type=F32) + ba_ref[:, sl])
        ig = _sigmoid_tanh(jnp.dot(xb_, wx_ref[b], preferred_element_type=F32) + bx_ref[:, sl])
        log_a = (-RG_C) * r * softplus[:, sl]
        a = jnp.exp(log_a)
        th = jnp.tanh(log_a)
        one_m_a2 = (-2.0) * th / (1.0 - th)
        a_s[:, sl] = a
        u_s[:, sl] = jnp.sqrt(one_m_a2) * (ig * xc[:, sl])

    row = lax.broadcasted_iota(I32, (SUBLANES, width), 0)

    def body(g, h):
        r0 = pl.multiple_of(g * SUBLANES, SUBLANES)
        a = a_s[pl.ds(r0, SUBLANES), :]
        u = u_s[pl.ds(r0, SUBLANES), :]
        for s in (1, 2, 4):
            a_sh = pltpu.roll(a, s, 0)
            u_sh = pltpu.roll(u, s, 0)
            keep = row >= s
            u = jnp.where(keep, a * u_sh + u, u)
            a = jnp.where(keep, a * a_sh, a)
        hh = a * h + u
        u_s[pl.ds(r0, SUBLANES), :] = hh
        return hh[SUBLANES - 1:SUBLANES, :]

    h_last = lax.fori_loop(0, tt // SUBLANES, body, hc[0:1, :])
    hc[0:1, :] = h_last
    o_ref[...] = (u_s[...] * gate_ref[...]).astype(o_ref.dtype)


def _rglru(rx, gate, conv_w, conv_b, wa_bf, ba, wx_bf, bx, lam):
    t = rx.shape[0]
    width = conv_w.shape[1]
    bw = width // RNN_BLOCKS
    tt = 512
    vec = lambda: pl.BlockSpec((1, width), lambda i: (0, 0))
    wblk = lambda: pl.BlockSpec((RNN_BLOCKS, bw, bw), lambda i: (0, 0, 0))
    return pl.pallas_call(
        functools.partial(_rglru_kernel, tt=tt),
        grid=(t // tt,),
        in_specs=[pl.BlockSpec((tt, width), lambda i: (i, 0)),
                  pl.BlockSpec((tt, width), lambda i: (i, 0)),
                  pl.BlockSpec((CONV_W, width), lambda i: (0, 0)), vec(),
                  wblk(), vec(), wblk(), vec(), vec()],
        out_specs=pl.BlockSpec((tt, width), lambda i: (i, 0)),
        out_shape=jax.ShapeDtypeStruct((t, width), BF16),
        scratch_shapes=[pltpu.VMEM((tt + SUBLANES, width), F32),
                        pltpu.VMEM((tt, width), F32),
                        pltpu.VMEM((tt, width), F32),
                        pltpu.VMEM((SUBLANES, width), F32)],
        compiler_params=_cparams("arbitrary"),
        name="rglru",
    )(rx, gate, conv_w, conv_b, wa_bf, ba, wx_bf, bx, lam)


def _upmerge_kernel(ya_ref, yb_ref, sga_ref, sgb_ref, wa_ref, wr_ref, o_ref, wab_ref, wrb_ref):
    @pl.when(pl.program_id(1) == 0)
    def _():
        wab_ref[...] = wa_ref[...].astype(BF16)
        wrb_ref[...] = wr_ref[...].astype(BF16)

    up_a = jnp.dot(ya_ref[...], wab_ref[...], preferred_element_type=F32)
    up_b = jnp.dot(yb_ref[...], wrb_ref[...], preferred_element_type=F32)
    o_ref[...] = (sga_ref[...] * up_a + sgb_ref[...] * up_b).astype(o_ref.dtype)


def _upmerge(ya, yb, sg, wa, wr):
    t, aw = ya.shape
    rw = yb.shape[1]
    d = wa.shape[1]
    tm, tn = PROJ_TM, PROJ_TN
    nj = d // tn
    return pl.pallas_call(
        _upmerge_kernel,
        grid=(nj, t // tm),
        in_specs=[pl.BlockSpec((tm, aw), lambda j, i: (i, 0)),
                  pl.BlockSpec((tm, rw), lambda j, i: (i, 0)),
                  pl.BlockSpec((tm, tn), lambda j, i: (i, j)),
                  pl.BlockSpec((tm, tn), lambda j, i: (i, j + nj)),
                  pl.BlockSpec((aw, tn), lambda j, i: (0, j)),
                  pl.BlockSpec((rw, tn), lambda j, i: (0, j))],
        out_specs=pl.BlockSpec((tm, tn), lambda j, i: (i, j)),
        out_shape=jax.ShapeDtypeStruct((t, d), BF16),
        scratch_shapes=[pltpu.VMEM((aw, tn), BF16), pltpu.VMEM((rw, tn), BF16)],
        compiler_params=_cparams("arbitrary", "arbitrary"),
        name="upmerge",
    )(ya, yb, sg, sg, wa, wr)


def _mixer_out_kernel(mg_ref, x_ref, wo_ref, g1_ref, n2_ref, sc2_ref, sh2_ref, wrh_ref, brt_ref,
                      h1_ref, hnp_ref, lg_ref):
    proj = jnp.dot(mg_ref[...], wo_ref[...], preferred_element_type=F32)
    h1 = x_ref[...] + g1_ref[...] * proj
    h1_ref[...] = h1
    ms = jnp.mean(h1 * h1, axis=-1, keepdims=True)
    hn = h1 * lax.rsqrt(ms + EPS) * n2_ref[...]
    hn = hn * (1.0 + sc2_ref[...]) + sh2_ref[...]
    hb = hn.astype(BF16)
    hb32 = hb.astype(F32)
    hl = (hn - hb32).astype(BF16)
    ne = lg_ref.shape[1]
    hh = jnp.dot(hb, wrh_ref[...], preferred_element_type=F32)
    lh = jnp.dot(hl, wrh_ref[:, :ne], preferred_element_type=F32)
    lg_ref[...] = hh[:, :ne] + hh[:, ne:] + lh + brt_ref[...]
    half = hn.shape[1] // 2
    lo = pltpu.bitcast(hb32[:, :half], U32) >> 16
    hi = pltpu.bitcast(hb32[:, half:], U32) & jnp.uint32(0xFFFF0000)
    packed = hi | lo
    tm = packed.shape[0]
    for s in range(PACK_TILES):
        hnp_ref[pl.ds(s, tm, stride=PACK_TILES), :] = packed[:, s * LANES:(s + 1) * LANES]


def _mixer_out(merged, x2, wo_bf, g1, n2g, sc2, sh2, w_router, b_router):
    t, d = x2.shape
    ne = w_router.shape[1]
    tm = 512
    w_hi = w_router.astype(BF16)
    w_lo = (w_router - w_hi.astype(F32)).astype(BF16)
    w_split = jnp.concatenate([w_hi, w_lo], axis=1)
    const = lambda shape: pl.BlockSpec(shape, lambda i: (0,) * len(shape), pipeline_mode=pl.Buffered(1))
    return pl.pallas_call(
        _mixer_out_kernel,
        grid=(t // tm,),
        in_specs=[pl.BlockSpec((tm, d), lambda i: (i, 0)),
                  pl.BlockSpec((tm, d), lambda i: (i, 0)),
                  const((d, d)),
                  const((1, d)), const((1, d)), const((1, d)), const((1, d)),
                  const((d, 2 * ne)), const((1, ne))],
        out_specs=[pl.BlockSpec((tm, d), lambda i: (i, 0)),
                   pl.BlockSpec((tm * PACK_TILES, LANES), lambda i: (i, 0)),
                   pl.BlockSpec((tm, ne), lambda i: (i, 0))],
        out_shape=[jax.ShapeDtypeStruct((t, d), F32),
                   jax.ShapeDtypeStruct((t * PACK_TILES, LANES), U32),
                   jax.ShapeDtypeStruct((t, ne), F32)],
        compiler_params=_cparams("arbitrary"),
        name="mixer_out",
    )(merged, x2, wo_bf, g1, n2g, sc2, sh2, w_split, b_router)


def _route_kernel(lg_ref, idx_ref, w_ref, pos_ref, cnt_ref, carry, *, rt):
    i = pl.program_id(0)
    ne = lg_ref.shape[1]

    @pl.when(i == 0)
    def _():
        carry[...] = jnp.zeros(carry.shape, F32)

    l = lg_ref[...]
    lane = lax.broadcasted_iota(I32, (rt, ne), 1).astype(F32)
    vals, idxs, hots = [], [], []
    for _ in range(TOP_K):
        m = jnp.max(l, axis=-1, keepdims=True)
        ik = jnp.min(jnp.where(l == m, lane, float(ne)), axis=-1, keepdims=True)
        hot = lane == ik
        vals.append(m)
        idxs.append(ik)
        hots.append(hot)
        l = jnp.where(hot, -jnp.inf, l)
    es = [jnp.exp(v - vals[0]) for v in vals]
    tot = es[0] + es[1] + es[2] + es[3]
    sel = jnp.zeros((rt, ne), F32)
    for hot in hots:
        sel = sel + hot.astype(F32)
    rr = lax.broadcasted_iota(I32, (rt, rt), 0)
    cc = lax.broadcasted_iota(I32, (rt, rt), 1)
    tri = (cc < rr).astype(BF16)
    before = jnp.dot(tri, sel.astype(BF16), preferred_element_type=F32) + carry[0:1, :]
    out_lane = lax.broadcasted_iota(I32, (rt, LANES), 1)
    idx_o = jnp.zeros((rt, LANES), F32)
    w_o = jnp.zeros((rt, LANES), F32)
    pos_o = jnp.zeros((rt, LANES), F32)
    for k in range(TOP_K):
        pk = jnp.sum(jnp.where(hots[k], before, 0.0), axis=-1, keepdims=True)
        idx_o = jnp.where(out_lane == k, idxs[k], idx_o)
        w_o = jnp.where(out_lane == k, es[k] / tot, w_o)
        pos_o = jnp.where(out_lane == k, pk, pos_o)
    idx_ref[...] = idx_o.astype(I32)
    w_ref[...] = w_o
    pos_ref[...] = pos_o.astype(I32)
    carry[0:1, :] = carry[0:1, :] + jnp.sum(sel, axis=0, keepdims=True)
    cnt_ref[...] = carry[...]


def _route(logits):
    t, ne = logits.shape
    rt = 512
    wide = lambda dt: jax.ShapeDtypeStruct((t, LANES), dt)
    return pl.pallas_call(
        functools.partial(_route_kernel, rt=rt),
        grid=(t // rt,),
        in_specs=[pl.BlockSpec((rt, ne), lambda i: (i, 0))],
        out_specs=[pl.BlockSpec((rt, LANES), lambda i: (i, 0)),
                   pl.BlockSpec((rt, LANES), lambda i: (i, 0)),
                   pl.BlockSpec((rt, LANES), lambda i: (i, 0)),
                   pl.BlockSpec((SUBLANES, ne), lambda i: (0, 0))],
        out_shape=[wide(I32), wide(F32), wide(I32), jax.ShapeDtypeStruct((SUBLANES, ne), F32)],
        scratch_shapes=[pltpu.VMEM((SUBLANES, ne), F32)],
        compiler_params=_cparams("arbitrary"),
        name="route",
    )(logits)


def _unpack_rows(x_ref):
    los, his = [], []
    for s in range(PACK_TILES):
        p = x_ref[pl.ds(s, ROW_BLOCK, stride=PACK_TILES), :]
        los.append(pltpu.bitcast(p << 16, F32).astype(BF16))
        his.append(pltpu.bitcast(p & jnp.uint32(0xFFFF0000), F32).astype(BF16))
    return jnp.concatenate(los + his, axis=1)


def _first_of_expert(be_ref, m, na):
    mm = jnp.minimum(m, na - 1)
    prev = be_ref[jnp.maximum(mm - 1, 0)]
    return (mm == 0) | (be_ref[mm] != prev)


def _weight_stream(be_ref, na_ref, end_ref, n_tiles, copies, convert):
    n = pl.program_id(0)
    m = pl.program_id(1)
    na = na_ref[0]
    active = m < na

    @pl.when((n == 0) & (m == 0))
    def _():
        for cp in copies(be_ref[0], 0):
            cp.start(priority=1)

    @pl.when(active & _first_of_expert(be_ref, m, na))
    def _():
        e = be_ref[m]
        for cp in copies(e, n):
            cp.wait()
        convert()
        run_end = end_ref[e]
        wrap = run_end >= na
        n2 = jnp.where(wrap, n + 1, n)
        m2 = jnp.where(wrap, 0, run_end)

        @pl.when(n2 < n_tiles)
        def _():
            for cp in copies(be_ref[m2], n2):
                cp.start(priority=1)

    return active


def _gu_weight_stream(be_ref, na_ref, end_ref, w_hbm, stg_g, stg_u, wgb, wub, sem, *, tn, col, f):
    def copies(e, n):
        return [pltpu.make_async_copy(w_hbm.at[e, :, pl.ds(col * tn, tn)], stg_g, sem.at[0]),
                pltpu.make_async_copy(w_hbm.at[e, :, pl.ds(f + col * tn, tn)], stg_u, sem.at[1])]

    def convert():
        wgb[...] = stg_g[...].astype(BF16)
        wub[...] = stg_u[...].astype(BF16)

    return _weight_stream(be_ref, na_ref, end_ref, 1, copies, convert)


def _gu_compute(x, wgb, wub, bg_ref, bu_ref, o_ref):
    g = jnp.dot(x, wgb[...], preferred_element_type=F32) + bg_ref[...]
    u = jnp.dot(x, wub[...], preferred_element_type=F32) + bu_ref[...]
    g = jnp.minimum(g, SWIGLU_LIMIT)
    u = jnp.clip(u, -SWIGLU_LIMIT, SWIGLU_LIMIT)
    glu = g * _sigmoid_tanh(SWIGLU_ALPHA * g)
    o_ref[...] = ((u + 1.0) * glu).astype(o_ref.dtype)


def _moe_gu_gather_kernel(be_ref, na_ref, end_ref, tok0_ref, tok1_ref, tok2_ref, hn_hbm, w_hbm, bg_ref,
                          bu_ref, o_ref, xb_ref, stg_g, stg_u, wgb, wub, xbuf0, xbuf1, xbuf2, sem, gsem,
                          *, tn, col, f):
    m = pl.program_id(1)
    na = na_ref[0]
    slot = m % MOE_BUFS
    xbufs = (xbuf0, xbuf1, xbuf2)

    def gather(tok_ref, s):
        for r in range(ROW_BLOCK):
            src = pl.multiple_of(tok_ref[0, 0, r], PACK_TILES)
            pltpu.make_async_copy(hn_hbm.at[pl.ds(src, PACK_TILES)],
                                  xbufs[s].at[pl.ds(r * PACK_TILES, PACK_TILES)], gsem.at[s]).start()

    def gather_wait(s):
        pltpu.make_async_copy(hn_hbm.at[pl.ds(0, ROW_BLOCK * PACK_TILES)], xbufs[s], gsem.at[s]).wait()

    @pl.when(m == 0)
    def _():
        gather(tok0_ref, 0)
        gather(tok1_ref, 1)

    active = _gu_weight_stream(be_ref, na_ref, end_ref, w_hbm, stg_g, stg_u, wgb, wub, sem,
                               tn=tn, col=col, f=f)

    def step(s):
        gather_wait(s)
        gather(tok2_ref, (s + 2) % MOE_BUFS)
        xb_ref[...] = xbufs[s][...]
        _gu_compute(_unpack_rows(xbufs[s]), wgb, wub, bg_ref, bu_ref, o_ref)

    for s in range(MOE_BUFS):
        @pl.when(active & (slot == s))
        def _(s=s):
            step(s)

        @pl.when(active & (m == na - 1) & (slot == s))
        def _(s=s):
            gather_wait((s + 1) % MOE_BUFS)
            gather_wait((s + 2) % MOE_BUFS)

    @pl.when(jnp.logical_not(active))
    def _():
        o_ref[...] = jnp.zeros(o_ref.shape, o_ref.dtype)
        xb_ref[...] = jnp.zeros(xb_ref.shape, xb_ref.dtype)


def _moe_gu_dense_kernel(be_ref, na_ref, end_ref, x_ref, w_hbm, bg_ref, bu_ref, o_ref,
                         stg_g, stg_u, wgb, wub, sem, *, tn, col, f):
    active = _gu_weight_stream(be_ref, na_ref, end_ref, w_hbm, stg_g, stg_u, wgb, wub, sem,
                               tn=tn, col=col, f=f)

    @pl.when(active)
    def _():
        _gu_compute(_unpack_rows(x_ref), wgb, wub, bg_ref, bu_ref, o_ref)

    @pl.when(jnp.logical_not(active))
    def _():
        o_ref[...] = jnp.zeros(o_ref.shape, o_ref.dtype)


def _moe_gu(hn_packed, src_tok, w_gu, b_gu, blk_e, n_active, run_end):
    nb = src_tok.shape[0]
    ne, d, f2 = w_gu.shape
    f = f2 // 2
    tn = MOE_GU_TN
    mc = lambda m, na: jnp.minimum(m, na[0] - 1)
    b3 = b_gu.reshape(ne, 1, f2)
    weight_scratch = [pltpu.VMEM((d, tn), F32), pltpu.VMEM((d, tn), F32),
                      pltpu.VMEM((d, tn), BF16), pltpu.VMEM((d, tn), BF16)]
    xrows = ROW_BLOCK * PACK_TILES
    act_shape = jax.ShapeDtypeStruct((nb * ROW_BLOCK, tn), BF16)

    def bias_specs(col):
        return [pl.BlockSpec((None, 1, tn), lambda n, m, be, na, en: (be[mc(m, na)], 0, col)),
                pl.BlockSpec((None, 1, tn), lambda n, m, be, na, en: (be[mc(m, na)], 0, col + f // tn))]

    tok_blk = lambda f_: pl.BlockSpec((1, 1, ROW_BLOCK), f_, memory_space=pltpu.SMEM)
    act0, xb = pl.pallas_call(
        functools.partial(_moe_gu_gather_kernel, tn=tn, col=0, f=f),
        grid_spec=pltpu.PrefetchScalarGridSpec(
            num_scalar_prefetch=3,
            grid=(1, nb),
            in_specs=[tok_blk(lambda n, m, be, na, en: (0, 0, 0)),
                      tok_blk(lambda n, m, be, na, en: (1 % na[0], 0, 0)),
                      tok_blk(lambda n, m, be, na, en: ((m + 2) % na[0], 0, 0)),
                      pl.BlockSpec(memory_space=pl.ANY),
                      pl.BlockSpec(memory_space=pl.ANY)] + bias_specs(0),
            out_specs=[pl.BlockSpec((ROW_BLOCK, tn), lambda n, m, be, na, en: (m, 0)),
                       pl.BlockSpec((xrows, LANES), lambda n, m, be, na, en: (m, 0))],
            scratch_shapes=weight_scratch + [pltpu.VMEM((xrows, LANES), U32)] * MOE_BUFS
                           + [pltpu.SemaphoreType.DMA((2,)), pltpu.SemaphoreType.DMA((MOE_BUFS,))],
        ),
        out_shape=[act_shape, jax.ShapeDtypeStruct((nb * xrows, LANES), U32)],
        compiler_params=_cparams("arbitrary", "arbitrary"),
        name="moe_gu0",
    )(blk_e, n_active, run_end, src_tok, src_tok, src_tok, hn_packed, w_gu, b3, b3)

    acts = [act0]
    for col in range(1, f // tn):
        acts.append(pl.pallas_call(
            functools.partial(_moe_gu_dense_kernel, tn=tn, col=col, f=f),
            grid_spec=pltpu.PrefetchScalarGridSpec(
                num_scalar_prefetch=3,
                grid=(1, nb),
                in_specs=[pl.BlockSpec((xrows, LANES), lambda n, m, be, na, en: (mc(m, na), 0)),
                          pl.BlockSpec(memory_space=pl.ANY)] + bias_specs(col),
                out_specs=pl.BlockSpec((ROW_BLOCK, tn), lambda n, m, be, na, en: (m, 0)),
                scratch_shapes=weight_scratch + [pltpu.SemaphoreType.DMA((2,))],
            ),
            out_shape=act_shape,
            compiler_params=_cparams("arbitrary", "arbitrary"),
            name="moe_gu%d" % col,
        )(blk_e, n_active, run_end, xb, w_gu, b3, b3))
    return acts


def _moe_down_kernel(be_ref, na_ref, end_ref, slot_ref, *refs, n_act):
    a_refs = refs[:n_act]
    w_hbm, b_ref, y_hbm, stg, wb, ybuf0, ybuf1, ybuf2, sem, ssem = refs[n_act:]
    m = pl.program_id(1)
    na = na_ref[0]
    p = m % MOE_BUFS
    ybufs = (ybuf0, ybuf1, ybuf2)

    def scatter(s):
        for r in range(ROW_BLOCK):
            pltpu.make_async_copy(ybufs[s].at[pl.ds(r, 1)], y_hbm.at[pl.ds(slot_ref[0, 0, r], 1)],
                                  ssem.at[s]).start()

    def scatter_wait(s):
        pltpu.make_async_copy(ybufs[s], y_hbm.at[pl.ds(0, ROW_BLOCK)], ssem.at[s]).wait()

    @pl.when(m == 0)
    def _():
        ybuf2[...] = jnp.zeros(ybuf2.shape, ybuf2.dtype)

    for s in range(MOE_BUFS):
        @pl.when((m >= 2) & (m <= na + 2) & (p == s))
        def _(s=s):
            scatter_wait(s)

    def copies(e, n):
        return [pltpu.make_async_copy(w_hbm.at[e], stg, sem.at[0])]

    def convert():
        wb[...] = stg[...].astype(BF16)

    active = _weight_stream(be_ref, na_ref, end_ref, 1, copies, convert)

    for s in range(MOE_BUFS):
        @pl.when(active & (p == s))
        def _(s=s):
            scatter((s + MOE_BUFS - 1) % MOE_BUFS)
            a = jnp.concatenate([a_ref[...] for a_ref in a_refs], axis=1)
            ybufs[s][...] = jnp.dot(a, wb[...], preferred_element_type=F32) + b_ref[...]

        @pl.when((m == na) & (p == s))
        def _(s=s):
            scatter((s + MOE_BUFS - 1) % MOE_BUFS)


def _moe_down(acts, slot_prev, w_down, b_down, blk_e, n_active, run_end, n_slots):
    n_rows, tf = acts[0].shape
    ne, f, d = w_down.shape
    nb = n_rows // ROW_BLOCK
    mc = lambda m, na: jnp.minimum(m, na[0] - 1)
    grid_spec = pltpu.PrefetchScalarGridSpec(
        num_scalar_prefetch=3,
        grid=(1, nb + MOE_BUFS),
        in_specs=[pl.BlockSpec((1, 1, ROW_BLOCK), lambda n, m, be, na, en: (jnp.minimum(m, nb), 0, 0),
                               memory_space=pltpu.SMEM),
                  *[pl.BlockSpec((ROW_BLOCK, tf), lambda n, m, be, na, en: (mc(m, na), 0))] * len(acts),
                  pl.BlockSpec(memory_space=pl.ANY),
                  pl.BlockSpec((None, 1, d), lambda n, m, be, na, en: (be[mc(m, na)], 0, 0))],
        out_specs=pl.BlockSpec(memory_space=pl.ANY),
        scratch_shapes=[pltpu.VMEM((f, d), F32), pltpu.VMEM((f, d), BF16),
                        *[pltpu.VMEM((ROW_BLOCK, d), F32)] * MOE_BUFS,
                        pltpu.SemaphoreType.DMA((1,)), pltpu.SemaphoreType.DMA((MOE_BUFS,))],
    )
    return pl.pallas_call(
        functools.partial(_moe_down_kernel, n_act=len(acts)),
        grid_spec=grid_spec,
        out_shape=jax.ShapeDtypeStruct((n_slots + ROW_BLOCK, d), F32),
        compiler_params=_cparams("arbitrary", "arbitrary"),
        name="moe_down",
    )(blk_e, n_active, run_end, slot_prev, *acts, w_down, b_down.reshape(ne, 1, d))


def _combine_kernel(w_ref, h1_ref, g2_ref, *refs):
    y_refs, o_ref = refs[:TOP_K], refs[TOP_K]
    w = w_ref[...]
    acc = y_refs[0][...] * w[:, 0:1]
    for k in range(1, TOP_K):
        acc = acc + y_refs[k][...] * w[:, k:k + 1]
    o_ref[...] = h1_ref[...] + g2_ref[...] * acc


def _combine(top_w, h1, g2, y_slots):
    t, d = h1.shape
    tm = 256
    nt = t // tm
    return pl.pallas_call(
        _combine_kernel,
        grid=(nt,),
        in_specs=[pl.BlockSpec((tm, LANES), lambda i: (i, 0)),
                  pl.BlockSpec((tm, d), lambda i: (i, 0)),
                  pl.BlockSpec((1, d), lambda i: (0, 0))]
                 + [pl.BlockSpec((tm, d), lambda i, k=k: (k * nt + i, 0)) for k in range(TOP_K)],
        out_specs=pl.BlockSpec((tm, d), lambda i: (i, 0)),
        out_shape=jax.ShapeDtypeStruct((t, d), F32),
        compiler_params=_cparams("arbitrary"),
        name="combine",
    )(top_w, h1, g2, *([y_slots] * TOP_K))


def kernel(x, c, ada_w, ada_b, norm1_g, w_in, q_norm_g, k_norm_g, rel_bias, conv_w, conv_b,
           rg_a_w, rg_a_b, rg_x_w, rg_x_b, rg_lambda, w_attn_up, w_rnn_up, w_out, norm2_g,
           w_router, b_router, w_gu, b_gu, w_down, b_down):
    b, s, d = x.shape
    assert b == 1 and ada_w.shape[0] == 1, "single batch row, single layer"
    t = s
    x2 = x.reshape(t, d)
    aw = ATTN_HEADS * HEAD_DIM
    rw = conv_w.shape[2]

    mod = _ada(c, ada_w[0], ada_b[0])
    sh1, sc1, g1, sh2, sc2, g2 = [mod[:, k * d:(k + 1) * d] for k in range(6)]

    w_in0 = w_in[0]
    qg = q_norm_g[0].reshape(1, HEAD_DIM)
    kg = k_norm_g[0].reshape(1, HEAD_DIM)
    n1g = norm1_g[0].reshape(1, d)
    xn = _norm_mod(x2, n1g, sc1, sh1)
    q = _proj(xn, w_in0, qg, col0=0, width=aw, kind="q", out_dtype=BF16, name="proj_q")
    k = _proj(xn, w_in0, kg, col0=aw, width=aw, kind="k", out_dtype=BF16, name="proj_k")
    v = _proj(xn, w_in0, kg, col0=2 * aw, width=aw, kind="id", out_dtype=BF16, name="proj_v")
    rx = _proj(xn, w_in0, kg, col0=3 * aw, width=rw, kind="id", out_dtype=F32, name="proj_rx")
    gate = _proj(xn, w_in0, kg, col0=3 * aw + rw, width=rw, kind="gelu", out_dtype=F32, name="proj_gate")
    sg = _proj(xn, w_in0, kg, col0=3 * aw + 2 * rw, width=2 * d, kind="sigmoid", out_dtype=F32,
               name="proj_sg")

    ya = _attn(q, k, v, _attn_distance_table(rel_bias[0]))
    yb = _rglru(rx, gate, conv_w[0], conv_b[0].reshape(1, rw),
                rg_a_w[0].astype(BF16), rg_a_b[0].reshape(1, rw),
                rg_x_w[0].astype(BF16), rg_x_b[0].reshape(1, rw), rg_lambda[0].reshape(1, rw))

    merged = _upmerge(ya, yb, sg, w_attn_up[0], w_rnn_up[0])
    h1, hn_packed, logits = _mixer_out(
        merged, x2, w_out[0].astype(BF16), g1, norm2_g[0].reshape(1, d), sc2, sh2,
        w_router[0], b_router[0].reshape(1, N_EXPERTS))

    idx_w, topw_w, pos_w, cnt = _route(logits)
    idx = idx_w[:, :TOP_K]
    pos = pos_w[:, :TOP_K]
    counts = cnt[0].astype(I32)
    padded = (counts + ROW_BLOCK - 1) // ROW_BLOCK * ROW_BLOCK
    pend = jnp.cumsum(padded)
    pstart = pend - padded
    experts = jnp.arange(N_EXPERTS, dtype=I32)
    dest = (pos + jnp.sum(jnp.where(idx[:, :, None] == experts, pstart, 0), axis=-1)).reshape(-1)
    n_blocks = (t * TOP_K) // ROW_BLOCK + N_EXPERTS
    n_rows = n_blocks * ROW_BLOCK
    blk_row0 = jnp.arange(n_blocks, dtype=I32) * ROW_BLOCK
    blk_e = jnp.minimum(jnp.sum((pend[None, :] <= blk_row0[:, None]).astype(I32), axis=1), N_EXPERTS - 1)
    n_active = (pend[-1:] // ROW_BLOCK).astype(I32)
    run_end = (pend // ROW_BLOCK).astype(I32)
    n_slots = t * TOP_K
    slot_ids = (jnp.arange(t, dtype=I32)[:, None] + jnp.arange(TOP_K, dtype=I32)[None, :] * t).reshape(-1)
    spare = n_slots + jnp.arange(ROW_BLOCK, dtype=I32)
    row_slot = jnp.tile(spare, n_blocks).at[dest].set(slot_ids, unique_indices=True, mode="promise_in_bounds")
    src_tok = (jnp.where(row_slot < n_slots, row_slot % t, 0) * PACK_TILES).reshape(n_blocks, 1, ROW_BLOCK)
    slot_prev = jnp.concatenate([spare, row_slot]).reshape(n_blocks + 1, 1, ROW_BLOCK)

    acts = _moe_gu(hn_packed, src_tok, w_gu[0], b_gu[0], blk_e, n_active, run_end)
    y_slots = _moe_down(acts, slot_prev, w_down[0], b_down[0], blk_e, n_active, run_end, n_slots)
    out = _combine(topw_w, h1, g2, y_slots)
    return out.reshape(b, s, d)
```

```python
import functools
import math

import jax
import jax.numpy as jnp
from jax import lax
from jax.experimental import pallas as pl
from jax.experimental.pallas import tpu as pltpu

F32 = jnp.float32
BF16 = jnp.bfloat16
U32 = jnp.uint32
I32 = jnp.int32

CHUNK = 64
LEFT_CHUNKS = 8
ATTN_HEADS = 8
HEAD_DIM = 128
REL_MAX = 256
REL_MIN = -(CHUNK - 1)
RNN_BLOCKS = 8
CONV_W = 4
RG_C = 8.0
N_EXPERTS = 32
TOP_K = 4
SWIGLU_LIMIT = 7.0
SWIGLU_ALPHA = 1.702
EPS = 1e-6
NEG_BIG = -1e30

LANES = 128
SUBLANES = 8
VMEM_LIMIT = 56 * 1024 * 1024

ROW_BLOCK = 512
MOE_BUFS = 3
MOE_GU_TN = 1024
PACK_TILES = 8
PROJ_TM = 1024
PROJ_TN = 1024
ATTN_QB = 4 * CHUNK
ATTN_KB = ATTN_QB + LEFT_CHUNKS * CHUNK
ATTN_TAB = ATTN_KB + ATTN_QB
CHUNK_SHIFT = CHUNK.bit_length() - 1
assert 1 << CHUNK_SHIFT == CHUNK


def _cparams(*sem):
    return pltpu.CompilerParams(dimension_semantics=sem, vmem_limit_bytes=VMEM_LIMIT)


def _ada_kernel(c_ref, w_ref, b_ref, o_ref):
    c = c_ref[...]
    ca = c * jax.nn.sigmoid(c)
    o_ref[...] = jnp.dot(ca, w_ref[...], preferred_element_type=F32,
                         precision=lax.Precision.HIGHEST) + b_ref[...]


def _ada(c, ada_w, ada_b):
    d, n = ada_w.shape
    tn = 1024
    c8 = jnp.broadcast_to(c, (SUBLANES, d))
    out = pl.pallas_call(
        _ada_kernel,
        grid=(n // tn,),
        in_specs=[pl.BlockSpec((SUBLANES, d), lambda j: (0, 0)),
                  pl.BlockSpec((d, tn), lambda j: (0, j)),
                  pl.BlockSpec((1, tn), lambda j: (0, j))],
        out_specs=pl.BlockSpec((SUBLANES, tn), lambda j: (0, j)),
        out_shape=jax.ShapeDtypeStruct((SUBLANES, n), F32),
        compiler_params=_cparams("arbitrary"),
        name="ada",
    )(c8, ada_w, ada_b.reshape(1, n))
    return out[0:1]


def _sigmoid_tanh(x):
    return 0.5 * jnp.tanh(0.5 * x) + 0.5


def _gelu_tanh(x):
    return 0.5 * x * (1.0 + jnp.tanh(math.sqrt(2.0 / math.pi) * (x + 0.044715 * (x * x * x))))


def _head_rms(a, g):
    outs = []
    for hh in range(a.shape[1] // HEAD_DIM):
        s = a[:, hh * HEAD_DIM:(hh + 1) * HEAD_DIM]
        ms = jnp.mean(s * s, axis=-1, keepdims=True)
        outs.append(s * lax.rsqrt(ms + EPS) * g)
    return jnp.concatenate(outs, axis=1)


def _norm_mod_kernel(x_ref, g_ref, sc_ref, sh_ref, o_ref):
    x = x_ref[...]
    ms = jnp.mean(x * x, axis=-1, keepdims=True)
    xn = x * lax.rsqrt(ms + EPS) * g_ref[...]
    o_ref[...] = (xn * (1.0 + sc_ref[...]) + sh_ref[...]).astype(o_ref.dtype)


def _norm_mod(x2, g, sc, sh):
    t, d = x2.shape
    tm = 512
    vec = lambda: pl.BlockSpec((1, d), lambda i: (0, 0))
    return pl.pallas_call(
        _norm_mod_kernel,
        grid=(t // tm,),
        in_specs=[pl.BlockSpec((tm, d), lambda i: (i, 0)), vec(), vec(), vec()],
        out_specs=pl.BlockSpec((tm, d), lambda i: (i, 0)),
        out_shape=jax.ShapeDtypeStruct((t, d), BF16),
        compiler_params=_cparams("arbitrary"),
        name="norm1",
    )(x2, g, sc, sh)


def _proj_kernel(x_ref, w_ref, hg_ref, o_ref, wb_ref, *, kind):
    @pl.when(pl.program_id(1) == 0)
    def _():
        wb_ref[...] = w_ref[...].astype(BF16)

    acc = jnp.dot(x_ref[...], wb_ref[...], preferred_element_type=F32)
    if kind == "q":
        r = _head_rms(acc, hg_ref[...]) * (1.0 / math.sqrt(HEAD_DIM))
    elif kind == "k":
        r = _head_rms(acc, hg_ref[...])
    elif kind == "gelu":
        r = _gelu_tanh(acc)
    elif kind == "sigmoid":
        r = _sigmoid_tanh(acc)
    else:
        r = acc
    o_ref[...] = r.astype(o_ref.dtype)


def _proj(xn, w, head_gain, *, col0, width, kind, out_dtype, name):
    t, d = xn.shape
    tm, tn = PROJ_TM, PROJ_TN
    j0 = col0 // tn
    return pl.pallas_call(
        functools.partial(_proj_kernel, kind=kind),
        grid=(width // tn, t // tm),
        in_specs=[pl.BlockSpec((tm, d), lambda j, i: (i, 0)),
                  pl.BlockSpec((d, tn), lambda j, i: (0, j + j0)),
                  pl.BlockSpec((1, HEAD_DIM), lambda j, i: (0, 0))],
        out_specs=pl.BlockSpec((tm, tn), lambda j, i: (i, j)),
        out_shape=jax.ShapeDtypeStruct((t, width), out_dtype),
        scratch_shapes=[pltpu.VMEM((d, tn), BF16)],
        compiler_params=_cparams("arbitrary", "arbitrary"),
        name=name,
    )(xn, w, head_gain)


def _attn_kernel(q_ref, k0_ref, k1_ref, k2_ref, v0_ref, v1_ref, v2_ref, ftab_ref, o_ref, bias_ref):
    i = pl.program_id(0)
    col = lax.broadcasted_iota(I32, (ATTN_QB, ATTN_KB), 1)

    @pl.when(i == 0)
    def _():
        row = lax.broadcasted_iota(I32, (ATTN_QB, ATTN_KB), 0)
        cq = row >> CHUNK_SHIFT
        ck = col >> CHUNK_SHIFT
        visible = (ck >= cq) & (ck <= cq + LEFT_CHUNKS)
        for h in range(ATTN_HEADS):
            g = jnp.broadcast_to(ftab_ref[h:h + 1, :], (ATTN_QB, ATTN_TAB))
            y = pltpu.roll(g, 0, 1, stride=1, stride_axis=0)
            bias_ref[h] = jnp.where(visible, y[:, :ATTN_KB], NEG_BIG)

    def heads(mask_left_pad):
        in_seq = col >= (LEFT_CHUNKS * CHUNK - i * ATTN_QB)
        for h in range(ATTN_HEADS):
            sl = slice(h * HEAD_DIM, (h + 1) * HEAD_DIM)
            q = q_ref[:, sl]
            k = jnp.concatenate([k0_ref[:, sl], k1_ref[:, sl], k2_ref[:, sl]], axis=0)
            v = jnp.concatenate([v0_ref[:, sl], v1_ref[:, sl], v2_ref[:, sl]], axis=0)
            s = lax.dot_general(q, k, (((1,), (1,)), ((), ())), preferred_element_type=F32)
            s = s + bias_ref[h]
            if mask_left_pad:
                s = jnp.where(in_seq, s, NEG_BIG)
            m = jnp.max(s, axis=-1, keepdims=True)
            p = jnp.exp(s - m)
            l = jnp.sum(p, axis=-1, keepdims=True)
            o = jnp.dot(p.astype(BF16), v, preferred_element_type=F32) / l
            o_ref[:, sl] = o.astype(o_ref.dtype)

    first_blocks = (LEFT_CHUNKS * CHUNK) // ATTN_QB

    @pl.when(i < first_blocks)
    def _():
        heads(True)

    @pl.when(i >= first_blocks)
    def _():
        heads(False)


def _attn(q, k, v, ftab):
    t = q.shape[0]
    w = ATTN_HEADS * HEAD_DIM
    qb = ATTN_QB
    blk = lambda f: pl.BlockSpec((qb, w), f)
    return pl.pallas_call(
        _attn_kernel,
        grid=(t // qb,),
        in_specs=[blk(lambda i: (i, 0)),
                  blk(lambda i: (jnp.maximum(i - 2, 0), 0)), blk(lambda i: (jnp.maximum(i - 1, 0), 0)),
                  blk(lambda i: (i, 0)),
                  blk(lambda i: (jnp.maximum(i - 2, 0), 0)), blk(lambda i: (jnp.maximum(i - 1, 0), 0)),
                  blk(lambda i: (i, 0)),
                  pl.BlockSpec((ATTN_HEADS, ATTN_TAB), lambda i: (0, 0))],
        out_specs=blk(lambda i: (i, 0)),
        out_shape=jax.ShapeDtypeStruct((t, w), BF16),
        scratch_shapes=[pltpu.VMEM((ATTN_HEADS, qb, ATTN_KB), F32)],
        compiler_params=_cparams("arbitrary"),
        name="attn",
    )(q, k, k, k, v, v, v, ftab)


def _attn_distance_table(rel_bias):
    pad = LEFT_CHUNKS * CHUNK
    n_far = pad - REL_MAX + 1
    n_mid = REL_MAX - REL_MIN
    n_ahead = ATTN_KB - n_far - n_mid
    far = rel_bias[:, -1:]
    return jnp.concatenate([jnp.broadcast_to(far, (ATTN_HEADS, n_far)),
                            rel_bias[:, :n_mid][:, ::-1],
                            jnp.broadcast_to(rel_bias[:, :1], (ATTN_HEADS, n_ahead)),
                            jnp.broadcast_to(far, (ATTN_HEADS, ATTN_TAB - ATTN_KB))], axis=1)


def _rglru_kernel(rx_ref, gate_ref, cw_ref, cb_ref, wa_ref, ba_ref, wx_ref, bx_ref, lam_ref,
                  o_ref, xbuf, a_s, u_s, hc, *, tt):
    i = pl.program_id(0)
    width = rx_ref.shape[1]
    bw = width // RNN_BLOCKS

    @pl.when(i == 0)
    def _():
        xbuf[0:SUBLANES, :] = jnp.zeros((SUBLANES, width), F32)
        hc[...] = jnp.zeros(hc.shape, F32)

    xbuf[SUBLANES:SUBLANES + tt, :] = rx_ref[...]
    xc = cb_ref[...] + cw_ref[CONV_W - 1:CONV_W, :] * xbuf[SUBLANES:SUBLANES + tt, :]
    for d in range(1, CONV_W):
        xc = xc + cw_ref[CONV_W - 1 - d:CONV_W - d, :] * xbuf[SUBLANES - d:SUBLANES - d + tt, :]
    tail = xbuf[tt:tt + SUBLANES, :]
    xbuf[0:SUBLANES, :] = tail

    z = -lam_ref[...]
    softplus = jnp.maximum(z, 0.0) + jnp.log1p(jnp.exp(-jnp.abs(z)))
    xcb = xc.astype(BF16)
    for b in range(RNN_BLOCKS):
        sl = slice(b * bw, (b + 1) * bw)
        xb_ = xcb[:, sl]
        r = _sigmoid_tanh(jnp.dot(xb_, wa_ref[b], preferred_element_type=F32) + ba_ref[:, sl])
        ig = _sigmoid_tanh(jnp.dot(xb_, wx_ref[b], preferred_element_type=F32) + bx_ref[:, sl])
        log_a = (-RG_C) * r * softplus[:, sl]
        a = jnp.exp(log_a)
        th = jnp.tanh(log_a)
        one_m_a2 = (-2.0) * th / (1.0 - th)
        a_s[:, sl] = a
        u_s[:, sl] = jnp.sqrt(one_m_a2) * (ig * xc[:, sl])

    row = lax.broadcasted_iota(I32, (SUBLANES, width), 0)

    def body(g, h):
        r0 = pl.multiple_of(g * SUBLANES, SUBLANES)
        a = a_s[pl.ds(r0, SUBLANES), :]
        u = u_s[pl.ds(r0, SUBLANES), :]
        for s in (1, 2, 4):
            a_sh = pltpu.roll(a, s, 0)
            u_sh = pltpu.roll(u, s, 0)
            keep = row >= s
            u = jnp.where(keep, a * u_sh + u, u)
            a = jnp.where(keep, a * a_sh, a)
        hh = a * h + u
        u_s[pl.ds(r0, SUBLANES), :] = hh
        return hh[SUBLANES - 1:SUBLANES, :]

    h_last = lax.fori_loop(0, tt // SUBLANES, body, hc[0:1, :])
    hc[0:1, :] = h_last
    o_ref[...] = (u_s[...] * gate_ref[...]).astype(o_ref.dtype)


def _rglru(rx, gate, conv_w, conv_b, wa_bf, ba, wx_bf, bx, lam):
    t = rx.shape[0]
    width = conv_w.shape[1]
    bw = width // RNN_BLOCKS
    tt = 512
    vec = lambda: pl.BlockSpec((1, width), lambda i: (0, 0))
    wblk = lambda: pl.BlockSpec((RNN_BLOCKS, bw, bw), lambda i: (0, 0, 0))
    return pl.pallas_call(
        functools.partial(_rglru_kernel, tt=tt),
        grid=(t // tt,),
        in_specs=[pl.BlockSpec((tt, width), lambda i: (i, 0)),
                  pl.BlockSpec((tt, width), lambda i: (i, 0)),
                  pl.BlockSpec((CONV_W, width), lambda i: (0, 0)), vec(),
                  wblk(), vec(), wblk(), vec(), vec()],
        out_specs=pl.BlockSpec((tt, width), lambda i: (i, 0)),
        out_shape=jax.ShapeDtypeStruct((t, width), BF16),
        scratch_shapes=[pltpu.VMEM((tt + SUBLANES, width), F32),
                        pltpu.VMEM((tt, width), F32),
                        pltpu.VMEM((tt, width), F32),
                        pltpu.VMEM((SUBLANES, width), F32)],
        compiler_params=_cparams("arbitrary"),
        name="rglru",
    )(rx, gate, conv_w, conv_b, wa_bf, ba, wx_bf, bx, lam)


def _upmerge_kernel(ya_ref, yb_ref, sga_ref, sgb_ref, wa_ref, wr_ref, o_ref, wab_ref, wrb_ref):
    @pl.when(pl.program_id(1) == 0)
    def _():
        wab_ref[...] = wa_ref[...].astype(BF16)
        wrb_ref[...] = wr_ref[...].astype(BF16)

    up_a = jnp.dot(ya_ref[...], wab_ref[...], preferred_element_type=F32)
    up_b = jnp.dot(yb_ref[...], wrb_ref[...], preferred_element_type=F32)
    o_ref[...] = (sga_ref[...] * up_a + sgb_ref[...] * up_b).astype(o_ref.dtype)


def _upmerge(ya, yb, sg, wa, wr):
    t, aw = ya.shape
    rw = yb.shape[1]
    d = wa.shape[1]
    tm, tn = PROJ_TM, PROJ_TN
    nj = d // tn
    return pl.pallas_call(
        _upmerge_kernel,
        grid=(nj, t // tm),
        in_specs=[pl.BlockSpec((tm, aw), lambda j, i: (i, 0)),
                  pl.BlockSpec((tm, rw), lambda j, i: (i, 0)),
                  pl.BlockSpec((tm, tn), lambda j, i: (i, j)),
                  pl.BlockSpec((tm, tn), lambda j, i: (i, j + nj)),
                  pl.BlockSpec((aw, tn), lambda j, i: (0, j)),
                  pl.BlockSpec((rw, tn), lambda j, i: (0, j))],
        out_specs=pl.BlockSpec((tm, tn), lambda j, i: (i, j)),
        out_shape=jax.ShapeDtypeStruct((t, d), BF16),
        scratch_shapes=[pltpu.VMEM((aw, tn), BF16), pltpu.VMEM((rw, tn), BF16)],
        compiler_params=_cparams("arbitrary", "arbitrary"),
        name="upmerge",
    )(ya, yb, sg, sg, wa, wr)


def _mixer_out_kernel(mg_ref, x_ref, wo_ref, g1_ref, n2_ref, sc2_ref, sh2_ref, wrh_ref, brt_ref,
                      h1_ref, hnp_ref, lg_ref):
    proj = jnp.dot(mg_ref[...], wo_ref[...], preferred_element_type=F32)
    h1 = x_ref[...] + g1_ref[...] * proj
    h1_ref[...] = h1
    ms = jnp.mean(h1 * h1, axis=-1, keepdims=True)
    hn = h1 * lax.rsqrt(ms + EPS) * n2_ref[...]
    hn = hn * (1.0 + sc2_ref[...]) + sh2_ref[...]
    hb = hn.astype(BF16)
    hb32 = hb.astype(F32)
    hl = (hn - hb32).astype(BF16)
    ne = lg_ref.shape[1]
    hh = jnp.dot(hb, wrh_ref[...], preferred_element_type=F32)
    lh = jnp.dot(hl, wrh_ref[:, :ne], preferred_element_type=F32)
    lg_ref[...] = hh[:, :ne] + hh[:, ne:] + lh + brt_ref[...]
    half = hn.shape[1] // 2
    lo = pltpu.bitcast(hb32[:, :half], U32) >> 16
    hi = pltpu.bitcast(hb32[:, half:], U32) & jnp.uint32(0xFFFF0000)
    packed = hi | lo
    tm = packed.shape[0]
    for s in range(PACK_TILES):
        hnp_ref[pl.ds(s, tm, stride=PACK_TILES), :] = packed[:, s * LANES:(s + 1) * LANES]


def _mixer_out(merged, x2, wo_bf, g1, n2g, sc2, sh2, w_router, b_router):
    t, d = x2.shape
    ne = w_router.shape[1]
    tm = 512
    w_hi = w_router.astype(BF16)
    w_lo = (w_router - w_hi.astype(F32)).astype(BF16)
    w_split = jnp.concatenate([w_hi, w_lo], axis=1)
    const = lambda shape: pl.BlockSpec(shape, lambda i: (0,) * len(shape), pipeline_mode=pl.Buffered(1))
    return pl.pallas_call(
        _mixer_out_kernel,
        grid=(t // tm,),
        in_specs=[pl.BlockSpec((tm, d), lambda i: (i, 0)),
                  pl.BlockSpec((tm, d), lambda i: (i, 0)),
                  const((d, d)),
                  const((1, d)), const((1, d)), const((1, d)), const((1, d)),
                  const((d, 2 * ne)), const((1, ne))],
        out_specs=[pl.BlockSpec((tm, d), lambda i: (i, 0)),
                   pl.BlockSpec((tm * PACK_TILES, LANES), lambda i: (i, 0)),
                   pl.BlockSpec((tm, ne), lambda i: (i, 0))],
        out_shape=[jax.ShapeDtypeStruct((t, d), F32),
                   jax.ShapeDtypeStruct((t * PACK_TILES, LANES), U32),
                   jax.ShapeDtypeStruct((t, ne), F32)],
        compiler_params=_cparams("arbitrary"),
        name="mixer_out",
    )(merged, x2, wo_bf, g1, n2g, sc2, sh2, w_split, b_router)


def _route_kernel(lg_ref, idx_ref, w_ref, pos_ref, cnt_ref, carry, *, rt):
    i = pl.program_id(0)
    ne = lg_ref.shape[1]

    @pl.when(i == 0)
    def _():
        carry[...] = jnp.zeros(carry.shape, F32)

    l = lg_ref[...]
    lane = lax.broadcasted_iota(I32, (rt, ne), 1).astype(F32)
    vals, idxs, hots = [], [], []
    for _ in range(TOP_K):
        m = jnp.max(l, axis=-1, keepdims=True)
        ik = jnp.min(jnp.where(l == m, lane, float(ne)), axis=-1, keepdims=True)
        hot = lane == ik
        vals.append(m)
        idxs.append(ik)
        hots.append(hot)
        l = jnp.where(hot, -jnp.inf, l)
    es = [jnp.exp(v - vals[0]) for v in vals]
    tot = es[0] + es[1] + es[2] + es[3]
    sel = jnp.zeros((rt, ne), F32)
    for hot in hots:
        sel = sel + hot.astype(F32)
    rr = lax.broadcasted_iota(I32, (rt, rt), 0)
    cc = lax.broadcasted_iota(I32, (rt, rt), 1)
    tri = (cc < rr).astype(BF16)
    before = jnp.dot(tri, sel.astype(BF16), preferred_element_type=F32) + carry[0:1, :]
    out_lane = lax.broadcasted_iota(I32, (rt, LANES), 1)
    idx_o = jnp.zeros((rt, LANES), F32)
    w_o = jnp.zeros((rt, LANES), F32)
    pos_o = jnp.zeros((rt, LANES), F32)
    for k in range(TOP_K):
        pk = jnp.sum(jnp.where(hots[k], before, 0.0), axis=-1, keepdims=True)
        idx_o = jnp.where(out_lane == k, idxs[k], idx_o)
        w_o = jnp.where(out_lane == k, es[k] / tot, w_o)
        pos_o = jnp.where(out_lane == k, pk, pos_o)
    idx_ref[...] = idx_o.astype(I32)
    w_ref[...] = w_o
    pos_ref[...] = pos_o.astype(I32)
    carry[0:1, :] = carry[0:1, :] + jnp.sum(sel, axis=0, keepdims=True)
    cnt_ref[...] = carry[...]


def _route(logits):
    t, ne = logits.shape
    rt = 512
    wide = lambda dt: jax.ShapeDtypeStruct((t, LANES), dt)
    return pl.pallas_call(
        functools.partial(_route_kernel, rt=rt),
        grid=(t // rt,),
        in_specs=[pl.BlockSpec((rt, ne), lambda i: (i, 0))],
        out_specs=[pl.BlockSpec((rt, LANES), lambda i: (i, 0)),
                   pl.BlockSpec((rt, LANES), lambda i: (i, 0)),
                   pl.BlockSpec((rt, LANES), lambda i: (i, 0)),
                   pl.BlockSpec((SUBLANES, ne), lambda i: (0, 0))],
        out_shape=[wide(I32), wide(F32), wide(I32), jax.ShapeDtypeStruct((SUBLANES, ne), F32)],
        scratch_shapes=[pltpu.VMEM((SUBLANES, ne), F32)],
        compiler_params=_cparams("arbitrary"),
        name="route",
    )(logits)


def _unpack_rows(x_ref):
    los, his = [], []
    for s in range(PACK_TILES):
        p = x_ref[pl.ds(s, ROW_BLOCK, stride=PACK_TILES), :]
        los.append(pltpu.bitcast(p << 16, F32).astype(BF16))
        his.append(pltpu.bitcast(p & jnp.uint32(0xFFFF0000), F32).astype(BF16))
    return jnp.concatenate(los + his, axis=1)


def _first_of_expert(be_ref, m, na):
    mm = jnp.minimum(m, na - 1)
    prev = be_ref[jnp.maximum(mm - 1, 0)]
    return (mm == 0) | (be_ref[mm] != prev)


def _weight_stream(be_ref, na_ref, end_ref, n_tiles, copies, convert):
    n = pl.program_id(0)
    m = pl.program_id(1)
    na = na_ref[0]
    active = m < na

    @pl.when((n == 0) & (m == 0))
    def _():
        for cp in copies(be_ref[0], 0):
            cp.start(priority=1)

    @pl.when(active & _first_of_expert(be_ref, m, na))
    def _():
        e = be_ref[m]
        for cp in copies(e, n):
            cp.wait()
        convert()
        run_end = end_ref[e]
        wrap = run_end >= na
        n2 = jnp.where(wrap, n + 1, n)
        m2 = jnp.where(wrap, 0, run_end)

        @pl.when(n2 < n_tiles)
        def _():
            for cp in copies(be_ref[m2], n2):
                cp.start(priority=1)

    return active


def _gu_weight_stream(be_ref, na_ref, end_ref, w_hbm, stg_g, stg_u, wgb, wub, sem, *, tn, col, f):
    def copies(e, n):
        return [pltpu.make_async_copy(w_hbm.at[e, :, pl.ds(col * tn, tn)], stg_g, sem.at[0]),
                pltpu.make_async_copy(w_hbm.at[e, :, pl.ds(f + col * tn, tn)], stg_u, sem.at[1])]

    def convert():
        wgb[...] = stg_g[...].astype(BF16)
        wub[...] = stg_u[...].astype(BF16)

    return _weight_stream(be_ref, na_ref, end_ref, 1, copies, convert)


def _gu_compute(x, wgb, wub, bg_ref, bu_ref, o_ref):
    g = jnp.dot(x, wgb[...], preferred_element_type=F32) + bg_ref[...]
    u = jnp.dot(x, wub[...], preferred_element_type=F32) + bu_ref[...]
    g = jnp.minimum(g, SWIGLU_LIMIT)
    u = jnp.clip(u, -SWIGLU_LIMIT, SWIGLU_LIMIT)
    glu = g * _sigmoid_tanh(SWIGLU_ALPHA * g)
    o_ref[...] = ((u + 1.0) * glu).astype(o_ref.dtype)


def _moe_gu_gather_kernel(be_ref, na_ref, end_ref, tok0_ref, tok1_ref, tok2_ref, hn_hbm, w_hbm, bg_ref,
                          bu_ref, o_ref, xb_ref, stg_g, stg_u, wgb, wub, xbuf0, xbuf1, xbuf2, sem, gsem,
                          *, tn, col, f):
    m = pl.program_id(1)
    na = na_ref[0]
    slot = m % MOE_BUFS
    xbufs = (xbuf0, xbuf1, xbuf2)

    def gather(tok_ref, s):
        for r in range(ROW_BLOCK):
            src = pl.multiple_of(tok_ref[0, 0, r], PACK_TILES)
            pltpu.make_async_copy(hn_hbm.at[pl.ds(src, PACK_TILES)],
                                  xbufs[s].at[pl.ds(r * PACK_TILES, PACK_TILES)], gsem.at[s]
                                  ).start(priority=r % 2)

    def gather_wait(s):
        pltpu.make_async_copy(hn_hbm.at[pl.ds(0, ROW_BLOCK * PACK_TILES)], xbufs[s], gsem.at[s]).wait()

    @pl.when(m == 0)
    def _():
        gather(tok0_ref, 0)
        gather(tok1_ref, 1)

    active = _gu_weight_stream(be_ref, na_ref, end_ref, w_hbm, stg_g, stg_u, wgb, wub, sem,
                               tn=tn, col=col, f=f)

    def step(s):
        gather_wait(s)
        gather(tok2_ref, (s + 2) % MOE_BUFS)
        xb_ref[...] = xbufs[s][...]
        _gu_compute(_unpack_rows(xbufs[s]), wgb, wub, bg_ref, bu_ref, o_ref)

    for s in range(MOE_BUFS):
        @pl.when(active & (slot == s))
        def _(s=s):
            step(s)

        @pl.when(active & (m == na - 1) & (slot == s))
        def _(s=s):
            gather_wait((s + 1) % MOE_BUFS)
            gather_wait((s + 2) % MOE_BUFS)

    @pl.when(jnp.logical_not(active))
    def _():
        o_ref[...] = jnp.zeros(o_ref.shape, o_ref.dtype)
        xb_ref[...] = jnp.zeros(xb_ref.shape, xb_ref.dtype)


def _moe_gu_dense_kernel(be_ref, na_ref, end_ref, x_ref, w_hbm, bg_ref, bu_ref, o_ref,
                         stg_g, stg_u, wgb, wub, sem, *, tn, col, f):
    active = _gu_weight_stream(be_ref, na_ref, end_ref, w_hbm, stg_g, stg_u, wgb, wub, sem,
                               tn=tn, col=col, f=f)

    @pl.when(active)
    def _():
        _gu_compute(_unpack_rows(x_ref), wgb, wub, bg_ref, bu_ref, o_ref)

    @pl.when(jnp.logical_not(active))
    def _():
        o_ref[...] = jnp.zeros(o_ref.shape, o_ref.dtype)


def _moe_gu(hn_packed, src_tok, w_gu, b_gu, blk_e, n_active, run_end):
    nb = src_tok.shape[0]
    ne, d, f2 = w_gu.shape
    f = f2 // 2
    tn = MOE_GU_TN
    mc = lambda m, na: jnp.minimum(m, na[0] - 1)
    b3 = b_gu.reshape(ne, 1, f2)
    weight_scratch = [pltpu.VMEM((d, tn), F32), pltpu.VMEM((d, tn), F32),
                      pltpu.VMEM((d, tn), BF16), pltpu.VMEM((d, tn), BF16)]
    xrows = ROW_BLOCK * PACK_TILES
    act_shape = jax.ShapeDtypeStruct((nb * ROW_BLOCK, tn), BF16)

    def bias_specs(col):
        return [pl.BlockSpec((None, 1, tn), lambda n, m, be, na, en: (be[mc(m, na)], 0, col)),
                pl.BlockSpec((None, 1, tn), lambda n, m, be, na, en: (be[mc(m, na)], 0, col + f // tn))]

    tok_blk = lambda f_: pl.BlockSpec((1, 1, ROW_BLOCK), f_, memory_space=pltpu.SMEM)
    act0, xb = pl.pallas_call(
        functools.partial(_moe_gu_gather_kernel, tn=tn, col=0, f=f),
        grid_spec=pltpu.PrefetchScalarGridSpec(
            num_scalar_prefetch=3,
            grid=(1, nb),
            in_specs=[tok_blk(lambda n, m, be, na, en: (0, 0, 0)),
                      tok_blk(lambda n, m, be, na, en: (1 % na[0], 0, 0)),
                      tok_blk(lambda n, m, be, na, en: ((m + 2) % na[0], 0, 0)),
                      pl.BlockSpec(memory_space=pl.ANY),
                      pl.BlockSpec(memory_space=pl.ANY)] + bias_specs(0),
            out_specs=[pl.BlockSpec((ROW_BLOCK, tn), lambda n, m, be, na, en: (m, 0)),
                       pl.BlockSpec((xrows, LANES), lambda n, m, be, na, en: (m, 0))],
            scratch_shapes=weight_scratch + [pltpu.VMEM((xrows, LANES), U32)] * MOE_BUFS
                           + [pltpu.SemaphoreType.DMA((2,)), pltpu.SemaphoreType.DMA((MOE_BUFS,))],
        ),
        out_shape=[act_shape, jax.ShapeDtypeStruct((nb * xrows, LANES), U32)],
        compiler_params=_cparams("arbitrary", "arbitrary"),
        name="moe_gu0",
    )(blk_e, n_active, run_end, src_tok, src_tok, src_tok, hn_packed, w_gu, b3, b3)

    acts = [act0]
    for col in range(1, f // tn):
        acts.append(pl.pallas_call(
            functools.partial(_moe_gu_dense_kernel, tn=tn, col=col, f=f),
            grid_spec=pltpu.PrefetchScalarGridSpec(
                num_scalar_prefetch=3,
                grid=(1, nb),
                in_specs=[pl.BlockSpec((xrows, LANES), lambda n, m, be, na, en: (mc(m, na), 0)),
                          pl.BlockSpec(memory_space=pl.ANY)] + bias_specs(col),
                out_specs=pl.BlockSpec((ROW_BLOCK, tn), lambda n, m, be, na, en: (m, 0)),
                scratch_shapes=weight_scratch + [pltpu.SemaphoreType.DMA((2,))],
            ),
            out_shape=act_shape,
            compiler_params=_cparams("arbitrary", "arbitrary"),
            name="moe_gu%d" % col,
        )(blk_e, n_active, run_end, xb, w_gu, b3, b3))
    return acts


def _moe_down_kernel(be_ref, na_ref, end_ref, slot_ref, *refs, n_act):
    a_refs = refs[:n_act]
    w_hbm, b_ref, y_hbm, stg, wb, ybuf0, ybuf1, ybuf2, sem, ssem = refs[n_act:]
    m = pl.program_id(1)
    na = na_ref[0]
    p = m % MOE_BUFS
    ybufs = (ybuf0, ybuf1, ybuf2)

    def scatter(s):
        for r in range(ROW_BLOCK):
            pltpu.make_async_copy(ybufs[s].at[pl.ds(r, 1)], y_hbm.at[pl.ds(slot_ref[0, 0, r], 1)],
                                  ssem.at[s]).start()

    def scatter_wait(s):
        pltpu.make_async_copy(ybufs[s], y_hbm.at[pl.ds(0, ROW_BLOCK)], ssem.at[s]).wait()

    @pl.when(m == 0)
    def _():
        ybuf2[...] = jnp.zeros(ybuf2.shape, ybuf2.dtype)

    for s in range(MOE_BUFS):
        @pl.when((m >= 2) & (m <= na + 2) & (p == s))
        def _(s=s):
            scatter_wait(s)

    def copies(e, n):
        return [pltpu.make_async_copy(w_hbm.at[e], stg, sem.at[0])]

    def convert():
        wb[...] = stg[...].astype(BF16)

    active = _weight_stream(be_ref, na_ref, end_ref, 1, copies, convert)

    for s in range(MOE_BUFS):
        @pl.when(active & (p == s))
        def _(s=s):
            scatter((s + MOE_BUFS - 1) % MOE_BUFS)
            a = jnp.concatenate([a_ref[...] for a_ref in a_refs], axis=1)
            ybufs[s][...] = jnp.dot(a, wb[...], preferred_element_type=F32) + b_ref[...]

        @pl.when((m == na) & (p == s))
        def _(s=s):
            scatter((s + MOE_BUFS - 1) % MOE_BUFS)


def _moe_down(acts, slot_prev, w_down, b_down, blk_e, n_active, run_end, n_slots):
    n_rows, tf = acts[0].shape
    ne, f, d = w_down.shape
    nb = n_rows // ROW_BLOCK
    mc = lambda m, na: jnp.minimum(m, na[0] - 1)
    grid_spec = pltpu.PrefetchScalarGridSpec(
        num_scalar_prefetch=3,
        grid=(1, nb + MOE_BUFS),
        in_specs=[pl.BlockSpec((1, 1, ROW_BLOCK), lambda n, m, be, na, en: (jnp.minimum(m, nb), 0, 0),
                               memory_space=pltpu.SMEM),
                  *[pl.BlockSpec((ROW_BLOCK, tf), lambda n, m, be, na, en: (mc(m, na), 0))] * len(acts),
                  pl.BlockSpec(memory_space=pl.ANY),
                  pl.BlockSpec((None, 1, d), lambda n, m, be, na, en: (be[mc(m, na)], 0, 0))],
        out_specs=pl.BlockSpec(memory_space=pl.ANY),
        scratch_shapes=[pltpu.VMEM((f, d), F32), pltpu.VMEM((f, d), BF16),
                        *[pltpu.VMEM((ROW_BLOCK, d), F32)] * MOE_BUFS,
                        pltpu.SemaphoreType.DMA((1,)), pltpu.SemaphoreType.DMA((MOE_BUFS,))],
    )
    return pl.pallas_call(
        functools.partial(_moe_down_kernel, n_act=len(acts)),
        grid_spec=grid_spec,
        out_shape=jax.ShapeDtypeStruct((n_slots + ROW_BLOCK, d), F32),
        compiler_params=_cparams("arbitrary", "arbitrary"),
        name="moe_down",
    )(blk_e, n_active, run_end, slot_prev, *acts, w_down, b_down.reshape(ne, 1, d))


def _combine_kernel(w_ref, h1_ref, g2_ref, *refs):
    y_refs, o_ref = refs[:TOP_K], refs[TOP_K]
    w = w_ref[...]
    acc = y_refs[0][...] * w[:, 0:1]
    for k in range(1, TOP_K):
        acc = acc + y_refs[k][...] * w[:, k:k + 1]
    o_ref[...] = h1_ref[...] + g2_ref[...] * acc


def _combine(top_w, h1, g2, y_slots):
    t, d = h1.shape
    tm = 256
    nt = t // tm
    return pl.pallas_call(
        _combine_kernel,
        grid=(nt,),
        in_specs=[pl.BlockSpec((tm, LANES), lambda i: (i, 0)),
                  pl.BlockSpec((tm, d), lambda i: (i, 0)),
                  pl.BlockSpec((1, d), lambda i: (0, 0))]
                 + [pl.BlockSpec((tm, d), lambda i, k=k: (k * nt + i, 0)) for k in range(TOP_K)],
        out_specs=pl.BlockSpec((tm, d), lambda i: (i, 0)),
        out_shape=jax.ShapeDtypeStruct((t, d), F32),
        compiler_params=_cparams("arbitrary"),
        name="combine",
    )(top_w, h1, g2, *([y_slots] * TOP_K))


def kernel(x, c, ada_w, ada_b, norm1_g, w_in, q_norm_g, k_norm_g, rel_bias, conv_w, conv_b,
           rg_a_w, rg_a_b, rg_x_w, rg_x_b, rg_lambda, w_attn_up, w_rnn_up, w_out, norm2_g,
           w_router, b_router, w_gu, b_gu, w_down, b_down):
    b, s, d = x.shape
    assert b == 1 and ada_w.shape[0] == 1, "single batch row, single layer"
    t = s
    x2 = x.reshape(t, d)
    aw = ATTN_HEADS * HEAD_DIM
    rw = conv_w.shape[2]

    mod = _ada(c, ada_w[0], ada_b[0])
    sh1, sc1, g1, sh2, sc2, g2 = [mod[:, k * d:(k + 1) * d] for k in range(6)]

    w_in0 = w_in[0]
    qg = q_norm_g[0].reshape(1, HEAD_DIM)
    kg = k_norm_g[0].reshape(1, HEAD_DIM)
    n1g = norm1_g[0].reshape(1, d)
    xn = _norm_mod(x2, n1g, sc1, sh1)
    q = _proj(xn, w_in0, qg, col0=0, width=aw, kind="q", out_dtype=BF16, name="proj_q")
    k = _proj(xn, w_in0, kg, col0=aw, width=aw, kind="k", out_dtype=BF16, name="proj_k")
    v = _proj(xn, w_in0, kg, col0=2 * aw, width=aw, kind="id", out_dtype=BF16, name="proj_v")
    rx = _proj(xn, w_in0, kg, col0=3 * aw, width=rw, kind="id", out_dtype=F32, name="proj_rx")
    gate = _proj(xn, w_in0, kg, col0=3 * aw + rw, width=rw, kind="gelu", out_dtype=F32, name="proj_gate")
    sg = _proj(xn, w_in0, kg, col0=3 * aw + 2 * rw, width=2 * d, kind="sigmoid", out_dtype=F32,
               name="proj_sg")

    ya = _attn(q, k, v, _attn_distance_table(rel_bias[0]))
    yb = _rglru(rx, gate, conv_w[0], conv_b[0].reshape(1, rw),
                rg_a_w[0].astype(BF16), rg_a_b[0].reshape(1, rw),
                rg_x_w[0].astype(BF16), rg_x_b[0].reshape(1, rw), rg_lambda[0].reshape(1, rw))

    merged = _upmerge(ya, yb, sg, w_attn_up[0], w_rnn_up[0])
    h1, hn_packed, logits = _mixer_out(
        merged, x2, w_out[0].astype(BF16), g1, norm2_g[0].reshape(1, d), sc2, sh2,
        w_router[0], b_router[0].reshape(1, N_EXPERTS))

    idx_w, topw_w, pos_w, cnt = _route(logits)
    idx = idx_w[:, :TOP_K]
    pos = pos_w[:, :TOP_K]
    counts = cnt[0].astype(I32)
    padded = (counts + ROW_BLOCK - 1) // ROW_BLOCK * ROW_BLOCK
    pend = jnp.cumsum(padded)
    pstart = pend - padded
    experts = jnp.arange(N_EXPERTS, dtype=I32)
    dest = (pos + jnp.sum(jnp.where(idx[:, :, None] == experts, pstart, 0), axis=-1)).reshape(-1)
    n_blocks = (t * TOP_K) // ROW_BLOCK + N_EXPERTS
    n_rows = n_blocks * ROW_BLOCK
    blk_row0 = jnp.arange(n_blocks, dtype=I32) * ROW_BLOCK
    blk_e = jnp.minimum(jnp.sum((pend[None, :] <= blk_row0[:, None]).astype(I32), axis=1), N_EXPERTS - 1)
    n_active = (pend[-1:] // ROW_BLOCK).astype(I32)
    run_end = (pend // ROW_BLOCK).astype(I32)
    n_slots = t * TOP_K
    slot_ids = (jnp.arange(t, dtype=I32)[:, None] + jnp.arange(TOP_K, dtype=I32)[None, :] * t).reshape(-1)
    spare = n_slots + jnp.arange(ROW_BLOCK, dtype=I32)
    row_slot = jnp.tile(spare, n_blocks).at[dest].set(slot_ids, unique_indices=True, mode="promise_in_bounds")
    src_tok = (jnp.where(row_slot < n_slots, row_slot % t, 0) * PACK_TILES).reshape(n_blocks, 1, ROW_BLOCK)
    slot_prev = jnp.concatenate([spare, row_slot]).reshape(n_blocks + 1, 1, ROW_BLOCK)

    acts = _moe_gu(hn_packed, src_tok, w_gu[0], b_gu[0], blk_e, n_active, run_end)
    y_slots = _moe_down(acts, slot_prev, w_down[0], b_down[0], blk_e, n_active, run_end, n_slots)
    out = _combine(topw_w, h1, g2, y_slots)
    return out.reshape(b, s, d)
```

```python
import functools
import math

import jax
import jax.numpy as jnp
from jax import lax
from jax.experimental import pallas as pl
from jax.experimental.pallas import tpu as pltpu

F32 = jnp.float32
BF16 = jnp.bfloat16
U32 = jnp.uint32
I32 = jnp.int32

CHUNK = 64
LEFT_CHUNKS = 8
ATTN_HEADS = 8
HEAD_DIM = 128
REL_MAX = 256
REL_MIN = -(CHUNK - 1)
RNN_BLOCKS = 8
CONV_W = 4
RG_C = 8.0
N_EXPERTS = 32
TOP_K = 4
SWIGLU_LIMIT = 7.0
SWIGLU_ALPHA = 1.702
EPS = 1e-6
NEG_BIG = -1e30

LANES = 128
SUBLANES = 8
VMEM_LIMIT = 56 * 1024 * 1024

ROW_BLOCK = 512
MOE_BUFS = 3
MOE_GU_TN = 1024
PACK_TILES = 8
PROJ_TM = 1024
PROJ_TN = 1024
ADA_TN = 1024
NORM_TM = 512
ATTN_QB = 4 * CHUNK
ATTN_KB = ATTN_QB + LEFT_CHUNKS * CHUNK
ATTN_TAB = ATTN_KB + ATTN_QB
CHUNK_SHIFT = CHUNK.bit_length() - 1
assert 1 << CHUNK_SHIFT == CHUNK


def _cparams(*sem):
    return pltpu.CompilerParams(dimension_semantics=sem, vmem_limit_bytes=VMEM_LIMIT)


def _ada_kernel(c_ref, w_ref, b_ref, o_ref):
    c = c_ref[...]
    ca = c * jax.nn.sigmoid(c)
    row = jnp.sum(ca * w_ref[...], axis=0, keepdims=True) + b_ref[...]
    o_ref[...] = jnp.broadcast_to(row, o_ref.shape)


def _ada(c, ada_w, ada_b):
    d, n = ada_w.shape
    tn = ADA_TN
    out = pl.pallas_call(
        _ada_kernel,
        grid=(n // tn,),
        in_specs=[pl.BlockSpec((d, 1), lambda j: (0, 0)),
                  pl.BlockSpec((d, tn), lambda j: (0, j)),
                  pl.BlockSpec((1, tn), lambda j: (0, j))],
        out_specs=pl.BlockSpec((SUBLANES, tn), lambda j: (0, j)),
        out_shape=jax.ShapeDtypeStruct((SUBLANES, n), F32),
        compiler_params=_cparams("arbitrary"),
        name="ada",
    )(c.reshape(d, 1), ada_w, ada_b.reshape(1, n))
    return out[0:1]


def _sigmoid_tanh(x):
    return 0.5 * jnp.tanh(0.5 * x) + 0.5


def _gelu_tanh(x):
    return 0.5 * x * (1.0 + jnp.tanh(math.sqrt(2.0 / math.pi) * (x + 0.044715 * (x * x * x))))


def _head_rms(a, g):
    outs = []
    for hh in range(a.shape[1] // HEAD_DIM):
        s = a[:, hh * HEAD_DIM:(hh + 1) * HEAD_DIM]
        ms = jnp.mean(s * s, axis=-1, keepdims=True)
        outs.append(s * lax.rsqrt(ms + EPS) * g)
    return jnp.concatenate(outs, axis=1)


def _norm_q_kernel(x_ref, g_ref, sc_ref, sh_ref, w_ref, hg_ref, xn_ref, q_ref, wb_ref):
    @pl.when(pl.program_id(0) == 0)
    def _():
        wb_ref[...] = w_ref[...].astype(BF16)

    x = x_ref[...]
    ms = jnp.mean(x * x, axis=-1, keepdims=True)
    xn = x * lax.rsqrt(ms + EPS) * g_ref[...]
    xn = (xn * (1.0 + sc_ref[...]) + sh_ref[...]).astype(BF16)
    xn_ref[...] = xn
    acc = jnp.dot(xn, wb_ref[...], preferred_element_type=F32)
    q_ref[...] = (_head_rms(acc, hg_ref[...]) * (1.0 / math.sqrt(HEAD_DIM))).astype(q_ref.dtype)


def _norm_q(x2, g, sc, sh, w, q_gain, width):
    t, d = x2.shape
    tm = NORM_TM
    vec = lambda: pl.BlockSpec((1, d), lambda i: (0, 0))
    return pl.pallas_call(
        _norm_q_kernel,
        grid=(t // tm,),
        in_specs=[pl.BlockSpec((tm, d), lambda i: (i, 0)), vec(), vec(), vec(),
                  pl.BlockSpec((d, width), lambda i: (0, 0), pipeline_mode=pl.Buffered(1)),
                  pl.BlockSpec((1, HEAD_DIM), lambda i: (0, 0))],
        out_specs=[pl.BlockSpec((tm, d), lambda i: (i, 0)),
                   pl.BlockSpec((tm, width), lambda i: (i, 0))],
        out_shape=[jax.ShapeDtypeStruct((t, d), BF16), jax.ShapeDtypeStruct((t, width), BF16)],
        scratch_shapes=[pltpu.VMEM((d, width), BF16)],
        compiler_params=_cparams("arbitrary"),
        name="norm1_q",
    )(x2, g, sc, sh, w, q_gain)


def _proj_kernel(x_ref, w_ref, hg_ref, o_ref, wb_ref, *, kind):
    @pl.when(pl.program_id(1) == 0)
    def _():
        wb_ref[...] = w_ref[...].astype(BF16)

    acc = jnp.dot(x_ref[...], wb_ref[...], preferred_element_type=F32)
    if kind == "k":
        r = _head_rms(acc, hg_ref[...])
    elif kind == "gelu":
        r = _gelu_tanh(acc)
    elif kind == "sigmoid":
        r = _sigmoid_tanh(acc)
    else:
        r = acc
    o_ref[...] = r.astype(o_ref.dtype)


def _proj(xn, w, head_gain, *, col0, width, kind, out_dtype, name):
    t, d = xn.shape
    tm, tn = PROJ_TM, PROJ_TN
    j0 = col0 // tn
    return pl.pallas_call(
        functools.partial(_proj_kernel, kind=kind),
        grid=(width // tn, t // tm),
        in_specs=[pl.BlockSpec((tm, d), lambda j, i: (i, 0)),
                  pl.BlockSpec((d, tn), lambda j, i: (0, j + j0)),
                  pl.BlockSpec((1, HEAD_DIM), lambda j, i: (0, 0))],
        out_specs=pl.BlockSpec((tm, tn), lambda j, i: (i, j)),
        out_shape=jax.ShapeDtypeStruct((t, width), out_dtype),
        scratch_shapes=[pltpu.VMEM((d, tn), BF16)],
        compiler_params=_cparams("arbitrary", "arbitrary"),
        name=name,
    )(xn, w, head_gain)


def _attn_kernel(q_ref, k0_ref, k1_ref, k2_ref, v0_ref, v1_ref, v2_ref, ftab_ref, o_ref, bias_ref):
    i = pl.program_id(0)
    col = lax.broadcasted_iota(I32, (ATTN_QB, ATTN_KB), 1)

    @pl.when(i == 0)
    def _():
        row = lax.broadcasted_iota(I32, (ATTN_QB, ATTN_KB), 0)
        cq = row >> CHUNK_SHIFT
        ck = col >> CHUNK_SHIFT
        visible = (ck >= cq) & (ck <= cq + LEFT_CHUNKS)
        for h in range(ATTN_HEADS):
            g = jnp.broadcast_to(ftab_ref[h:h + 1, :], (ATTN_QB, ATTN_TAB))
            y = pltpu.roll(g, 0, 1, stride=1, stride_axis=0)
            bias_ref[h] = jnp.where(visible, y[:, :ATTN_KB], NEG_BIG)

    def heads(mask_left_pad):
        in_seq = col >= (LEFT_CHUNKS * CHUNK - i * ATTN_QB)
        for h in range(ATTN_HEADS):
            sl = slice(h * HEAD_DIM, (h + 1) * HEAD_DIM)
            q = q_ref[:, sl]
            k = jnp.concatenate([k0_ref[:, sl], k1_ref[:, sl], k2_ref[:, sl]], axis=0)
            v = jnp.concatenate([v0_ref[:, sl], v1_ref[:, sl], v2_ref[:, sl]], axis=0)
            s = lax.dot_general(q, k, (((1,), (1,)), ((), ())), preferred_element_type=F32)
            s = s + bias_ref[h]
            if mask_left_pad:
                s = jnp.where(in_seq, s, NEG_BIG)
            m = jnp.max(s, axis=-1, keepdims=True)
            p = jnp.exp(s - m)
            l = jnp.sum(p, axis=-1, keepdims=True)
            o = jnp.dot(p.astype(BF16), v, preferred_element_type=F32) / l
            o_ref[:, sl] = o.astype(o_ref.dtype)

    first_blocks = (LEFT_CHUNKS * CHUNK) // ATTN_QB

    @pl.when(i < first_blocks)
    def _():
        heads(True)

    @pl.when(i >= first_blocks)
    def _():
        heads(False)


def _attn(q, k, v, ftab):
    t = q.shape[0]
    w = ATTN_HEADS * HEAD_DIM
    qb = ATTN_QB
    blk = lambda f: pl.BlockSpec((qb, w), f)
    return pl.pallas_call(
        _attn_kernel,
        grid=(t // qb,),
        in_specs=[blk(lambda i: (i, 0)),
                  blk(lambda i: (jnp.maximum(i - 2, 0), 0)), blk(lambda i: (jnp.maximum(i - 1, 0), 0)),
                  blk(lambda i: (i, 0)),
                  blk(lambda i: (jnp.maximum(i - 2, 0), 0)), blk(lambda i: (jnp.maximum(i - 1, 0), 0)),
                  blk(lambda i: (i, 0)),
                  pl.BlockSpec((ATTN_HEADS, ATTN_TAB), lambda i: (0, 0))],
        out_specs=blk(lambda i: (i, 0)),
        out_shape=jax.ShapeDtypeStruct((t, w), BF16),
        scratch_shapes=[pltpu.VMEM((ATTN_HEADS, qb, ATTN_KB), F32)],
        compiler_params=_cparams("arbitrary"),
        name="attn",
    )(q, k, k, k, v, v, v, ftab)


def _attn_distance_table(rel_bias):
    pad = LEFT_CHUNKS * CHUNK
    n_far = pad - REL_MAX + 1
    n_mid = REL_MAX - REL_MIN
    n_ahead = ATTN_KB - n_far - n_mid
    far = rel_bias[:, -1:]
    return jnp.concatenate([jnp.broadcast_to(far, (ATTN_HEADS, n_far)),
                            rel_bias[:, :n_mid][:, ::-1],
                            jnp.broadcast_to(rel_bias[:, :1], (ATTN_HEADS, n_ahead)),
                            jnp.broadcast_to(far, (ATTN_HEADS, ATTN_TAB - ATTN_KB))], axis=1)


def _rglru_kernel(rx_ref, gate_ref, cw_ref, cb_ref, wa_ref, ba_ref, wx_ref, bx_ref, lam_ref,
                  o_ref, xbuf, a_s, u_s, hc, *, tt):
    i = pl.program_id(0)
    width = rx_ref.shape[1]
    bw = width // RNN_BLOCKS

    @pl.when(i == 0)
    def _():
        xbuf[0:SUBLANES, :] = jnp.zeros((SUBLANES, width), F32)
        hc[...] = jnp.zeros(hc.shape, F32)

    xbuf[SUBLANES:SUBLANES + tt, :] = rx_ref[...]
    xc = cb_ref[...] + cw_ref[CONV_W - 1:CONV_W, :] * xbuf[SUBLANES:SUBLANES + tt, :]
    for d in range(1, CONV_W):
        xc = xc + cw_ref[CONV_W - 1 - d:CONV_W - d, :] * xbuf[SUBLANES - d:SUBLANES - d + tt, :]
    tail = xbuf[tt:tt + SUBLANES, :]
    xbuf[0:SUBLANES, :] = tail

    z = -lam_ref[...]
    softplus = jnp.maximum(z, 0.0) + jnp.log1p(jnp.exp(-jnp.abs(z)))
    xcb = xc.astype(BF16)
    for b in range(RNN_BLOCKS):
        sl = slice(b * bw, (b + 1) * bw)
        xb_ = xcb[:, sl]
        r = _sigmoid_tanh(jnp.dot(xb_, wa_ref[b], preferred_element_type=F32) + ba_ref[:, sl])
        ig = _sigmoid_tanh(jnp.dot(xb_, wx_ref[b], preferred_element_type=F32) + bx_ref[:, sl])
        log_a = (-RG_C) * r * softplus[:, sl]
        a = jnp.exp(log_a)
        th = jnp.tanh(log_a)
        one_m_a2 = (-2.0) * th / (1.0 - th)
        a_s[:, sl] = a
        u_s[:, sl] = jnp.sqrt(one_m_a2) * (ig * xc[:, sl])

    row = lax.broadcasted_iota(I32, (SUBLANES, width), 0)

    def body(g, h):
        r0 = pl.multiple_of(g * SUBLANES, SUBLANES)
        a = a_s[pl.ds(r0, SUBLANES), :]
        u = u_s[pl.ds(r0, SUBLANES), :]
        for s in (1, 2, 4):
            a_sh = pltpu.roll(a, s, 0)
            u_sh = pltpu.roll(u, s, 0)
            keep = row >= s
            u = jnp.where(keep, a * u_sh + u, u)
            a = jnp.where(keep, a * a_sh, a)
        hh = a * h + u
        u_s[pl.ds(r0, SUBLANES), :] = hh
        return hh[SUBLANES - 1:SUBLANES, :]

    h_last = lax.fori_loop(0, tt // SUBLANES, body, hc[0:1, :])
    hc[0:1, :] = h_last
    o_ref[...] = (u_s[...] * gate_ref[...]).astype(o_ref.dtype)


def _rglru(rx, gate, conv_w, conv_b, wa_bf, ba, wx_bf, bx, lam):
    t = rx.shape[0]
    width = conv_w.shape[1]
    bw = width // RNN_BLOCKS
    tt = 512
    vec = lambda: pl.BlockSpec((1, width), lambda i: (0, 0))
    wblk = lambda: pl.BlockSpec((RNN_BLOCKS, bw, bw), lambda i: (0, 0, 0))
    return pl.pallas_call(
        functools.partial(_rglru_kernel, tt=tt),
        grid=(t // tt,),
        in_specs=[pl.BlockSpec((tt, width), lambda i: (i, 0)),
                  pl.BlockSpec((tt, width), lambda i: (i, 0)),
                  pl.BlockSpec((CONV_W, width), lambda i: (0, 0)), vec(),
                  wblk(), vec(), wblk(), vec(), vec()],
        out_specs=pl.BlockSpec((tt, width), lambda i: (i, 0)),
        out_shape=jax.ShapeDtypeStruct((t, width), BF16),
        scratch_shapes=[pltpu.VMEM((tt + SUBLANES, width), F32),
                        pltpu.VMEM((tt, width), F32),
                        pltpu.VMEM((tt, width), F32),
                        pltpu.VMEM((SUBLANES, width), F32)],
        compiler_params=_cparams("arbitrary"),
        name="rglru",
    )(rx, gate, conv_w, conv_b, wa_bf, ba, wx_bf, bx, lam)


def _upmerge_kernel(ya_ref, yb_ref, sga_ref, sgb_ref, wa_ref, wr_ref, o_ref, wab_ref, wrb_ref):
    @pl.when(pl.program_id(1) == 0)
    def _():
        wab_ref[...] = wa_ref[...].astype(BF16)
        wrb_ref[...] = wr_ref[...].astype(BF16)

    up_a = jnp.dot(ya_ref[...], wab_ref[...], preferred_element_type=F32)
    up_b = jnp.dot(yb_ref[...], wrb_ref[...], preferred_element_type=F32)
    o_ref[...] = (sga_ref[...] * up_a + sgb_ref[...] * up_b).astype(o_ref.dtype)


def _upmerge(ya, yb, sg, wa, wr):
    t, aw = ya.shape
    rw = yb.shape[1]
    d = wa.shape[1]
    tm, tn = PROJ_TM, PROJ_TN
    nj = d // tn
    return pl.pallas_call(
        _upmerge_kernel,
        grid=(nj, t // tm),
        in_specs=[pl.BlockSpec((tm, aw), lambda j, i: (i, 0)),
                  pl.BlockSpec((tm, rw), lambda j, i: (i, 0)),
                  pl.BlockSpec((tm, tn), lambda j, i: (i, j)),
                  pl.BlockSpec((tm, tn), lambda j, i: (i, j + nj)),
                  pl.BlockSpec((aw, tn), lambda j, i: (0, j)),
                  pl.BlockSpec((rw, tn), lambda j, i: (0, j))],
        out_specs=pl.BlockSpec((tm, tn), lambda j, i: (i, j)),
        out_shape=jax.ShapeDtypeStruct((t, d), BF16),
        scratch_shapes=[pltpu.VMEM((aw, tn), BF16), pltpu.VMEM((rw, tn), BF16)],
        compiler_params=_cparams("arbitrary", "arbitrary"),
        name="upmerge",
    )(ya, yb, sg, sg, wa, wr)


def _mixer_out_kernel(mg_ref, x_ref, wo_ref, g1_ref, n2_ref, sc2_ref, sh2_ref, wrh_ref, brt_ref,
                      h1_ref, hnp_ref, lg_ref):
    proj = jnp.dot(mg_ref[...], wo_ref[...], preferred_element_type=F32)
    h1 = x_ref[...] + g1_ref[...] * proj
    h1_ref[...] = h1
    ms = jnp.mean(h1 * h1, axis=-1, keepdims=True)
    hn = h1 * lax.rsqrt(ms + EPS) * n2_ref[...]
    hn = hn * (1.0 + sc2_ref[...]) + sh2_ref[...]
    hb = hn.astype(BF16)
    hb32 = hb.astype(F32)
    hl = (hn - hb32).astype(BF16)
    ne = lg_ref.shape[1]
    hh = jnp.dot(hb, wrh_ref[...], preferred_element_type=F32)
    lh = jnp.dot(hl, wrh_ref[:, :ne], preferred_element_type=F32)
    lg_ref[...] = hh[:, :ne] + hh[:, ne:] + lh + brt_ref[...]
    half = hn.shape[1] // 2
    lo = pltpu.bitcast(hb32[:, :half], U32) >> 16
    hi = pltpu.bitcast(hb32[:, half:], U32) & jnp.uint32(0xFFFF0000)
    packed = hi | lo
    tm = packed.shape[0]
    for s in range(PACK_TILES):
        hnp_ref[pl.ds(s, tm, stride=PACK_TILES), :] = packed[:, s * LANES:(s + 1) * LANES]


def _mixer_out(merged, x2, wo_bf, g1, n2g, sc2, sh2, w_router, b_router):
    t, d = x2.shape
    ne = w_router.shape[1]
    tm = 512
    w_hi = w_router.astype(BF16)
    w_lo = (w_router - w_hi.astype(F32)).astype(BF16)
    w_split = jnp.concatenate([w_hi, w_lo], axis=1)
    const = lambda shape: pl.BlockSpec(shape, lambda i: (0,) * len(shape), pipeline_mode=pl.Buffered(1))
    return pl.pallas_call(
        _mixer_out_kernel,
        grid=(t // tm,),
        in_specs=[pl.BlockSpec((tm, d), lambda i: (i, 0)),
                  pl.BlockSpec((tm, d), lambda i: (i, 0)),
                  const((d, d)),
                  const((1, d)), const((1, d)), const((1, d)), const((1, d)),
                  const((d, 2 * ne)), const((1, ne))],
        out_specs=[pl.BlockSpec((tm, d), lambda i: (i, 0)),
                   pl.BlockSpec((tm * PACK_TILES, LANES), lambda i: (i, 0)),
                   pl.BlockSpec((tm, ne), lambda i: (i, 0))],
        out_shape=[jax.ShapeDtypeStruct((t, d), F32),
                   jax.ShapeDtypeStruct((t * PACK_TILES, LANES), U32),
                   jax.ShapeDtypeStruct((t, ne), F32)],
        compiler_params=_cparams("arbitrary"),
        name="mixer_out",
    )(merged, x2, wo_bf, g1, n2g, sc2, sh2, w_split, b_router)


def _route_kernel(lg_ref, idx_ref, w_ref, pos_ref, cnt_ref, carry, *, rt):
    i = pl.program_id(0)
    ne = lg_ref.shape[1]

    @pl.when(i == 0)
    def _():
        carry[...] = jnp.zeros(carry.shape, F32)

    l = lg_ref[...]
    lane = lax.broadcasted_iota(I32, (rt, ne), 1).astype(F32)
    vals, idxs, hots = [], [], []
    for _ in range(TOP_K):
        m = jnp.max(l, axis=-1, keepdims=True)
        ik = jnp.min(jnp.where(l == m, lane, float(ne)), axis=-1, keepdims=True)
        hot = lane == ik
        vals.append(m)
        idxs.append(ik)
        hots.append(hot)
        l = jnp.where(hot, -jnp.inf, l)
    es = [jnp.exp(v - vals[0]) for v in vals]
    tot = es[0] + es[1] + es[2] + es[3]
    sel = jnp.zeros((rt, ne), F32)
    for hot in hots:
        sel = sel + hot.astype(F32)
    rr = lax.broadcasted_iota(I32, (rt, rt), 0)
    cc = lax.broadcasted_iota(I32, (rt, rt), 1)
    tri = (cc < rr).astype(BF16)
    before = jnp.dot(tri, sel.astype(BF16), preferred_element_type=F32) + carry[0:1, :]
    out_lane = lax.broadcasted_iota(I32, (rt, LANES), 1)
    idx_o = jnp.zeros((rt, LANES), F32)
    w_o = jnp.zeros((rt, LANES), F32)
    pos_o = jnp.zeros((rt, LANES), F32)
    for k in range(TOP_K):
        pk = jnp.sum(jnp.where(hots[k], before, 0.0), axis=-1, keepdims=True)
        idx_o = jnp.where(out_lane == k, idxs[k], idx_o)
        w_o = jnp.where(out_lane == k, es[k] / tot, w_o)
        pos_o = jnp.where(out_lane == k, pk, pos_o)
    idx_ref[...] = idx_o.astype(I32)
    w_ref[...] = w_o
    pos_ref[...] = pos_o.astype(I32)
    carry[0:1, :] = carry[0:1, :] + jnp.sum(sel, axis=0, keepdims=True)
    cnt_ref[...] = carry[...]


def _route(logits):
    t, ne = logits.shape
    rt = 512
    wide = lambda dt: jax.ShapeDtypeStruct((t, LANES), dt)
    return pl.pallas_call(
        functools.partial(_route_kernel, rt=rt),
        grid=(t // rt,),
        in_specs=[pl.BlockSpec((rt, ne), lambda i: (i, 0))],
        out_specs=[pl.BlockSpec((rt, LANES), lambda i: (i, 0)),
                   pl.BlockSpec((rt, LANES), lambda i: (i, 0)),
                   pl.BlockSpec((rt, LANES), lambda i: (i, 0)),
                   pl.BlockSpec((SUBLANES, ne), lambda i: (0, 0))],
        out_shape=[wide(I32), wide(F32), wide(I32), jax.ShapeDtypeStruct((SUBLANES, ne), F32)],
        scratch_shapes=[pltpu.VMEM((SUBLANES, ne), F32)],
        compiler_params=_cparams("arbitrary"),
        name="route",
    )(logits)


def _unpack_rows(x_ref):
    los, his = [], []
    for s in range(PACK_TILES):
        p = x_ref[pl.ds(s, ROW_BLOCK, stride=PACK_TILES), :]
        los.append(pltpu.bitcast(p << 16, F32).astype(BF16))
        his.append(pltpu.bitcast(p & jnp.uint32(0xFFFF0000), F32).astype(BF16))
    return jnp.concatenate(los + his, axis=1)


def _first_of_expert(be_ref, m, na):
    mm = jnp.minimum(m, na - 1)
    prev = be_ref[jnp.maximum(mm - 1, 0)]
    return (mm == 0) | (be_ref[mm] != prev)


def _weight_stream(be_ref, na_ref, end_ref, n_tiles, copies, convert):
    n = pl.program_id(0)
    m = pl.program_id(1)
    na = na_ref[0]
    active = m < na

    @pl.when((n == 0) & (m == 0))
    def _():
        for cp in copies(be_ref[0], 0):
            cp.start(priority=1)

    @pl.when(active & _first_of_expert(be_ref, m, na))
    def _():
        e = be_ref[m]
        for cp in copies(e, n):
            cp.wait()
        convert()
        run_end = end_ref[e]
        wrap = run_end >= na
        n2 = jnp.where(wrap, n + 1, n)
        m2 = jnp.where(wrap, 0, run_end)

        @pl.when(n2 < n_tiles)
        def _():
            for cp in copies(be_ref[m2], n2):
                cp.start(priority=1)

    return active


def _gu_weight_stream(be_ref, na_ref, end_ref, w_hbm, stg_g, stg_u, wgb, wub, sem, *, tn, col, f):
    def copies(e, n):
        return [pltpu.make_async_copy(w_hbm.at[e, :, pl.ds(col * tn, tn)], stg_g, sem.at[0]),
                pltpu.make_async_copy(w_hbm.at[e, :, pl.ds(f + col * tn, tn)], stg_u, sem.at[1])]

    def convert():
        wgb[...] = stg_g[...].astype(BF16)
        wub[...] = stg_u[...].astype(BF16)

    return _weight_stream(be_ref, na_ref, end_ref, 1, copies, convert)


def _gu_compute(x, wgb, wub, bg_ref, bu_ref, o_ref):
    g = jnp.dot(x, wgb[...], preferred_element_type=F32) + bg_ref[...]
    u = jnp.dot(x, wub[...], preferred_element_type=F32) + bu_ref[...]
    g = jnp.minimum(g, SWIGLU_LIMIT)
    u = jnp.clip(u, -SWIGLU_LIMIT, SWIGLU_LIMIT)
    glu = g * _sigmoid_tanh(SWIGLU_ALPHA * g)
    o_ref[...] = ((u + 1.0) * glu).astype(o_ref.dtype)


def _moe_gu_gather_kernel(be_ref, na_ref, end_ref, tok0_ref, tok1_ref, tok2_ref, hn_hbm, w_hbm, bg_ref,
                          bu_ref, o_ref, xb_ref, stg_g, stg_u, wgb, wub, xbuf0, xbuf1, xbuf2, sem, gsem,
                          *, tn, col, f):
    m = pl.program_id(1)
    na = na_ref[0]
    slot = m % MOE_BUFS
    xbufs = (xbuf0, xbuf1, xbuf2)

    def gather(tok_ref, s):
        for r in range(ROW_BLOCK):
            src = pl.multiple_of(tok_ref[0, 0, r], PACK_TILES)
            pltpu.make_async_copy(hn_hbm.at[pl.ds(src, PACK_TILES)],
                                  xbufs[s].at[pl.ds(r * PACK_TILES, PACK_TILES)], gsem.at[s]
                                  ).start(priority=r % 2)

    def gather_wait(s):
        pltpu.make_async_copy(hn_hbm.at[pl.ds(0, ROW_BLOCK * PACK_TILES)], xbufs[s], gsem.at[s]).wait()

    @pl.when(m == 0)
    def _():
        gather(tok0_ref, 0)
        gather(tok1_ref, 1)

    active = _gu_weight_stream(be_ref, na_ref, end_ref, w_hbm, stg_g, stg_u, wgb, wub, sem,
                               tn=tn, col=col, f=f)

    def step(s):
        gather_wait(s)
        gather(tok2_ref, (s + 2) % MOE_BUFS)
        xb_ref[...] = xbufs[s][...]
        _gu_compute(_unpack_rows(xbufs[s]), wgb, wub, bg_ref, bu_ref, o_ref)

    for s in range(MOE_BUFS):
        @pl.when(active & (slot == s))
        def _(s=s):
            step(s)

        @pl.when(active & (m == na - 1) & (slot == s))
        def _(s=s):
            gather_wait((s + 1) % MOE_BUFS)
            gather_wait((s + 2) % MOE_BUFS)

    @pl.when(jnp.logical_not(active))
    def _():
        o_ref[...] = jnp.zeros(o_ref.shape, o_ref.dtype)
        xb_ref[...] = jnp.zeros(xb_ref.shape, xb_ref.dtype)


def _moe_gu_dense_kernel(be_ref, na_ref, end_ref, x_ref, w_hbm, bg_ref, bu_ref, o_ref,
                         stg_g, stg_u, wgb, wub, sem, *, tn, col, f):
    active = _gu_weight_stream(be_ref, na_ref, end_ref, w_hbm, stg_g, stg_u, wgb, wub, sem,
                               tn=tn, col=col, f=f)

    @pl.when(active)
    def _():
        _gu_compute(_unpack_rows(x_ref), wgb, wub, bg_ref, bu_ref, o_ref)

    @pl.when(jnp.logical_not(active))
    def _():
        o_ref[...] = jnp.zeros(o_ref.shape, o_ref.dtype)


def _moe_gu(hn_packed, src_tok, w_gu, b_gu, blk_e, n_active, run_end):
    nb = src_tok.shape[0]
    ne, d, f2 = w_gu.shape
    f = f2 // 2
    tn = MOE_GU_TN
    mc = lambda m, na: jnp.minimum(m, na[0] - 1)
    b3 = b_gu.reshape(ne, 1, f2)
    weight_scratch = [pltpu.VMEM((d, tn), F32), pltpu.VMEM((d, tn), F32),
                      pltpu.VMEM((d, tn), BF16), pltpu.VMEM((d, tn), BF16)]
    xrows = ROW_BLOCK * PACK_TILES
    act_shape = jax.ShapeDtypeStruct((nb * ROW_BLOCK, tn), BF16)

    def bias_specs(col):
        return [pl.BlockSpec((None, 1, tn), lambda n, m, be, na, en: (be[mc(m, na)], 0, col)),
                pl.BlockSpec((None, 1, tn), lambda n, m, be, na, en: (be[mc(m, na)], 0, col + f // tn))]

    tok_blk = lambda f_: pl.BlockSpec((1, 1, ROW_BLOCK), f_, memory_space=pltpu.SMEM)
    act0, xb = pl.pallas_call(
        functools.partial(_moe_gu_gather_kernel, tn=tn, col=0, f=f),
        grid_spec=pltpu.PrefetchScalarGridSpec(
            num_scalar_prefetch=3,
            grid=(1, nb),
            in_specs=[tok_blk(lambda n, m, be, na, en: (0, 0, 0)),
                      tok_blk(lambda n, m, be, na, en: (1 % na[0], 0, 0)),
                      tok_blk(lambda n, m, be, na, en: ((m + 2) % na[0], 0, 0)),
                      pl.BlockSpec(memory_space=pl.ANY),
                      pl.BlockSpec(memory_space=pl.ANY)] + bias_specs(0),
            out_specs=[pl.BlockSpec((ROW_BLOCK, tn), lambda n, m, be, na, en: (m, 0)),
                       pl.BlockSpec((xrows, LANES), lambda n, m, be, na, en: (m, 0))],
            scratch_shapes=weight_scratch + [pltpu.VMEM((xrows, LANES), U32)] * MOE_BUFS
                           + [pltpu.SemaphoreType.DMA((2,)), pltpu.SemaphoreType.DMA((MOE_BUFS,))],
        ),
        out_shape=[act_shape, jax.ShapeDtypeStruct((nb * xrows, LANES), U32)],
        compiler_params=_cparams("arbitrary", "arbitrary"),
        name="moe_gu0",
    )(blk_e, n_active, run_end, src_tok, src_tok, src_tok, hn_packed, w_gu, b3, b3)

    acts = [act0]
    for col in range(1, f // tn):
        acts.append(pl.pallas_call(
            functools.partial(_moe_gu_dense_kernel, tn=tn, col=col, f=f),
            grid_spec=pltpu.PrefetchScalarGridSpec(
                num_scalar_prefetch=3,
                grid=(1, nb),
                in_specs=[pl.BlockSpec((xrows, LANES), lambda n, m, be, na, en: (mc(m, na), 0)),
                          pl.BlockSpec(memory_space=pl.ANY)] + bias_specs(col),
                out_specs=pl.BlockSpec((ROW_BLOCK, tn), lambda n, m, be, na, en: (m, 0)),
                scratch_shapes=weight_scratch + [pltpu.SemaphoreType.DMA((2,))],
            ),
            out_shape=act_shape,
            compiler_params=_cparams("arbitrary", "arbitrary"),
            name="moe_gu%d" % col,
        )(blk_e, n_active, run_end, xb, w_gu, b3, b3))
    return acts


def _moe_down_kernel(be_ref, na_ref, end_ref, slot_ref, *refs, n_act):
    a_refs = refs[:n_act]
    w_hbm, b_ref, y_hbm, stg, wb, ybuf0, ybuf1, ybuf2, sem, ssem = refs[n_act:]
    m = pl.program_id(1)
    na = na_ref[0]
    p = m % MOE_BUFS
    ybufs = (ybuf0, ybuf1, ybuf2)

    def scatter(s):
        for r in range(ROW_BLOCK):
            pltpu.make_async_copy(ybufs[s].at[pl.ds(r, 1)], y_hbm.at[pl.ds(slot_ref[0, 0, r], 1)],
                                  ssem.at[s]).start()

    def scatter_wait(s):
        pltpu.make_async_copy(ybufs[s], y_hbm.at[pl.ds(0, ROW_BLOCK)], ssem.at[s]).wait()

    @pl.when(m == 0)
    def _():
        ybuf2[...] = jnp.zeros(ybuf2.shape, ybuf2.dtype)

    for s in range(MOE_BUFS):
        @pl.when((m >= 2) & (m <= na + 2) & (p == s))
        def _(s=s):
            scatter_wait(s)

    def copies(e, n):
        return [pltpu.make_async_copy(w_hbm.at[e], stg, sem.at[0])]

    def convert():
        wb[...] = stg[...].astype(BF16)

    active = _weight_stream(be_ref, na_ref, end_ref, 1, copies, convert)

    for s in range(MOE_BUFS):
        @pl.when(active & (p == s))
        def _(s=s):
            scatter((s + MOE_BUFS - 1) % MOE_BUFS)
            a = jnp.concatenate([a_ref[...] for a_ref in a_refs], axis=1)
            ybufs[s][...] = jnp.dot(a, wb[...], preferred_element_type=F32) + b_ref[...]

        @pl.when((m == na) & (p == s))
        def _(s=s):
            scatter((s + MOE_BUFS - 1) % MOE_BUFS)


def _moe_down(acts, slot_prev, w_down, b_down, blk_e, n_active, run_end, n_slots):
    n_rows, tf = acts[0].shape
    ne, f, d = w_down.shape
    nb = n_rows // ROW_BLOCK
    mc = lambda m, na: jnp.minimum(m, na[0] - 1)
    grid_spec = pltpu.PrefetchScalarGridSpec(
        num_scalar_prefetch=3,
        grid=(1, nb + MOE_BUFS),
        in_specs=[pl.BlockSpec((1, 1, ROW_BLOCK), lambda n, m, be, na, en: (jnp.minimum(m, nb), 0, 0),
                               memory_space=pltpu.SMEM),
                  *[pl.BlockSpec((ROW_BLOCK, tf), lambda n, m, be, na, en: (mc(m, na), 0))] * len(acts),
                  pl.BlockSpec(memory_space=pl.ANY),
                  pl.BlockSpec((None, 1, d), lambda n, m, be, na, en: (be[mc(m, na)], 0, 0))],
        out_specs=pl.BlockSpec(memory_space=pl.ANY),
        scratch_shapes=[pltpu.VMEM((f, d), F32), pltpu.VMEM((f, d), BF16),
                        *[pltpu.VMEM((ROW_BLOCK, d), F32)] * MOE_BUFS,
                        pltpu.SemaphoreType.DMA((1,)), pltpu.SemaphoreType.DMA((MOE_BUFS,))],
    )
    return pl.pallas_call(
        functools.partial(_moe_down_kernel, n_act=len(acts)),
        grid_spec=grid_spec,
        out_shape=jax.ShapeDtypeStruct((n_slots + ROW_BLOCK, d), F32),
        compiler_params=_cparams("arbitrary", "arbitrary"),
        name="moe_down",
    )(blk_e, n_active, run_end, slot_prev, *acts, w_down, b_down.reshape(ne, 1, d))


def _combine_kernel(w_ref, h1_ref, g2_ref, *refs):
    y_refs, o_ref = refs[:TOP_K], refs[TOP_K]
    w = w_ref[...]
    acc = y_refs[0][...] * w[:, 0:1]
    for k in range(1, TOP_K):
        acc = acc + y_refs[k][...] * w[:, k:k + 1]
    o_ref[...] = h1_ref[...] + g2_ref[...] * acc


def _combine(top_w, h1, g2, y_slots):
    t, d = h1.shape
    tm = 256
    nt = t // tm
    return pl.pallas_call(
        _combine_kernel,
        grid=(nt,),
        in_specs=[pl.BlockSpec((tm, LANES), lambda i: (i, 0)),
                  pl.BlockSpec((tm, d), lambda i: (i, 0)),
                  pl.BlockSpec((1, d), lambda i: (0, 0))]
                 + [pl.BlockSpec((tm, d), lambda i, k=k: (k * nt + i, 0)) for k in range(TOP_K)],
        out_specs=pl.BlockSpec((tm, d), lambda i: (i, 0)),
        out_shape=jax.ShapeDtypeStruct((t, d), F32),
        compiler_params=_cparams("arbitrary"),
        name="combine",
    )(top_w, h1, g2, *([y_slots] * TOP_K))


def kernel(x, c, ada_w, ada_b, norm1_g, w_in, q_norm_g, k_norm_g, rel_bias, conv_w, conv_b,
           rg_a_w, rg_a_b, rg_x_w, rg_x_b, rg_lambda, w_attn_up, w_rnn_up, w_out, norm2_g,
           w_router, b_router, w_gu, b_gu, w_down, b_down):
    b, s, d = x.shape
    assert b == 1 and ada_w.shape[0] == 1, "single batch row, single layer"
    t = s
    x2 = x.reshape(t, d)
    aw = ATTN_HEADS * HEAD_DIM
    rw = conv_w.shape[2]

    mod = _ada(c, ada_w[0], ada_b[0])
    sh1, sc1, g1, sh2, sc2, g2 = [mod[:, k * d:(k + 1) * d] for k in range(6)]

    w_in0 = w_in[0]
    qg = q_norm_g[0].reshape(1, HEAD_DIM)
    kg = k_norm_g[0].reshape(1, HEAD_DIM)
    n1g = norm1_g[0].reshape(1, d)
    xn, q = _norm_q(x2, n1g, sc1, sh1, w_in0, qg, aw)
    k = _proj(xn, w_in0, kg, col0=aw, width=aw, kind="k", out_dtype=BF16, name="proj_k")
    v = _proj(xn, w_in0, kg, col0=2 * aw, width=aw, kind="id", out_dtype=BF16, name="proj_v")
    rx = _proj(xn, w_in0, kg, col0=3 * aw, width=rw, kind="id", out_dtype=F32, name="proj_rx")
    gate = _proj(xn, w_in0, kg, col0=3 * aw + rw, width=rw, kind="gelu", out_dtype=F32, name="proj_gate")
    sg = _proj(xn, w_in0, kg, col0=3 * aw + 2 * rw, width=2 * d, kind="sigmoid", out_dtype=F32,
               name="proj_sg")

    ya = _attn(q, k, v, _attn_distance_table(rel_bias[0]))
    yb = _rglru(rx, gate, conv_w[0], conv_b[0].reshape(1, rw),
                rg_a_w[0].astype(BF16), rg_a_b[0].reshape(1, rw),
                rg_x_w[0].astype(BF16), rg_x_b[0].reshape(1, rw), rg_lambda[0].reshape(1, rw))

    merged = _upmerge(ya, yb, sg, w_attn_up[0], w_rnn_up[0])
    h1, hn_packed, logits = _mixer_out(
        merged, x2, w_out[0].astype(BF16), g1, norm2_g[0].reshape(1, d), sc2, sh2,
        w_router[0], b_router[0].reshape(1, N_EXPERTS))

    idx_w, topw_w, pos_w, cnt = _route(logits)
    idx = idx_w[:, :TOP_K]
    pos = pos_w[:, :TOP_K]
    counts = cnt[0].astype(I32)
    padded = (counts + ROW_BLOCK - 1) // ROW_BLOCK * ROW_BLOCK
    pend = jnp.cumsum(padded)
    pstart = pend - padded
    experts = jnp.arange(N_EXPERTS, dtype=I32)
    dest = (pos + jnp.sum(jnp.where(idx[:, :, None] == experts, pstart, 0), axis=-1)).reshape(-1)
    n_blocks = (t * TOP_K) // ROW_BLOCK + N_EXPERTS
    n_rows = n_blocks * ROW_BLOCK
    blk_row0 = jnp.arange(n_blocks, dtype=I32) * ROW_BLOCK
    blk_e = jnp.minimum(jnp.sum((pend[None, :] <= blk_row0[:, None]).astype(I32), axis=1), N_EXPERTS - 1)
    n_active = (pend[-1:] // ROW_BLOCK).astype(I32)
    run_end = (pend // ROW_BLOCK).astype(I32)
    n_slots = t * TOP_K
    slot_ids = (jnp.arange(t, dtype=I32)[:, None] + jnp.arange(TOP_K, dtype=I32)[None, :] * t).reshape(-1)
    spare = n_slots + jnp.arange(ROW_BLOCK, dtype=I32)
    row_slot = jnp.tile(spare, n_blocks).at[dest].set(slot_ids, unique_indices=True, mode="promise_in_bounds")
    src_tok = (jnp.where(row_slot < n_slots, row_slot % t, 0) * PACK_TILES).reshape(n_blocks, 1, ROW_BLOCK)
    slot_prev = jnp.concatenate([spare, row_slot]).reshape(n_blocks + 1, 1, ROW_BLOCK)

    acts = _moe_gu(hn_packed, src_tok, w_gu[0], b_gu[0], blk_e, n_active, run_end)
    y_slots = _moe_down(acts, slot_prev, w_down[0], b_down[0], blk_e, n_active, run_end, n_slots)
    out = _combine(topw_w, h1, g2, y_slots)
    return out.reshape(b, s, d)
```

```python
import functools
import math

import jax
import jax.numpy as jnp
from jax import lax
from jax.experimental import pallas as pl
from jax.experimental.pallas import tpu as pltpu

F32 = jnp.float32
BF16 = jnp.bfloat16
U32 = jnp.uint32
I32 = jnp.int32

CHUNK = 64
LEFT_CHUNKS = 8
ATTN_HEADS = 8
HEAD_DIM = 128
REL_MAX = 256
REL_MIN = -(CHUNK - 1)
RNN_BLOCKS = 8
CONV_W = 4
RG_C = 8.0
N_EXPERTS = 32
TOP_K = 4
SWIGLU_LIMIT = 7.0
SWIGLU_ALPHA = 1.702
EPS = 1e-6
NEG_BIG = -1e30

LANES = 128
SUBLANES = 8
VMEM_LIMIT = 56 * 1024 * 1024

ROW_BLOCK = 512
MOE_BUFS = 3
MOE_GU_TN = 1024
PACK_TILES = 8
PROJ_TM = 1024
PROJ_TN = 1024
ADA_TN = 1024
NORM_TM = 512
RGLRU_TT = 512
MIXER_TM = 512
ROUTE_TM = 512
COMBINE_TM = 512
ATTN_QB = 4 * CHUNK
ATTN_KB = ATTN_QB + LEFT_CHUNKS * CHUNK
ATTN_TAB = ATTN_KB + ATTN_QB
CHUNK_SHIFT = CHUNK.bit_length() - 1
assert 1 << CHUNK_SHIFT == CHUNK


def _cparams(*sem):
    return pltpu.CompilerParams(dimension_semantics=sem, vmem_limit_bytes=VMEM_LIMIT)


def _ada_kernel(c_ref, w_ref, b_ref, o_ref):
    c = c_ref[...]
    ca = c * jax.nn.sigmoid(c)
    row = jnp.sum(ca * w_ref[...], axis=0, keepdims=True) + b_ref[...]
    o_ref[...] = jnp.broadcast_to(row, o_ref.shape)


def _ada(c, ada_w, ada_b):
    d, n = ada_w.shape
    tn = ADA_TN
    out = pl.pallas_call(
        _ada_kernel,
        grid=(n // tn,),
        in_specs=[pl.BlockSpec((d, 1), lambda j: (0, 0)),
                  pl.BlockSpec((d, tn), lambda j: (0, j)),
                  pl.BlockSpec((1, tn), lambda j: (0, j))],
        out_specs=pl.BlockSpec((SUBLANES, tn), lambda j: (0, j)),
        out_shape=jax.ShapeDtypeStruct((SUBLANES, n), F32),
        compiler_params=_cparams("arbitrary"),
        name="ada",
    )(c.reshape(d, 1), ada_w, ada_b.reshape(1, n))
    return out[0:1]


def _sigmoid_tanh(x):
    return 0.5 * jnp.tanh(0.5 * x) + 0.5


def _gelu_tanh(x):
    return 0.5 * x * (1.0 + jnp.tanh(math.sqrt(2.0 / math.pi) * (x + 0.044715 * (x * x * x))))


def _head_rms(a, g):
    outs = []
    for hh in range(a.shape[1] // HEAD_DIM):
        s = a[:, hh * HEAD_DIM:(hh + 1) * HEAD_DIM]
        ms = jnp.mean(s * s, axis=-1, keepdims=True)
        outs.append(s * lax.rsqrt(ms + EPS) * g)
    return jnp.concatenate(outs, axis=1)


def _norm_q_kernel(x_ref, g_ref, sc_ref, sh_ref, w_ref, hg_ref, xn_ref, q_ref, wb_ref):
    @pl.when(pl.program_id(0) == 0)
    def _():
        wb_ref[...] = w_ref[...].astype(BF16)

    x = x_ref[...]
    ms = jnp.mean(x * x, axis=-1, keepdims=True)
    xn = x * lax.rsqrt(ms + EPS) * g_ref[...]
    xn = (xn * (1.0 + sc_ref[...]) + sh_ref[...]).astype(BF16)
    xn_ref[...] = xn
    acc = jnp.dot(xn, wb_ref[...], preferred_element_type=F32)
    q_ref[...] = (_head_rms(acc, hg_ref[...]) * (1.0 / math.sqrt(HEAD_DIM))).astype(q_ref.dtype)


def _norm_q(x2, g, sc, sh, w, q_gain, width):
    t, d = x2.shape
    tm = NORM_TM
    vec = lambda: pl.BlockSpec((1, d), lambda i: (0, 0))
    return pl.pallas_call(
        _norm_q_kernel,
        grid=(t // tm,),
        in_specs=[pl.BlockSpec((tm, d), lambda i: (i, 0)), vec(), vec(), vec(),
                  pl.BlockSpec((d, width), lambda i: (0, 0), pipeline_mode=pl.Buffered(1)),
                  pl.BlockSpec((1, HEAD_DIM), lambda i: (0, 0))],
        out_specs=[pl.BlockSpec((tm, d), lambda i: (i, 0)),
                   pl.BlockSpec((tm, width), lambda i: (i, 0))],
        out_shape=[jax.ShapeDtypeStruct((t, d), BF16), jax.ShapeDtypeStruct((t, width), BF16)],
        scratch_shapes=[pltpu.VMEM((d, width), BF16)],
        compiler_params=_cparams("arbitrary"),
        name="norm1_q",
    )(x2, g, sc, sh, w, q_gain)


def _proj_kernel(x_ref, w_ref, hg_ref, o_ref, wb_ref, *, kind):
    @pl.when(pl.program_id(1) == 0)
    def _():
        wb_ref[...] = w_ref[...].astype(BF16)

    acc = jnp.dot(x_ref[...], wb_ref[...], preferred_element_type=F32)
    if kind == "k":
        r = _head_rms(acc, hg_ref[...])
    elif kind == "gelu":
        r = _gelu_tanh(acc)
    elif kind == "sigmoid":
        r = _sigmoid_tanh(acc)
    else:
        r = acc
    o_ref[...] = r.astype(o_ref.dtype)


def _proj(xn, w, head_gain, *, col0, width, kind, out_dtype, name):
    t, d = xn.shape
    tm, tn = PROJ_TM, PROJ_TN
    j0 = col0 // tn
    return pl.pallas_call(
        functools.partial(_proj_kernel, kind=kind),
        grid=(width // tn, t // tm),
        in_specs=[pl.BlockSpec((tm, d), lambda j, i: (i, 0)),
                  pl.BlockSpec((d, tn), lambda j, i: (0, j + j0)),
                  pl.BlockSpec((1, HEAD_DIM), lambda j, i: (0, 0))],
        out_specs=pl.BlockSpec((tm, tn), lambda j, i: (i, j)),
        out_shape=jax.ShapeDtypeStruct((t, width), out_dtype),
        scratch_shapes=[pltpu.VMEM((d, tn), BF16)],
        compiler_params=_cparams("arbitrary", "arbitrary"),
        name=name,
    )(xn, w, head_gain)


def _attn_kernel(q_ref, k0_ref, k1_ref, k2_ref, v0_ref, v1_ref, v2_ref, ftab_ref, o_ref, bias_ref):
    i = pl.program_id(0)
    col = lax.broadcasted_iota(I32, (ATTN_QB, ATTN_KB), 1)

    @pl.when(i == 0)
    def _():
        row = lax.broadcasted_iota(I32, (ATTN_QB, ATTN_KB), 0)
        cq = row >> CHUNK_SHIFT
        ck = col >> CHUNK_SHIFT
        visible = (ck >= cq) & (ck <= cq + LEFT_CHUNKS)
        for h in range(ATTN_HEADS):
            g = jnp.broadcast_to(ftab_ref[h:h + 1, :], (ATTN_QB, ATTN_TAB))
            y = pltpu.roll(g, 0, 1, stride=1, stride_axis=0)
            bias_ref[h] = jnp.where(visible, y[:, :ATTN_KB], NEG_BIG)

    def heads(mask_left_pad):
        in_seq = col >= (LEFT_CHUNKS * CHUNK - i * ATTN_QB)
        for h in range(ATTN_HEADS):
            sl = slice(h * HEAD_DIM, (h + 1) * HEAD_DIM)
            q = q_ref[:, sl]
            k = jnp.concatenate([k0_ref[:, sl], k1_ref[:, sl], k2_ref[:, sl]], axis=0)
            v = jnp.concatenate([v0_ref[:, sl], v1_ref[:, sl], v2_ref[:, sl]], axis=0)
            s = lax.dot_general(q, k, (((1,), (1,)), ((), ())), preferred_element_type=F32)
            s = s + bias_ref[h]
            if mask_left_pad:
                s = jnp.where(in_seq, s, NEG_BIG)
            m = jnp.max(s, axis=-1, keepdims=True)
            p = jnp.exp(s - m)
            l = jnp.sum(p, axis=-1, keepdims=True)
            o = jnp.dot(p.astype(BF16), v, preferred_element_type=F32) / l
            o_ref[:, sl] = o.astype(o_ref.dtype)

    first_blocks = (LEFT_CHUNKS * CHUNK) // ATTN_QB

    @pl.when(i < first_blocks)
    def _():
        heads(True)

    @pl.when(i >= first_blocks)
    def _():
        heads(False)


def _attn(q, k, v, ftab):
    t = q.shape[0]
    w = ATTN_HEADS * HEAD_DIM
    qb = ATTN_QB
    blk = lambda f: pl.BlockSpec((qb, w), f)
    return pl.pallas_call(
        _attn_kernel,
        grid=(t // qb,),
        in_specs=[blk(lambda i: (i, 0)),
                  blk(lambda i: (jnp.maximum(i - 2, 0), 0)), blk(lambda i: (jnp.maximum(i - 1, 0), 0)),
                  blk(lambda i: (i, 0)),
                  blk(lambda i: (jnp.maximum(i - 2, 0), 0)), blk(lambda i: (jnp.maximum(i - 1, 0), 0)),
                  blk(lambda i: (i, 0)),
                  pl.BlockSpec((ATTN_HEADS, ATTN_TAB), lambda i: (0, 0))],
        out_specs=blk(lambda i: (i, 0)),
        out_shape=jax.ShapeDtypeStruct((t, w), BF16),
        scratch_shapes=[pltpu.VMEM((ATTN_HEADS, qb, ATTN_KB), F32)],
        compiler_params=_cparams("arbitrary"),
        name="attn",
    )(q, k, k, k, v, v, v, ftab)


def _attn_distance_table(rel_bias):
    pad = LEFT_CHUNKS * CHUNK
    n_far = pad - REL_MAX + 1
    n_mid = REL_MAX - REL_MIN
    n_ahead = ATTN_KB - n_far - n_mid
    far = rel_bias[:, -1:]
    return jnp.concatenate([jnp.broadcast_to(far, (ATTN_HEADS, n_far)),
                            rel_bias[:, :n_mid][:, ::-1],
                            jnp.broadcast_to(rel_bias[:, :1], (ATTN_HEADS, n_ahead)),
                            jnp.broadcast_to(far, (ATTN_HEADS, ATTN_TAB - ATTN_KB))], axis=1)


def _rglru_kernel(rx_ref, gate_ref, cw_ref, cb_ref, wa_ref, ba_ref, wx_ref, bx_ref, lam_ref,
                  o_ref, xbuf, a_s, u_s, hc, *, tt):
    i = pl.program_id(0)
    width = rx_ref.shape[1]
    bw = width // RNN_BLOCKS

    @pl.when(i == 0)
    def _():
        xbuf[0:SUBLANES, :] = jnp.zeros((SUBLANES, width), F32)
        hc[...] = jnp.zeros(hc.shape, F32)

    xbuf[SUBLANES:SUBLANES + tt, :] = rx_ref[...]
    xc = cb_ref[...] + cw_ref[CONV_W - 1:CONV_W, :] * xbuf[SUBLANES:SUBLANES + tt, :]
    for d in range(1, CONV_W):
        xc = xc + cw_ref[CONV_W - 1 - d:CONV_W - d, :] * xbuf[SUBLANES - d:SUBLANES - d + tt, :]
    tail = xbuf[tt:tt + SUBLANES, :]
    xbuf[0:SUBLANES, :] = tail

    z = -lam_ref[...]
    softplus = jnp.maximum(z, 0.0) + jnp.log1p(jnp.exp(-jnp.abs(z)))
    xcb = xc.astype(BF16)
    for b in range(RNN_BLOCKS):
        sl = slice(b * bw, (b + 1) * bw)
        xb_ = xcb[:, sl]
        r = _sigmoid_tanh(jnp.dot(xb_, wa_ref[b], preferred_element_type=F32) + ba_ref[:, sl])
        ig = _sigmoid_tanh(jnp.dot(xb_, wx_ref[b], preferred_element_type=F32) + bx_ref[:, sl])
        log_a = (-RG_C) * r * softplus[:, sl]
        a = jnp.exp(log_a)
        th = jnp.tanh(log_a)
        one_m_a2 = (-2.0) * th / (1.0 - th)
        a_s[:, sl] = a
        u_s[:, sl] = jnp.sqrt(one_m_a2) * (ig * xc[:, sl])

    row = lax.broadcasted_iota(I32, (SUBLANES, width), 0)

    def body(g, h):
        r0 = pl.multiple_of(g * SUBLANES, SUBLANES)
        a = a_s[pl.ds(r0, SUBLANES), :]
        u = u_s[pl.ds(r0, SUBLANES), :]
        for s in (1, 2, 4):
            a_sh = pltpu.roll(a, s, 0)
            u_sh = pltpu.roll(u, s, 0)
            keep = row >= s
            u = jnp.where(keep, a * u_sh + u, u)
            a = jnp.where(keep, a * a_sh, a)
        hh = a * h + u
        u_s[pl.ds(r0, SUBLANES), :] = hh
        return hh[SUBLANES - 1:SUBLANES, :]

    h_last = lax.fori_loop(0, tt // SUBLANES, body, hc[0:1, :])
    hc[0:1, :] = h_last
    o_ref[...] = (u_s[...] * gate_ref[...]).astype(o_ref.dtype)


def _rglru(rx, gate, conv_w, conv_b, wa_bf, ba, wx_bf, bx, lam):
    t = rx.shape[0]
    width = conv_w.shape[1]
    bw = width // RNN_BLOCKS
    tt = RGLRU_TT
    vec = lambda: pl.BlockSpec((1, width), lambda i: (0, 0))
    wblk = lambda: pl.BlockSpec((RNN_BLOCKS, bw, bw), lambda i: (0, 0, 0))
    return pl.pallas_call(
        functools.partial(_rglru_kernel, tt=tt),
        grid=(t // tt,),
        in_specs=[pl.BlockSpec((tt, width), lambda i: (i, 0)),
                  pl.BlockSpec((tt, width), lambda i: (i, 0)),
                  pl.BlockSpec((CONV_W, width), lambda i: (0, 0)), vec(),
                  wblk(), vec(), wblk(), vec(), vec()],
        out_specs=pl.BlockSpec((tt, width), lambda i: (i, 0)),
        out_shape=jax.ShapeDtypeStruct((t, width), BF16),
        scratch_shapes=[pltpu.VMEM((tt + SUBLANES, width), F32),
                        pltpu.VMEM((tt, width), F32),
                        pltpu.VMEM((tt, width), F32),
                        pltpu.VMEM((SUBLANES, width), F32)],
        compiler_params=_cparams("arbitrary"),
        name="rglru",
    )(rx, gate, conv_w, conv_b, wa_bf, ba, wx_bf, bx, lam)


def _upmerge_kernel(ya_ref, yb_ref, sga_ref, sgb_ref, wa_ref, wr_ref, o_ref, wab_ref, wrb_ref):
    @pl.when(pl.program_id(1) == 0)
    def _():
        wab_ref[...] = wa_ref[...].astype(BF16)
        wrb_ref[...] = wr_ref[...].astype(BF16)

    up_a = jnp.dot(ya_ref[...], wab_ref[...], preferred_element_type=F32)
    up_b = jnp.dot(yb_ref[...], wrb_ref[...], preferred_element_type=F32)
    o_ref[...] = (sga_ref[...] * up_a + sgb_ref[...] * up_b).astype(o_ref.dtype)


def _upmerge(ya, yb, sg, wa, wr):
    t, aw = ya.shape
    rw = yb.shape[1]
    d = wa.shape[1]
    tm, tn = PROJ_TM, PROJ_TN
    nj = d // tn
    return pl.pallas_call(
        _upmerge_kernel,
        grid=(nj, t // tm),
        in_specs=[pl.BlockSpec((tm, aw), lambda j, i: (i, 0)),
                  pl.BlockSpec((tm, rw), lambda j, i: (i, 0)),
                  pl.BlockSpec((tm, tn), lambda j, i: (i, j)),
                  pl.BlockSpec((tm, tn), lambda j, i: (i, j + nj)),
                  pl.BlockSpec((aw, tn), lambda j, i: (0, j)),
                  pl.BlockSpec((rw, tn), lambda j, i: (0, j))],
        out_specs=pl.BlockSpec((tm, tn), lambda j, i: (i, j)),
        out_shape=jax.ShapeDtypeStruct((t, d), BF16),
        scratch_shapes=[pltpu.VMEM((aw, tn), BF16), pltpu.VMEM((rw, tn), BF16)],
        compiler_params=_cparams("arbitrary", "arbitrary"),
        name="upmerge",
    )(ya, yb, sg, sg, wa, wr)


def _mixer_out_kernel(mg_ref, x_ref, wo_ref, g1_ref, n2_ref, sc2_ref, sh2_ref, wrh_ref, brt_ref,
                      h1_ref, hnp_ref, lg_ref):
    proj = jnp.dot(mg_ref[...], wo_ref[...], preferred_element_type=F32)
    h1 = x_ref[...] + g1_ref[...] * proj
    h1_ref[...] = h1
    ms = jnp.mean(h1 * h1, axis=-1, keepdims=True)
    hn = h1 * lax.rsqrt(ms + EPS) * n2_ref[...]
    hn = hn * (1.0 + sc2_ref[...]) + sh2_ref[...]
    hb = hn.astype(BF16)
    hb32 = hb.astype(F32)
    hl = (hn - hb32).astype(BF16)
    ne = lg_ref.shape[1]
    hh = jnp.dot(hb, wrh_ref[...], preferred_element_type=F32)
    lh = jnp.dot(hl, wrh_ref[:, :ne], preferred_element_type=F32)
    lg_ref[...] = hh[:, :ne] + hh[:, ne:] + lh + brt_ref[...]
    half = hn.shape[1] // 2
    lo = pltpu.bitcast(hb32[:, :half], U32) >> 16
    hi = pltpu.bitcast(hb32[:, half:], U32) & jnp.uint32(0xFFFF0000)
    packed = hi | lo
    tm = packed.shape[0]
    for s in range(PACK_TILES):
        hnp_ref[pl.ds(s, tm, stride=PACK_TILES), :] = packed[:, s * LANES:(s + 1) * LANES]


def _mixer_out(merged, x2, wo_bf, g1, n2g, sc2, sh2, w_router, b_router):
    t, d = x2.shape
    ne = w_router.shape[1]
    tm = MIXER_TM
    w_hi = w_router.astype(BF16)
    w_lo = (w_router - w_hi.astype(F32)).astype(BF16)
    w_split = jnp.concatenate([w_hi, w_lo], axis=1)
    const = lambda shape: pl.BlockSpec(shape, lambda i: (0,) * len(shape), pipeline_mode=pl.Buffered(1))
    return pl.pallas_call(
        _mixer_out_kernel,
        grid=(t // tm,),
        in_specs=[pl.BlockSpec((tm, d), lambda i: (i, 0)),
                  pl.BlockSpec((tm, d), lambda i: (i, 0)),
                  const((d, d)),
                  const((1, d)), const((1, d)), const((1, d)), const((1, d)),
                  const((d, 2 * ne)), const((1, ne))],
        out_specs=[pl.BlockSpec((tm, d), lambda i: (i, 0)),
                   pl.BlockSpec((tm * PACK_TILES, LANES), lambda i: (i, 0)),
                   pl.BlockSpec((tm, ne), lambda i: (i, 0))],
        out_shape=[jax.ShapeDtypeStruct((t, d), F32),
                   jax.ShapeDtypeStruct((t * PACK_TILES, LANES), U32),
                   jax.ShapeDtypeStruct((t, ne), F32)],
        compiler_params=_cparams("arbitrary"),
        name="mixer_out",
    )(merged, x2, wo_bf, g1, n2g, sc2, sh2, w_split, b_router)


def _route_kernel(lg_ref, idx_ref, w_ref, pos_ref, cnt_ref, carry, *, rt):
    i = pl.program_id(0)
    ne = lg_ref.shape[1]

    @pl.when(i == 0)
    def _():
        carry[...] = jnp.zeros(carry.shape, F32)

    l = lg_ref[...]
    lane = lax.broadcasted_iota(I32, (rt, ne), 1).astype(F32)
    vals, idxs, hots = [], [], []
    for _ in range(TOP_K):
        m = jnp.max(l, axis=-1, keepdims=True)
        ik = jnp.min(jnp.where(l == m, lane, float(ne)), axis=-1, keepdims=True)
        hot = lane == ik
        vals.append(m)
        idxs.append(ik)
        hots.append(hot)
        l = jnp.where(hot, -jnp.inf, l)
    es = [jnp.exp(v - vals[0]) for v in vals]
    tot = es[0] + es[1] + es[2] + es[3]
    sel = jnp.zeros((rt, ne), F32)
    for hot in hots:
        sel = sel + hot.astype(F32)
    rr = lax.broadcasted_iota(I32, (rt, rt), 0)
    cc = lax.broadcasted_iota(I32, (rt, rt), 1)
    tri = (cc < rr).astype(BF16)
    before = jnp.dot(tri, sel.astype(BF16), preferred_element_type=F32) + carry[0:1, :]
    out_lane = lax.broadcasted_iota(I32, (rt, LANES), 1)
    idx_o = jnp.zeros((rt, LANES), F32)
    w_o = jnp.zeros((rt, LANES), F32)
    pos_o = jnp.zeros((rt, LANES), F32)
    for k in range(TOP_K):
        pk = jnp.sum(jnp.where(hots[k], before, 0.0), axis=-1, keepdims=True)
        idx_o = jnp.where(out_lane == k, idxs[k], idx_o)
        w_o = jnp.where(out_lane == k, es[k] / tot, w_o)
        pos_o = jnp.where(out_lane == k, pk, pos_o)
    idx_ref[...] = idx_o.astype(I32)
    w_ref[...] = w_o
    pos_ref[...] = pos_o.astype(I32)
    carry[0:1, :] = carry[0:1, :] + jnp.sum(sel, axis=0, keepdims=True)
    cnt_ref[...] = carry[...]


def _route(logits):
    t, ne = logits.shape
    rt = ROUTE_TM
    wide = lambda dt: jax.ShapeDtypeStruct((t, LANES), dt)
    return pl.pallas_call(
        functools.partial(_route_kernel, rt=rt),
        grid=(t // rt,),
        in_specs=[pl.BlockSpec((rt, ne), lambda i: (i, 0))],
        out_specs=[pl.BlockSpec((rt, LANES), lambda i: (i, 0)),
                   pl.BlockSpec((rt, LANES), lambda i: (i, 0)),
                   pl.BlockSpec((rt, LANES), lambda i: (i, 0)),
                   pl.BlockSpec((SUBLANES, ne), lambda i: (0, 0))],
        out_shape=[wide(I32), wide(F32), wide(I32), jax.ShapeDtypeStruct((SUBLANES, ne), F32)],
        scratch_shapes=[pltpu.VMEM((SUBLANES, ne), F32)],
        compiler_params=_cparams("arbitrary"),
        name="route",
    )(logits)


def _unpack_rows(x_ref):
    los, his = [], []
    for s in range(PACK_TILES):
        p = x_ref[pl.ds(s, ROW_BLOCK, stride=PACK_TILES), :]
        los.append(pltpu.bitcast(p << 16, F32).astype(BF16))
        his.append(pltpu.bitcast(p & jnp.uint32(0xFFFF0000), F32).astype(BF16))
    return jnp.concatenate(los + his, axis=1)


def _first_of_expert(be_ref, m, na):
    mm = jnp.minimum(m, na - 1)
    prev = be_ref[jnp.maximum(mm - 1, 0)]
    return (mm == 0) | (be_ref[mm] != prev)


def _weight_stream(be_ref, na_ref, end_ref, n_tiles, copies, convert):
    n = pl.program_id(0)
    m = pl.program_id(1)
    na = na_ref[0]
    active = m < na

    @pl.when((n == 0) & (m == 0))
    def _():
        for cp in copies(be_ref[0], 0):
            cp.start(priority=1)

    @pl.when(active & _first_of_expert(be_ref, m, na))
    def _():
        e = be_ref[m]
        for cp in copies(e, n):
            cp.wait()
        convert()
        run_end = end_ref[e]
        wrap = run_end >= na
        n2 = jnp.where(wrap, n + 1, n)
        m2 = jnp.where(wrap, 0, run_end)

        @pl.when(n2 < n_tiles)
        def _():
            for cp in copies(be_ref[m2], n2):
                cp.start(priority=1)

    return active


def _gu_weight_stream(be_ref, na_ref, end_ref, w_hbm, stg_g, stg_u, wgb, wub, sem, *, tn, col, f):
    def copies(e, n):
        return [pltpu.make_async_copy(w_hbm.at[e, :, pl.ds(col * tn, tn)], stg_g, sem.at[0]),
                pltpu.make_async_copy(w_hbm.at[e, :, pl.ds(f + col * tn, tn)], stg_u, sem.at[1])]

    def convert():
        wgb[...] = stg_g[...].astype(BF16)
        wub[...] = stg_u[...].astype(BF16)

    return _weight_stream(be_ref, na_ref, end_ref, 1, copies, convert)


def _gu_compute(x, wgb, wub, bg_ref, bu_ref, o_ref):
    g = jnp.dot(x, wgb[...], preferred_element_type=F32) + bg_ref[...]
    u = jnp.dot(x, wub[...], preferred_element_type=F32) + bu_ref[...]
    g = jnp.minimum(g, SWIGLU_LIMIT)
    u = jnp.clip(u, -SWIGLU_LIMIT, SWIGLU_LIMIT)
    glu = g * _sigmoid_tanh(SWIGLU_ALPHA * g)
    o_ref[...] = ((u + 1.0) * glu).astype(o_ref.dtype)


def _moe_gu_gather_kernel(be_ref, na_ref, end_ref, tok0_ref, tok1_ref, tok2_ref, hn_hbm, w_hbm, bg_ref,
                          bu_ref, o_ref, xb_ref, stg_g, stg_u, wgb, wub, xbuf0, xbuf1, xbuf2, sem, gsem,
                          *, tn, col, f):
    m = pl.program_id(1)
    na = na_ref[0]
    slot = m % MOE_BUFS
    xbufs = (xbuf0, xbuf1, xbuf2)

    def gather(tok_ref, s):
        for r in range(ROW_BLOCK):
            src = pl.multiple_of(tok_ref[0, 0, r], PACK_TILES)
            pltpu.make_async_copy(hn_hbm.at[pl.ds(src, PACK_TILES)],
                                  xbufs[s].at[pl.ds(r * PACK_TILES, PACK_TILES)], gsem.at[s]
                                  ).start(priority=r % 2)

    def gather_wait(s):
        pltpu.make_async_copy(hn_hbm.at[pl.ds(0, ROW_BLOCK * PACK_TILES)], xbufs[s], gsem.at[s]).wait()

    @pl.when(m == 0)
    def _():
        gather(tok0_ref, 0)
        gather(tok1_ref, 1)

    active = _gu_weight_stream(be_ref, na_ref, end_ref, w_hbm, stg_g, stg_u, wgb, wub, sem,
                               tn=tn, col=col, f=f)

    def step(s):
        gather_wait(s)
        gather(tok2_ref, (s + 2) % MOE_BUFS)
        xb_ref[...] = xbufs[s][...]
        _gu_compute(_unpack_rows(xbufs[s]), wgb, wub, bg_ref, bu_ref, o_ref)

    for s in range(MOE_BUFS):
        @pl.when(active & (slot == s))
        def _(s=s):
            step(s)

        @pl.when(active & (m == na - 1) & (slot == s))
        def _(s=s):
            gather_wait((s + 1) % MOE_BUFS)
            gather_wait((s + 2) % MOE_BUFS)

    @pl.when(jnp.logical_not(active))
    def _():
        o_ref[...] = jnp.zeros(o_ref.shape, o_ref.dtype)
        xb_ref[...] = jnp.zeros(xb_ref.shape, xb_ref.dtype)


def _moe_gu_dense_kernel(be_ref, na_ref, end_ref, x_ref, w_hbm, bg_ref, bu_ref, o_ref,
                         stg_g, stg_u, wgb, wub, sem, *, tn, col, f):
    active = _gu_weight_stream(be_ref, na_ref, end_ref, w_hbm, stg_g, stg_u, wgb, wub, sem,
                               tn=tn, col=col, f=f)

    @pl.when(active)
    def _():
        _gu_compute(_unpack_rows(x_ref), wgb, wub, bg_ref, bu_ref, o_ref)

    @pl.when(jnp.logical_not(active))
    def _():
        o_ref[...] = jnp.zeros(o_ref.shape, o_ref.dtype)


def _moe_gu(hn_packed, src_tok, w_gu, b_gu, blk_e, n_active, run_end):
    nb = src_tok.shape[0]
    ne, d, f2 = w_gu.shape
    f = f2 // 2
    tn = MOE_GU_TN
    mc = lambda m, na: jnp.minimum(m, na[0] - 1)
    b3 = b_gu.reshape(ne, 1, f2)
    weight_scratch = [pltpu.VMEM((d, tn), F32), pltpu.VMEM((d, tn), F32),
                      pltpu.VMEM((d, tn), BF16), pltpu.VMEM((d, tn), BF16)]
    xrows = ROW_BLOCK * PACK_TILES
    act_shape = jax.ShapeDtypeStruct((nb * ROW_BLOCK, tn), BF16)

    def bias_specs(col):
        return [pl.BlockSpec((None, 1, tn), lambda n, m, be, na, en: (be[mc(m, na)], 0, col)),
                pl.BlockSpec((None, 1, tn), lambda n, m, be, na, en: (be[mc(m, na)], 0, col + f // tn))]

    tok_blk = lambda f_: pl.BlockSpec((1, 1, ROW_BLOCK), f_, memory_space=pltpu.SMEM)
    act0, xb = pl.pallas_call(
        functools.partial(_moe_gu_gather_kernel, tn=tn, col=0, f=f),
        grid_spec=pltpu.PrefetchScalarGridSpec(
            num_scalar_prefetch=3,
            grid=(1, nb),
            in_specs=[tok_blk(lambda n, m, be, na, en: (0, 0, 0)),
                      tok_blk(lambda n, m, be, na, en: (1 % na[0], 0, 0)),
                      tok_blk(lambda n, m, be, na, en: ((m + 2) % na[0], 0, 0)),
                      pl.BlockSpec(memory_space=pl.ANY),
                      pl.BlockSpec(memory_space=pl.ANY)] + bias_specs(0),
            out_specs=[pl.BlockSpec((ROW_BLOCK, tn), lambda n, m, be, na, en: (m, 0)),
                       pl.BlockSpec((xrows, LANES), lambda n, m, be, na, en: (m, 0))],
            scratch_shapes=weight_scratch + [pltpu.VMEM((xrows, LANES), U32)] * MOE_BUFS
                           + [pltpu.SemaphoreType.DMA((2,)), pltpu.SemaphoreType.DMA((MOE_BUFS,))],
        ),
        out_shape=[act_shape, jax.ShapeDtypeStruct((nb * xrows, LANES), U32)],
        compiler_params=_cparams("arbitrary", "arbitrary"),
        name="moe_gu0",
    )(blk_e, n_active, run_end, src_tok, src_tok, src_tok, hn_packed, w_gu, b3, b3)

    acts = [act0]
    for col in range(1, f // tn):
        acts.append(pl.pallas_call(
            functools.partial(_moe_gu_dense_kernel, tn=tn, col=col, f=f),
            grid_spec=pltpu.PrefetchScalarGridSpec(
                num_scalar_prefetch=3,
                grid=(1, nb),
                in_specs=[pl.BlockSpec((xrows, LANES), lambda n, m, be, na, en: (mc(m, na), 0)),
                          pl.BlockSpec(memory_space=pl.ANY)] + bias_specs(col),
                out_specs=pl.BlockSpec((ROW_BLOCK, tn), lambda n, m, be, na, en: (m, 0)),
                scratch_shapes=weight_scratch + [pltpu.SemaphoreType.DMA((2,))],
            ),
            out_shape=act_shape,
            compiler_params=_cparams("arbitrary", "arbitrary"),
            name="moe_gu%d" % col,
        )(blk_e, n_active, run_end, xb, w_gu, b3, b3))
    return acts


def _moe_down_kernel(be_ref, na_ref, end_ref, slot_ref, *refs, n_act):
    a_refs = refs[:n_act]
    w_hbm, b_ref, y_hbm, stg, wb, ybuf0, ybuf1, ybuf2, sem, ssem = refs[n_act:]
    m = pl.program_id(1)
    na = na_ref[0]
    p = m % MOE_BUFS
    ybufs = (ybuf0, ybuf1, ybuf2)

    def scatter(s):
        for r in range(ROW_BLOCK):
            pltpu.make_async_copy(ybufs[s].at[pl.ds(r, 1)], y_hbm.at[pl.ds(slot_ref[0, 0, r], 1)],
                                  ssem.at[s]).start()

    def scatter_wait(s):
        pltpu.make_async_copy(ybufs[s], y_hbm.at[pl.ds(0, ROW_BLOCK)], ssem.at[s]).wait()

    @pl.when(m == 0)
    def _():
        ybuf2[...] = jnp.zeros(ybuf2.shape, ybuf2.dtype)

    for s in range(MOE_BUFS):
        @pl.when((m >= 2) & (m <= na + 2) & (p == s))
        def _(s=s):
            scatter_wait(s)

    def copies(e, n):
        return [pltpu.make_async_copy(w_hbm.at[e], stg, sem.at[0])]

    def convert():
        wb[...] = stg[...].astype(BF16)

    active = _weight_stream(be_ref, na_ref, end_ref, 1, copies, convert)

    for s in range(MOE_BUFS):
        @pl.when(active & (p == s))
        def _(s=s):
            scatter((s + MOE_BUFS - 1) % MOE_BUFS)
            a = jnp.concatenate([a_ref[...] for a_ref in a_refs], axis=1)
            ybufs[s][...] = jnp.dot(a, wb[...], preferred_element_type=F32) + b_ref[...]

        @pl.when((m == na) & (p == s))
        def _(s=s):
            scatter((s + MOE_BUFS - 1) % MOE_BUFS)


def _moe_down(acts, slot_prev, w_down, b_down, blk_e, n_active, run_end, n_slots):
    n_rows, tf = acts[0].shape
    ne, f, d = w_down.shape
    nb = n_rows // ROW_BLOCK
    mc = lambda m, na: jnp.minimum(m, na[0] - 1)
    grid_spec = pltpu.PrefetchScalarGridSpec(
        num_scalar_prefetch=3,
        grid=(1, nb + MOE_BUFS),
        in_specs=[pl.BlockSpec((1, 1, ROW_BLOCK), lambda n, m, be, na, en: (jnp.minimum(m, nb), 0, 0),
                               memory_space=pltpu.SMEM),
                  *[pl.BlockSpec((ROW_BLOCK, tf), lambda n, m, be, na, en: (mc(m, na), 0))] * len(acts),
                  pl.BlockSpec(memory_space=pl.ANY),
                  pl.BlockSpec((None, 1, d), lambda n, m, be, na, en: (be[mc(m, na)], 0, 0))],
        out_specs=pl.BlockSpec(memory_space=pl.ANY),
        scratch_shapes=[pltpu.VMEM((f, d), F32), pltpu.VMEM((f, d), BF16),
                        *[pltpu.VMEM((ROW_BLOCK, d), F32)] * MOE_BUFS,
                        pltpu.SemaphoreType.DMA((1,)), pltpu.SemaphoreType.DMA((MOE_BUFS,))],
    )
    return pl.pallas_call(
        functools.partial(_moe_down_kernel, n_act=len(acts)),
        grid_spec=grid_spec,
        out_shape=jax.ShapeDtypeStruct((n_slots + ROW_BLOCK, d), F32),
        compiler_params=_cparams("arbitrary", "arbitrary"),
        name="moe_down",
    )(blk_e, n_active, run_end, slot_prev, *acts, w_down, b_down.reshape(ne, 1, d))


def _combine_kernel(w_ref, h1_ref, g2_ref, *refs):
    y_refs, o_ref = refs[:TOP_K], refs[TOP_K]
    w = w_ref[...]
    acc = y_refs[0][...] * w[:, 0:1]
    for k in range(1, TOP_K):
        acc = acc + y_refs[k][...] * w[:, k:k + 1]
    o_ref[...] = h1_ref[...] + g2_ref[...] * acc


def _combine(top_w, h1, g2, y_slots):
    t, d = h1.shape
    tm = COMBINE_TM
    nt = t // tm
    return pl.pallas_call(
        _combine_kernel,
        grid=(nt,),
        in_specs=[pl.BlockSpec((tm, LANES), lambda i: (i, 0)),
                  pl.BlockSpec((tm, d), lambda i: (i, 0)),
                  pl.BlockSpec((1, d), lambda i: (0, 0))]
                 + [pl.BlockSpec((tm, d), lambda i, k=k: (k * nt + i, 0)) for k in range(TOP_K)],
        out_specs=pl.BlockSpec((tm, d), lambda i: (i, 0)),
        out_shape=jax.ShapeDtypeStruct((t, d), F32),
        compiler_params=_cparams("arbitrary"),
        name="combine",
    )(top_w, h1, g2, *([y_slots] * TOP_K))


def kernel(x, c, ada_w, ada_b, norm1_g, w_in, q_norm_g, k_norm_g, rel_bias, conv_w, conv_b,
           rg_a_w, rg_a_b, rg_x_w, rg_x_b, rg_lambda, w_attn_up, w_rnn_up, w_out, norm2_g,
           w_router, b_router, w_gu, b_gu, w_down, b_down):
    b, s, d = x.shape
    assert b == 1 and ada_w.shape[0] == 1, "single batch row, single layer"
    assert d == 2 * PACK_TILES * LANES, "packed token row must be exactly one (PACK_TILES, LANES) tile"
    assert s % PROJ_TM == 0 and (s * TOP_K) % ROW_BLOCK == 0, "sequence must tile evenly"
    t = s
    x2 = x.reshape(t, d)
    aw = ATTN_HEADS * HEAD_DIM
    rw = conv_w.shape[2]

    mod = _ada(c, ada_w[0], ada_b[0])
    sh1, sc1, g1, sh2, sc2, g2 = [mod[:, k * d:(k + 1) * d] for k in range(6)]

    w_in0 = w_in[0]
    qg = q_norm_g[0].reshape(1, HEAD_DIM)
    kg = k_norm_g[0].reshape(1, HEAD_DIM)
    n1g = norm1_g[0].reshape(1, d)
    xn, q = _norm_q(x2, n1g, sc1, sh1, w_in0, qg, aw)
    k = _proj(xn, w_in0, kg, col0=aw, width=aw, kind="k", out_dtype=BF16, name="proj_k")
    v = _proj(xn, w_in0, kg, col0=2 * aw, width=aw, kind="id", out_dtype=BF16, name="proj_v")
    rx = _proj(xn, w_in0, kg, col0=3 * aw, width=rw, kind="id", out_dtype=F32, name="proj_rx")
    gate = _proj(xn, w_in0, kg, col0=3 * aw + rw, width=rw, kind="gelu", out_dtype=F32, name="proj_gate")
    sg = _proj(xn, w_in0, kg, col0=3 * aw + 2 * rw, width=2 * d, kind="sigmoid", out_dtype=F32,
               name="proj_sg")

    ya = _attn(q, k, v, _attn_distance_table(rel_bias[0]))
    yb = _rglru(rx, gate, conv_w[0], conv_b[0].reshape(1, rw),
                rg_a_w[0].astype(BF16), rg_a_b[0].reshape(1, rw),
                rg_x_w[0].astype(BF16), rg_x_b[0].reshape(1, rw), rg_lambda[0].reshape(1, rw))

    merged = _upmerge(ya, yb, sg, w_attn_up[0], w_rnn_up[0])
    h1, hn_packed, logits = _mixer_out(
        merged, x2, w_out[0].astype(BF16), g1, norm2_g[0].reshape(1, d), sc2, sh2,
        w_router[0], b_router[0].reshape(1, N_EXPERTS))

    idx_w, topw_w, pos_w, cnt = _route(logits)
    idx = idx_w[:, :TOP_K]
    pos = pos_w[:, :TOP_K]
    counts = cnt[0].astype(I32)
    padded = (counts + ROW_BLOCK - 1) // ROW_BLOCK * ROW_BLOCK
    pend = jnp.cumsum(padded)
    pstart = pend - padded
    experts = jnp.arange(N_EXPERTS, dtype=I32)
    dest = (pos + jnp.sum(jnp.where(idx[:, :, None] == experts, pstart, 0), axis=-1)).reshape(-1)
    n_blocks = (t * TOP_K) // ROW_BLOCK + N_EXPERTS
    n_rows = n_blocks * ROW_BLOCK
    blk_row0 = jnp.arange(n_blocks, dtype=I32) * ROW_BLOCK
    blk_e = jnp.minimum(jnp.sum((pend[None, :] <= blk_row0[:, None]).astype(I32), axis=1), N_EXPERTS - 1)
    n_active = (pend[-1:] // ROW_BLOCK).astype(I32)
    run_end = (pend // ROW_BLOCK).astype(I32)
    n_slots = t * TOP_K
    slot_ids = (jnp.arange(t, dtype=I32)[:, None] + jnp.arange(TOP_K, dtype=I32)[None, :] * t).reshape(-1)
    spare = n_slots + jnp.arange(ROW_BLOCK, dtype=I32)
    row_slot = jnp.tile(spare, n_blocks).at[dest].set(slot_ids, unique_indices=True, mode="promise_in_bounds")
    src_tok = (jnp.where(row_slot < n_slots, row_slot % t, 0) * PACK_TILES).reshape(n_blocks, 1, ROW_BLOCK)
    slot_prev = jnp.concatenate([spare, row_slot]).reshape(n_blocks + 1, 1, ROW_BLOCK)

    acts = _moe_gu(hn_packed, src_tok, w_gu[0], b_gu[0], blk_e, n_active, run_end)
    y_slots = _moe_down(acts, slot_prev, w_down[0], b_down[0], blk_e, n_active, run_end, n_slots)
    out = _combine(topw_w, h1, g2, y_slots)
    return out.reshape(b, s, d)
```

```python
import functools
import math

import jax
import jax.numpy as jnp
from jax import lax
from jax.experimental import pallas as pl
from jax.experimental.pallas import tpu as pltpu

F32 = jnp.float32
BF16 = jnp.bfloat16
U32 = jnp.uint32
I32 = jnp.int32

CHUNK = 64
LEFT_CHUNKS = 8
ATTN_HEADS = 8
HEAD_DIM = 128
REL_MAX = 256
REL_MIN = -(CHUNK - 1)
RNN_BLOCKS = 8
CONV_W = 4
RG_C = 8.0
N_EXPERTS = 32
TOP_K = 4
SWIGLU_LIMIT = 7.0
SWIGLU_ALPHA = 1.702
EPS = 1e-6
NEG_BIG = -1e30

LANES = 128
SUBLANES = 8
VMEM_LIMIT = 56 * 1024 * 1024

ROW_BLOCK = 512
MOE_BUFS = 3
MOE_GU_SPLIT = (1280, 768)
PACK_TILES = 8
PROJ_TM = 1024
PROJ_TN = 1024
ADA_TN = 1024
NORM_TM = 512
RGLRU_TT = 512
MIXER_TM = 512
ROUTE_TM = 512
COMBINE_TM = 512
ATTN_QB = 4 * CHUNK
ATTN_KB = ATTN_QB + LEFT_CHUNKS * CHUNK
ATTN_TAB = ATTN_KB + ATTN_QB
CHUNK_SHIFT = CHUNK.bit_length() - 1
assert 1 << CHUNK_SHIFT == CHUNK


def _cparams(*sem):
    return pltpu.CompilerParams(dimension_semantics=sem, vmem_limit_bytes=VMEM_LIMIT)


def _ada_kernel(c_ref, w_ref, b_ref, o_ref):
    c = c_ref[...]
    ca = c * jax.nn.sigmoid(c)
    row = jnp.sum(ca * w_ref[...], axis=0, keepdims=True) + b_ref[...]
    o_ref[...] = jnp.broadcast_to(row, o_ref.shape)


def _ada(c, ada_w, ada_b):
    d, n = ada_w.shape
    tn = ADA_TN
    out = pl.pallas_call(
        _ada_kernel,
        grid=(n // tn,),
        in_specs=[pl.BlockSpec((d, 1), lambda j: (0, 0)),
                  pl.BlockSpec((d, tn), lambda j: (0, j)),
                  pl.BlockSpec((1, tn), lambda j: (0, j))],
        out_specs=pl.BlockSpec((SUBLANES, tn), lambda j: (0, j)),
        out_shape=jax.ShapeDtypeStruct((SUBLANES, n), F32),
        compiler_params=_cparams("arbitrary"),
        name="ada",
    )(c.reshape(d, 1), ada_w, ada_b.reshape(1, n))
    return out[0:1]


def _sigmoid_tanh(x):
    return 0.5 * jnp.tanh(0.5 * x) + 0.5


def _gelu_tanh(x):
    return 0.5 * x * (1.0 + jnp.tanh(math.sqrt(2.0 / math.pi) * (x + 0.044715 * (x * x * x))))


def _head_rms(a, g):
    outs = []
    for hh in range(a.shape[1] // HEAD_DIM):
        s = a[:, hh * HEAD_DIM:(hh + 1) * HEAD_DIM]
        ms = jnp.mean(s * s, axis=-1, keepdims=True)
        outs.append(s * lax.rsqrt(ms + EPS) * g)
    return jnp.concatenate(outs, axis=1)


def _norm_q_kernel(x_ref, g_ref, sc_ref, sh_ref, w_ref, hg_ref, xn_ref, q_ref, wb_ref):
    @pl.when(pl.program_id(0) == 0)
    def _():
        wb_ref[...] = w_ref[...].astype(BF16)

    x = x_ref[...]
    ms = jnp.mean(x * x, axis=-1, keepdims=True)
    xn = x * lax.rsqrt(ms + EPS) * g_ref[...]
    xn = (xn * (1.0 + sc_ref[...]) + sh_ref[...]).astype(BF16)
    xn_ref[...] = xn
    acc = jnp.dot(xn, wb_ref[...], preferred_element_type=F32)
    q_ref[...] = (_head_rms(acc, hg_ref[...]) * (1.0 / math.sqrt(HEAD_DIM))).astype(q_ref.dtype)


def _norm_q(x2, g, sc, sh, w, q_gain, width):
    t, d = x2.shape
    tm = NORM_TM
    vec = lambda: pl.BlockSpec((1, d), lambda i: (0, 0))
    return pl.pallas_call(
        _norm_q_kernel,
        grid=(t // tm,),
        in_specs=[pl.BlockSpec((tm, d), lambda i: (i, 0)), vec(), vec(), vec(),
                  pl.BlockSpec((d, width), lambda i: (0, 0), pipeline_mode=pl.Buffered(1)),
                  pl.BlockSpec((1, HEAD_DIM), lambda i: (0, 0))],
        out_specs=[pl.BlockSpec((tm, d), lambda i: (i, 0)),
                   pl.BlockSpec((tm, width), lambda i: (i, 0))],
        out_shape=[jax.ShapeDtypeStruct((t, d), BF16), jax.ShapeDtypeStruct((t, width), BF16)],
        scratch_shapes=[pltpu.VMEM((d, width), BF16)],
        compiler_params=_cparams("arbitrary"),
        name="norm1_q",
    )(x2, g, sc, sh, w, q_gain)


def _proj_kernel(x_ref, w_ref, hg_ref, o_ref, wb_ref, *, kind):
    @pl.when(pl.program_id(1) == 0)
    def _():
        wb_ref[...] = w_ref[...].astype(BF16)

    acc = jnp.dot(x_ref[...], wb_ref[...], preferred_element_type=F32)
    if kind == "k":
        r = _head_rms(acc, hg_ref[...])
    elif kind == "gelu":
        r = _gelu_tanh(acc)
    elif kind == "sigmoid":
        r = _sigmoid_tanh(acc)
    else:
        r = acc
    o_ref[...] = r.astype(o_ref.dtype)


def _proj(xn, w, head_gain, *, col0, width, kind, out_dtype, name):
    t, d = xn.shape
    tm, tn = PROJ_TM, PROJ_TN
    j0 = col0 // tn
    return pl.pallas_call(
        functools.partial(_proj_kernel, kind=kind),
        grid=(width // tn, t // tm),
        in_specs=[pl.BlockSpec((tm, d), lambda j, i: (i, 0)),
                  pl.BlockSpec((d, tn), lambda j, i: (0, j + j0)),
                  pl.BlockSpec((1, HEAD_DIM), lambda j, i: (0, 0))],
        out_specs=pl.BlockSpec((tm, tn), lambda j, i: (i, j)),
        out_shape=jax.ShapeDtypeStruct((t, width), out_dtype),
        scratch_shapes=[pltpu.VMEM((d, tn), BF16)],
        compiler_params=_cparams("arbitrary", "arbitrary"),
        name=name,
    )(xn, w, head_gain)


def _attn_kernel(q_ref, k0_ref, k1_ref, k2_ref, v0_ref, v1_ref, v2_ref, ftab_ref, o_ref, bias_ref):
    i = pl.program_id(0)
    col = lax.broadcasted_iota(I32, (ATTN_QB, ATTN_KB), 1)

    @pl.when(i == 0)
    def _():
        row = lax.broadcasted_iota(I32, (ATTN_QB, ATTN_KB), 0)
        cq = row >> CHUNK_SHIFT
        ck = col >> CHUNK_SHIFT
        visible = (ck >= cq) & (ck <= cq + LEFT_CHUNKS)
        for h in range(ATTN_HEADS):
            g = jnp.broadcast_to(ftab_ref[h:h + 1, :], (ATTN_QB, ATTN_TAB))
            y = pltpu.roll(g, 0, 1, stride=1, stride_axis=0)
            bias_ref[h] = jnp.where(visible, y[:, :ATTN_KB], NEG_BIG)

    def heads(mask_left_pad):
        in_seq = col >= (LEFT_CHUNKS * CHUNK - i * ATTN_QB)
        for h in range(ATTN_HEADS):
            sl = slice(h * HEAD_DIM, (h + 1) * HEAD_DIM)
            q = q_ref[:, sl]
            k = jnp.concatenate([k0_ref[:, sl], k1_ref[:, sl], k2_ref[:, sl]], axis=0)
            v = jnp.concatenate([v0_ref[:, sl], v1_ref[:, sl], v2_ref[:, sl]], axis=0)
            s = lax.dot_general(q, k, (((1,), (1,)), ((), ())), preferred_element_type=F32)
            s = s + bias_ref[h]
            if mask_left_pad:
                s = jnp.where(in_seq, s, NEG_BIG)
            m = jnp.max(s, axis=-1, keepdims=True)
            p = jnp.exp(s - m)
            l = jnp.sum(p, axis=-1, keepdims=True)
            o = jnp.dot(p.astype(BF16), v, preferred_element_type=F32) / l
            o_ref[:, sl] = o.astype(o_ref.dtype)

    first_blocks = (LEFT_CHUNKS * CHUNK) // ATTN_QB

    @pl.when(i < first_blocks)
    def _():
        heads(True)

    @pl.when(i >= first_blocks)
    def _():
        heads(False)


def _attn(q, k, v, ftab):
    t = q.shape[0]
    w = ATTN_HEADS * HEAD_DIM
    qb = ATTN_QB
    blk = lambda f: pl.BlockSpec((qb, w), f)
    return pl.pallas_call(
        _attn_kernel,
        grid=(t // qb,),
        in_specs=[blk(lambda i: (i, 0)),
                  blk(lambda i: (jnp.maximum(i - 2, 0), 0)), blk(lambda i: (jnp.maximum(i - 1, 0), 0)),
                  blk(lambda i: (i, 0)),
                  blk(lambda i: (jnp.maximum(i - 2, 0), 0)), blk(lambda i: (jnp.maximum(i - 1, 0), 0)),
                  blk(lambda i: (i, 0)),
                  pl.BlockSpec((ATTN_HEADS, ATTN_TAB), lambda i: (0, 0))],
        out_specs=blk(lambda i: (i, 0)),
        out_shape=jax.ShapeDtypeStruct((t, w), BF16),
        scratch_shapes=[pltpu.VMEM((ATTN_HEADS, qb, ATTN_KB), F32)],
        compiler_params=_cparams("arbitrary"),
        name="attn",
    )(q, k, k, k, v, v, v, ftab)


def _attn_distance_table(rel_bias):
    pad = LEFT_CHUNKS * CHUNK
    n_far = pad - REL_MAX + 1
    n_mid = REL_MAX - REL_MIN
    n_ahead = ATTN_KB - n_far - n_mid
    far = rel_bias[:, -1:]
    return jnp.concatenate([jnp.broadcast_to(far, (ATTN_HEADS, n_far)),
                            rel_bias[:, :n_mid][:, ::-1],
                            jnp.broadcast_to(rel_bias[:, :1], (ATTN_HEADS, n_ahead)),
                            jnp.broadcast_to(far, (ATTN_HEADS, ATTN_TAB - ATTN_KB))], axis=1)


def _rglru_kernel(rx_ref, gate_ref, cw_ref, cb_ref, wa_ref, ba_ref, wx_ref, bx_ref, lam_ref,
                  o_ref, xbuf, a_s, u_s, hc, *, tt):
    i = pl.program_id(0)
    width = rx_ref.shape[1]
    bw = width // RNN_BLOCKS

    @pl.when(i == 0)
    def _():
        xbuf[0:SUBLANES, :] = jnp.zeros((SUBLANES, width), F32)
        hc[...] = jnp.zeros(hc.shape, F32)

    xbuf[SUBLANES:SUBLANES + tt, :] = rx_ref[...]
    xc = cb_ref[...] + cw_ref[CONV_W - 1:CONV_W, :] * xbuf[SUBLANES:SUBLANES + tt, :]
    for d in range(1, CONV_W):
        xc = xc + cw_ref[CONV_W - 1 - d:CONV_W - d, :] * xbuf[SUBLANES - d:SUBLANES - d + tt, :]
    tail = xbuf[tt:tt + SUBLANES, :]
    xbuf[0:SUBLANES, :] = tail

    z = -lam_ref[...]
    softplus = jnp.maximum(z, 0.0) + jnp.log1p(jnp.exp(-jnp.abs(z)))
    xcb = xc.astype(BF16)
    for b in range(RNN_BLOCKS):
        sl = slice(b * bw, (b + 1) * bw)
        xb_ = xcb[:, sl]
        r = _sigmoid_tanh(jnp.dot(xb_, wa_ref[b], preferred_element_type=F32) + ba_ref[:, sl])
        ig = _sigmoid_tanh(jnp.dot(xb_, wx_ref[b], preferred_element_type=F32) + bx_ref[:, sl])
        log_a = (-RG_C) * r * softplus[:, sl]
        a = jnp.exp(log_a)
        th = jnp.tanh(log_a)
        one_m_a2 = (-2.0) * th / (1.0 - th)
        a_s[:, sl] = a
        u_s[:, sl] = jnp.sqrt(one_m_a2) * (ig * xc[:, sl])

    row = lax.broadcasted_iota(I32, (SUBLANES, width), 0)

    def body(g, h):
        r0 = pl.multiple_of(g * SUBLANES, SUBLANES)
        a = a_s[pl.ds(r0, SUBLANES), :]
        u = u_s[pl.ds(r0, SUBLANES), :]
        for s in (1, 2, 4):
            a_sh = pltpu.roll(a, s, 0)
            u_sh = pltpu.roll(u, s, 0)
            keep = row >= s
            u = jnp.where(keep, a * u_sh + u, u)
            a = jnp.where(keep, a * a_sh, a)
        hh = a * h + u
        u_s[pl.ds(r0, SUBLANES), :] = hh
        return hh[SUBLANES - 1:SUBLANES, :]

    h_last = lax.fori_loop(0, tt // SUBLANES, body, hc[0:1, :])
    hc[0:1, :] = h_last
    o_ref[...] = (u_s[...] * gate_ref[...]).astype(o_ref.dtype)


def _rglru(rx, gate, conv_w, conv_b, wa_bf, ba, wx_bf, bx, lam):
    t = rx.shape[0]
    width = conv_w.shape[1]
    bw = width // RNN_BLOCKS
    tt = RGLRU_TT
    vec = lambda: pl.BlockSpec((1, width), lambda i: (0, 0))
    wblk = lambda: pl.BlockSpec((RNN_BLOCKS, bw, bw), lambda i: (0, 0, 0))
    return pl.pallas_call(
        functools.partial(_rglru_kernel, tt=tt),
        grid=(t // tt,),
        in_specs=[pl.BlockSpec((tt, width), lambda i: (i, 0)),
                  pl.BlockSpec((tt, width), lambda i: (i, 0)),
                  pl.BlockSpec((CONV_W, width), lambda i: (0, 0)), vec(),
                  wblk(), vec(), wblk(), vec(), vec()],
        out_specs=pl.BlockSpec((tt, width), lambda i: (i, 0)),
        out_shape=jax.ShapeDtypeStruct((t, width), BF16),
        scratch_shapes=[pltpu.VMEM((tt + SUBLANES, width), F32),
                        pltpu.VMEM((tt, width), F32),
                        pltpu.VMEM((tt, width), F32),
                        pltpu.VMEM((SUBLANES, width), F32)],
        compiler_params=_cparams("arbitrary"),
        name="rglru",
    )(rx, gate, conv_w, conv_b, wa_bf, ba, wx_bf, bx, lam)


def _upmerge_kernel(ya_ref, yb_ref, sga_ref, sgb_ref, wa_ref, wr_ref, o_ref, wab_ref, wrb_ref):
    @pl.when(pl.program_id(1) == 0)
    def _():
        wab_ref[...] = wa_ref[...].astype(BF16)
        wrb_ref[...] = wr_ref[...].astype(BF16)

    up_a = jnp.dot(ya_ref[...], wab_ref[...], preferred_element_type=F32)
    up_b = jnp.dot(yb_ref[...], wrb_ref[...], preferred_element_type=F32)
    o_ref[...] = (sga_ref[...] * up_a + sgb_ref[...] * up_b).astype(o_ref.dtype)


def _upmerge(ya, yb, sg, wa, wr):
    t, aw = ya.shape
    rw = yb.shape[1]
    d = wa.shape[1]
    tm, tn = PROJ_TM, PROJ_TN
    nj = d // tn
    return pl.pallas_call(
        _upmerge_kernel,
        grid=(nj, t // tm),
        in_specs=[pl.BlockSpec((tm, aw), lambda j, i: (i, 0)),
                  pl.BlockSpec((tm, rw), lambda j, i: (i, 0)),
                  pl.BlockSpec((tm, tn), lambda j, i: (i, j)),
                  pl.BlockSpec((tm, tn), lambda j, i: (i, j + nj)),
                  pl.BlockSpec((aw, tn), lambda j, i: (0, j)),
                  pl.BlockSpec((rw, tn), lambda j, i: (0, j))],
        out_specs=pl.BlockSpec((tm, tn), lambda j, i: (i, j)),
        out_shape=jax.ShapeDtypeStruct((t, d), BF16),
        scratch_shapes=[pltpu.VMEM((aw, tn), BF16), pltpu.VMEM((rw, tn), BF16)],
        compiler_params=_cparams("arbitrary", "arbitrary"),
        name="upmerge",
    )(ya, yb, sg, sg, wa, wr)


def _mixer_out_kernel(mg_ref, x_ref, wo_ref, g1_ref, n2_ref, sc2_ref, sh2_ref, wrh_ref, brt_ref,
                      h1_ref, hnp_ref, lg_ref):
    proj = jnp.dot(mg_ref[...], wo_ref[...], preferred_element_type=F32)
    h1 = x_ref[...] + g1_ref[...] * proj
    h1_ref[...] = h1
    ms = jnp.mean(h1 * h1, axis=-1, keepdims=True)
    hn = h1 * lax.rsqrt(ms + EPS) * n2_ref[...]
    hn = hn * (1.0 + sc2_ref[...]) + sh2_ref[...]
    hb = hn.astype(BF16)
    hb32 = hb.astype(F32)
    hl = (hn - hb32).astype(BF16)
    ne = lg_ref.shape[1]
    hh = jnp.dot(hb, wrh_ref[...], preferred_element_type=F32)
    lh = jnp.dot(hl, wrh_ref[:, :ne], preferred_element_type=F32)
    lg_ref[...] = hh[:, :ne] + hh[:, ne:] + lh + brt_ref[...]
    half = hn.shape[1] // 2
    lo = pltpu.bitcast(hb32[:, :half], U32) >> 16
    hi = pltpu.bitcast(hb32[:, half:], U32) & jnp.uint32(0xFFFF0000)
    packed = hi | lo
    tm = packed.shape[0]
    for s in range(PACK_TILES):
        hnp_ref[pl.ds(s, tm, stride=PACK_TILES), :] = packed[:, s * LANES:(s + 1) * LANES]


def _mixer_out(merged, x2, wo_bf, g1, n2g, sc2, sh2, w_router, b_router):
    t, d = x2.shape
    ne = w_router.shape[1]
    tm = MIXER_TM
    w_hi = w_router.astype(BF16)
    w_lo = (w_router - w_hi.astype(F32)).astype(BF16)
    w_split = jnp.concatenate([w_hi, w_lo], axis=1)
    const = lambda shape: pl.BlockSpec(shape, lambda i: (0,) * len(shape), pipeline_mode=pl.Buffered(1))
    return pl.pallas_call(
        _mixer_out_kernel,
        grid=(t // tm,),
        in_specs=[pl.BlockSpec((tm, d), lambda i: (i, 0)),
                  pl.BlockSpec((tm, d), lambda i: (i, 0)),
                  const((d, d)),
                  const((1, d)), const((1, d)), const((1, d)), const((1, d)),
                  const((d, 2 * ne)), const((1, ne))],
        out_specs=[pl.BlockSpec((tm, d), lambda i: (i, 0)),
                   pl.BlockSpec((tm * PACK_TILES, LANES), lambda i: (i, 0)),
                   pl.BlockSpec((tm, ne), lambda i: (i, 0))],
        out_shape=[jax.ShapeDtypeStruct((t, d), F32),
                   jax.ShapeDtypeStruct((t * PACK_TILES, LANES), U32),
                   jax.ShapeDtypeStruct((t, ne), F32)],
        compiler_params=_cparams("arbitrary"),
        name="mixer_out",
    )(merged, x2, wo_bf, g1, n2g, sc2, sh2, w_split, b_router)


def _route_kernel(lg_ref, idx_ref, w_ref, pos_ref, cnt_ref, carry, *, rt):
    i = pl.program_id(0)
    ne = lg_ref.shape[1]

    @pl.when(i == 0)
    def _():
        carry[...] = jnp.zeros(carry.shape, F32)

    l = lg_ref[...]
    lane = lax.broadcasted_iota(I32, (rt, ne), 1).astype(F32)
    vals, idxs, hots = [], [], []
    for _ in range(TOP_K):
        m = jnp.max(l, axis=-1, keepdims=True)
        ik = jnp.min(jnp.where(l == m, lane, float(ne)), axis=-1, keepdims=True)
        hot = lane == ik
        vals.append(m)
        idxs.append(ik)
        hots.append(hot)
        l = jnp.where(hot, -jnp.inf, l)
    es = [jnp.exp(v - vals[0]) for v in vals]
    tot = es[0] + es[1] + es[2] + es[3]
    sel = jnp.zeros((rt, ne), F32)
    for hot in hots:
        sel = sel + hot.astype(F32)
    rr = lax.broadcasted_iota(I32, (rt, rt), 0)
    cc = lax.broadcasted_iota(I32, (rt, rt), 1)
    tri = (cc < rr).astype(BF16)
    before = jnp.dot(tri, sel.astype(BF16), preferred_element_type=F32) + carry[0:1, :]
    out_lane = lax.broadcasted_iota(I32, (rt, LANES), 1)
    idx_o = jnp.zeros((rt, LANES), F32)
    w_o = jnp.zeros((rt, LANES), F32)
    pos_o = jnp.zeros((rt, LANES), F32)
    for k in range(TOP_K):
        pk = jnp.sum(jnp.where(hots[k], before, 0.0), axis=-1, keepdims=True)
        idx_o = jnp.where(out_lane == k, idxs[k], idx_o)
        w_o = jnp.where(out_lane == k, es[k] / tot, w_o)
        pos_o = jnp.where(out_lane == k, pk, pos_o)
    idx_ref[...] = idx_o.astype(I32)
    w_ref[...] = w_o
    pos_ref[...] = pos_o.astype(I32)
    carry[0:1, :] = carry[0:1, :] + jnp.sum(sel, axis=0, keepdims=True)
    cnt_ref[...] = carry[...]


def _route(logits):
    t, ne = logits.shape
    rt = ROUTE_TM
    wide = lambda dt: jax.ShapeDtypeStruct((t, LANES), dt)
    return pl.pallas_call(
        functools.partial(_route_kernel, rt=rt),
        grid=(t // rt,),
        in_specs=[pl.BlockSpec((rt, ne), lambda i: (i, 0))],
        out_specs=[pl.BlockSpec((rt, LANES), lambda i: (i, 0)),
                   pl.BlockSpec((rt, LANES), lambda i: (i, 0)),
                   pl.BlockSpec((rt, LANES), lambda i: (i, 0)),
                   pl.BlockSpec((SUBLANES, ne), lambda i: (0, 0))],
        out_shape=[wide(I32), wide(F32), wide(I32), jax.ShapeDtypeStruct((SUBLANES, ne), F32)],
        scratch_shapes=[pltpu.VMEM((SUBLANES, ne), F32)],
        compiler_params=_cparams("arbitrary"),
        name="route",
    )(logits)


def _unpack_rows(x_ref):
    los, his = [], []
    for s in range(PACK_TILES):
        p = x_ref[pl.ds(s, ROW_BLOCK, stride=PACK_TILES), :]
        los.append(pltpu.bitcast(p << 16, F32).astype(BF16))
        his.append(pltpu.bitcast(p & jnp.uint32(0xFFFF0000), F32).astype(BF16))
    return jnp.concatenate(los + his, axis=1)


def _first_of_expert(be_ref, m, na):
    mm = jnp.minimum(m, na - 1)
    prev = be_ref[jnp.maximum(mm - 1, 0)]
    return (mm == 0) | (be_ref[mm] != prev)


def _weight_stream(be_ref, na_ref, end_ref, n_tiles, copies, convert):
    n = pl.program_id(0)
    m = pl.program_id(1)
    na = na_ref[0]
    active = m < na

    @pl.when((n == 0) & (m == 0))
    def _():
        for cp in copies(be_ref[0], 0):
            cp.start(priority=1)

    @pl.when(active & _first_of_expert(be_ref, m, na))
    def _():
        e = be_ref[m]
        for cp in copies(e, n):
            cp.wait()
        convert()
        run_end = end_ref[e]
        wrap = run_end >= na
        n2 = jnp.where(wrap, n + 1, n)
        m2 = jnp.where(wrap, 0, run_end)

        @pl.when(n2 < n_tiles)
        def _():
            for cp in copies(be_ref[m2], n2):
                cp.start(priority=1)

    return active


def _gu_weight_stream(be_ref, na_ref, end_ref, w_hbm, stg_g, stg_u, wgb, wub, sem, *, c0, tn, f):
    def copies(e, n):
        return [pltpu.make_async_copy(w_hbm.at[e, :, pl.ds(c0, tn)], stg_g, sem.at[0]),
                pltpu.make_async_copy(w_hbm.at[e, :, pl.ds(f + c0, tn)], stg_u, sem.at[1])]

    def convert():
        wgb[...] = stg_g[...].astype(BF16)
        wub[...] = stg_u[...].astype(BF16)

    return _weight_stream(be_ref, na_ref, end_ref, 1, copies, convert)


def _gu_compute(x, wgb, wub, b_ref, o_ref, *, c0, tn, f):
    g = jnp.dot(x, wgb[...], preferred_element_type=F32) + b_ref[:, c0:c0 + tn]
    u = jnp.dot(x, wub[...], preferred_element_type=F32) + b_ref[:, f + c0:f + c0 + tn]
    g = jnp.minimum(g, SWIGLU_LIMIT)
    u = jnp.clip(u, -SWIGLU_LIMIT, SWIGLU_LIMIT)
    glu = g * _sigmoid_tanh(SWIGLU_ALPHA * g)
    o_ref[...] = ((u + 1.0) * glu).astype(o_ref.dtype)


def _moe_gu_gather_kernel(be_ref, na_ref, end_ref, tok0_ref, tok1_ref, tok2_ref, hn_hbm, w_hbm, b_ref,
                          o_ref, xb_ref, stg_g, stg_u, wgb, wub, xbuf0, xbuf1, xbuf2, sem, gsem,
                          *, c0, tn, f):
    m = pl.program_id(1)
    na = na_ref[0]
    slot = m % MOE_BUFS
    xbufs = (xbuf0, xbuf1, xbuf2)

    def gather(tok_ref, s):
        for r in range(ROW_BLOCK):
            src = pl.multiple_of(tok_ref[0, 0, r], PACK_TILES)
            pltpu.make_async_copy(hn_hbm.at[pl.ds(src, PACK_TILES)],
                                  xbufs[s].at[pl.ds(r * PACK_TILES, PACK_TILES)], gsem.at[s]
                                  ).start(priority=r % 2)

    def gather_wait(s):
        pltpu.make_async_copy(hn_hbm.at[pl.ds(0, ROW_BLOCK * PACK_TILES)], xbufs[s], gsem.at[s]).wait()

    @pl.when(m == 0)
    def _():
        gather(tok0_ref, 0)
        gather(tok1_ref, 1)

    active = _gu_weight_stream(be_ref, na_ref, end_ref, w_hbm, stg_g, stg_u, wgb, wub, sem,
                               c0=c0, tn=tn, f=f)

    def step(s):
        gather_wait(s)
        gather(tok2_ref, (s + 2) % MOE_BUFS)
        xb_ref[...] = xbufs[s][...]
        _gu_compute(_unpack_rows(xbufs[s]), wgb, wub, b_ref, o_ref, c0=c0, tn=tn, f=f)

    for s in range(MOE_BUFS):
        @pl.when(active & (slot == s))
        def _(s=s):
            step(s)

        @pl.when(active & (m == na - 1) & (slot == s))
        def _(s=s):
            gather_wait((s + 1) % MOE_BUFS)
            gather_wait((s + 2) % MOE_BUFS)

    @pl.when(jnp.logical_not(active))
    def _():
        o_ref[...] = jnp.zeros(o_ref.shape, o_ref.dtype)
        xb_ref[...] = jnp.zeros(xb_ref.shape, xb_ref.dtype)


def _moe_gu_dense_kernel(be_ref, na_ref, end_ref, x_ref, w_hbm, b_ref, o_ref,
                         stg_g, stg_u, wgb, wub, sem, *, c0, tn, f):
    active = _gu_weight_stream(be_ref, na_ref, end_ref, w_hbm, stg_g, stg_u, wgb, wub, sem,
                               c0=c0, tn=tn, f=f)

    @pl.when(active)
    def _():
        _gu_compute(_unpack_rows(x_ref), wgb, wub, b_ref, o_ref, c0=c0, tn=tn, f=f)

    @pl.when(jnp.logical_not(active))
    def _():
        o_ref[...] = jnp.zeros(o_ref.shape, o_ref.dtype)


def _moe_gu(hn_packed, src_tok, w_gu, b_gu, blk_e, n_active, run_end):
    nb = src_tok.shape[0]
    ne, d, f2 = w_gu.shape
    f = f2 // 2
    assert sum(MOE_GU_SPLIT) == f
    mc = lambda m, na: jnp.minimum(m, na[0] - 1)
    b3 = b_gu.reshape(ne, 1, f2)
    bias_spec = pl.BlockSpec((None, 1, f2), lambda n, m, be, na, en: (be[mc(m, na)], 0, 0))
    weight_scratch = lambda tn: [pltpu.VMEM((d, tn), F32), pltpu.VMEM((d, tn), F32),
                                 pltpu.VMEM((d, tn), BF16), pltpu.VMEM((d, tn), BF16)]
    xrows = ROW_BLOCK * PACK_TILES
    act_shape = lambda tn: jax.ShapeDtypeStruct((nb * ROW_BLOCK, tn), BF16)

    tok_blk = lambda f_: pl.BlockSpec((1, 1, ROW_BLOCK), f_, memory_space=pltpu.SMEM)
    tn = MOE_GU_SPLIT[0]
    act0, xb = pl.pallas_call(
        functools.partial(_moe_gu_gather_kernel, c0=0, tn=tn, f=f),
        grid_spec=pltpu.PrefetchScalarGridSpec(
            num_scalar_prefetch=3,
            grid=(1, nb),
            in_specs=[tok_blk(lambda n, m, be, na, en: (0, 0, 0)),
                      tok_blk(lambda n, m, be, na, en: (1 % na[0], 0, 0)),
                      tok_blk(lambda n, m, be, na, en: ((m + 2) % na[0], 0, 0)),
                      pl.BlockSpec(memory_space=pl.ANY),
                      pl.BlockSpec(memory_space=pl.ANY), bias_spec],
            out_specs=[pl.BlockSpec((ROW_BLOCK, tn), lambda n, m, be, na, en: (m, 0)),
                       pl.BlockSpec((xrows, LANES), lambda n, m, be, na, en: (m, 0))],
            scratch_shapes=weight_scratch(tn) + [pltpu.VMEM((xrows, LANES), U32)] * MOE_BUFS
                           + [pltpu.SemaphoreType.DMA((2,)), pltpu.SemaphoreType.DMA((MOE_BUFS,))],
        ),
        out_shape=[act_shape(tn), jax.ShapeDtypeStruct((nb * xrows, LANES), U32)],
        compiler_params=_cparams("arbitrary", "arbitrary"),
        name="moe_gu0",
    )(blk_e, n_active, run_end, src_tok, src_tok, src_tok, hn_packed, w_gu, b3)

    acts = [act0]
    c0 = tn
    for col, tn in enumerate(MOE_GU_SPLIT[1:], start=1):
        acts.append(pl.pallas_call(
            functools.partial(_moe_gu_dense_kernel, c0=c0, tn=tn, f=f),
            grid_spec=pltpu.PrefetchScalarGridSpec(
                num_scalar_prefetch=3,
                grid=(1, nb),
                in_specs=[pl.BlockSpec((xrows, LANES), lambda n, m, be, na, en: (mc(m, na), 0)),
                          pl.BlockSpec(memory_space=pl.ANY), bias_spec],
                out_specs=pl.BlockSpec((ROW_BLOCK, tn), lambda n, m, be, na, en: (m, 0)),
                scratch_shapes=weight_scratch(tn) + [pltpu.SemaphoreType.DMA((2,))],
            ),
            out_shape=act_shape(tn),
            compiler_params=_cparams("arbitrary", "arbitrary"),
            name="moe_gu%d" % col,
        )(blk_e, n_active, run_end, xb, w_gu, b3))
        c0 += tn
    return acts


def _moe_down_kernel(be_ref, na_ref, end_ref, slot_ref, *refs, n_act):
    a_refs = refs[:n_act]
    w_hbm, b_ref, y_hbm, stg, wb, ybuf0, ybuf1, ybuf2, sem, ssem = refs[n_act:]
    m = pl.program_id(1)
    na = na_ref[0]
    p = m % MOE_BUFS
    ybufs = (ybuf0, ybuf1, ybuf2)

    def scatter(s):
        for r in range(ROW_BLOCK):
            pltpu.make_async_copy(ybufs[s].at[pl.ds(r, 1)], y_hbm.at[pl.ds(slot_ref[0, 0, r], 1)],
                                  ssem.at[s]).start()

    def scatter_wait(s):
        pltpu.make_async_copy(ybufs[s], y_hbm.at[pl.ds(0, ROW_BLOCK)], ssem.at[s]).wait()

    @pl.when(m == 0)
    def _():
        ybuf2[...] = jnp.zeros(ybuf2.shape, ybuf2.dtype)

    for s in range(MOE_BUFS):
        @pl.when((m >= 2) & (m <= na + 2) & (p == s))
        def _(s=s):
            scatter_wait(s)

    def copies(e, n):
        return [pltpu.make_async_copy(w_hbm.at[e], stg, sem.at[0])]

    def convert():
        wb[...] = stg[...].astype(BF16)

    active = _weight_stream(be_ref, na_ref, end_ref, 1, copies, convert)

    for s in range(MOE_BUFS):
        @pl.when(active & (p == s))
        def _(s=s):
            scatter((s + MOE_BUFS - 1) % MOE_BUFS)
            a = jnp.concatenate([a_ref[...] for a_ref in a_refs], axis=1)
            ybufs[s][...] = jnp.dot(a, wb[...], preferred_element_type=F32) + b_ref[...]

        @pl.when((m == na) & (p == s))
        def _(s=s):
            scatter((s + MOE_BUFS - 1) % MOE_BUFS)


def _moe_down(acts, slot_prev, w_down, b_down, blk_e, n_active, run_end, n_slots):
    n_rows = acts[0].shape[0]
    ne, f, d = w_down.shape
    nb = n_rows // ROW_BLOCK
    mc = lambda m, na: jnp.minimum(m, na[0] - 1)
    grid_spec = pltpu.PrefetchScalarGridSpec(
        num_scalar_prefetch=3,
        grid=(1, nb + MOE_BUFS),
        in_specs=[pl.BlockSpec((1, 1, ROW_BLOCK), lambda n, m, be, na, en: (jnp.minimum(m, nb), 0, 0),
                               memory_space=pltpu.SMEM),
                  *[pl.BlockSpec((ROW_BLOCK, a.shape[1]), lambda n, m, be, na, en: (mc(m, na), 0))
                    for a in acts],
                  pl.BlockSpec(memory_space=pl.ANY),
                  pl.BlockSpec((None, 1, d), lambda n, m, be, na, en: (be[mc(m, na)], 0, 0))],
        out_specs=pl.BlockSpec(memory_space=pl.ANY),
        scratch_shapes=[pltpu.VMEM((f, d), F32), pltpu.VMEM((f, d), BF16),
                        *[pltpu.VMEM((ROW_BLOCK, d), F32)] * MOE_BUFS,
                        pltpu.SemaphoreType.DMA((1,)), pltpu.SemaphoreType.DMA((MOE_BUFS,))],
    )
    return pl.pallas_call(
        functools.partial(_moe_down_kernel, n_act=len(acts)),
        grid_spec=grid_spec,
        out_shape=jax.ShapeDtypeStruct((n_slots + ROW_BLOCK, d), F32),
        compiler_params=_cparams("arbitrary", "arbitrary"),
        name="moe_down",
    )(blk_e, n_active, run_end, slot_prev, *acts, w_down, b_down.reshape(ne, 1, d))


def _combine_kernel(w_ref, h1_ref, g2_ref, *refs):
    y_refs, o_ref = refs[:TOP_K], refs[TOP_K]
    w = w_ref[...]
    acc = y_refs[0][...] * w[:, 0:1]
    for k in range(1, TOP_K):
        acc = acc + y_refs[k][...] * w[:, k:k + 1]
    o_ref[...] = h1_ref[...] + g2_ref[...] * acc


def _combine(top_w, h1, g2, y_slots):
    t, d = h1.shape
    tm = COMBINE_TM
    nt = t // tm
    return pl.pallas_call(
        _combine_kernel,
        grid=(nt,),
        in_specs=[pl.BlockSpec((tm, LANES), lambda i: (i, 0)),
                  pl.BlockSpec((tm, d), lambda i: (i, 0)),
                  pl.BlockSpec((1, d), lambda i: (0, 0))]
                 + [pl.BlockSpec((tm, d), lambda i, k=k: (k * nt + i, 0)) for k in range(TOP_K)],
        out_specs=pl.BlockSpec((tm, d), lambda i: (i, 0)),
        out_shape=jax.ShapeDtypeStruct((t, d), F32),
        compiler_params=_cparams("arbitrary"),
        name="combine",
    )(top_w, h1, g2, *([y_slots] * TOP_K))


def kernel(x, c, ada_w, ada_b, norm1_g, w_in, q_norm_g, k_norm_g, rel_bias, conv_w, conv_b,
           rg_a_w, rg_a_b, rg_x_w, rg_x_b, rg_lambda, w_attn_up, w_rnn_up, w_out, norm2_g,
           w_router, b_router, w_gu, b_gu, w_down, b_down):
    b, s, d = x.shape
    assert b == 1 and ada_w.shape[0] == 1, "single batch row, single layer"
    assert d == 2 * PACK_TILES * LANES, "packed token row must be exactly one (PACK_TILES, LANES) tile"
    assert s % PROJ_TM == 0 and (s * TOP_K) % ROW_BLOCK == 0, "sequence must tile evenly"
    t = s
    x2 = x.reshape(t, d)
    aw = ATTN_HEADS * HEAD_DIM
    rw = conv_w.shape[2]

    mod = _ada(c, ada_w[0], ada_b[0])
    sh1, sc1, g1, sh2, sc2, g2 = [mod[:, k * d:(k + 1) * d] for k in range(6)]

    w_in0 = w_in[0]
    qg = q_norm_g[0].reshape(1, HEAD_DIM)
    kg = k_norm_g[0].reshape(1, HEAD_DIM)
    n1g = norm1_g[0].reshape(1, d)
    xn, q = _norm_q(x2, n1g, sc1, sh1, w_in0, qg, aw)
    k = _proj(xn, w_in0, kg, col0=aw, width=aw, kind="k", out_dtype=BF16, name="proj_k")
    v = _proj(xn, w_in0, kg, col0=2 * aw, width=aw, kind="id", out_dtype=BF16, name="proj_v")
    rx = _proj(xn, w_in0, kg, col0=3 * aw, width=rw, kind="id", out_dtype=F32, name="proj_rx")
    gate = _proj(xn, w_in0, kg, col0=3 * aw + rw, width=rw, kind="gelu", out_dtype=F32, name="proj_gate")
    sg = _proj(xn, w_in0, kg, col0=3 * aw + 2 * rw, width=2 * d, kind="sigmoid", out_dtype=F32,
               name="proj_sg")

    ya = _attn(q, k, v, _attn_distance_table(rel_bias[0]))
    yb = _rglru(rx, gate, conv_w[0], conv_b[0].reshape(1, rw),
                rg_a_w[0].astype(BF16), rg_a_b[0].reshape(1, rw),
                rg_x_w[0].astype(BF16), rg_x_b[0].reshape(1, rw), rg_lambda[0].reshape(1, rw))

    merged = _upmerge(ya, yb, sg, w_attn_up[0], w_rnn_up[0])
    h1, hn_packed, logits = _mixer_out(
        merged, x2, w_out[0].astype(BF16), g1, norm2_g[0].reshape(1, d), sc2, sh2,
        w_router[0], b_router[0].reshape(1, N_EXPERTS))

    idx_w, topw_w, pos_w, cnt = _route(logits)
    idx = idx_w[:, :TOP_K]
    pos = pos_w[:, :TOP_K]
    counts = cnt[0].astype(I32)
    padded = (counts + ROW_BLOCK - 1) // ROW_BLOCK * ROW_BLOCK
    pend = jnp.cumsum(padded)
    pstart = pend - padded
    experts = jnp.arange(N_EXPERTS, dtype=I32)
    dest = (pos + jnp.sum(jnp.where(idx[:, :, None] == experts, pstart, 0), axis=-1)).reshape(-1)
    n_blocks = (t * TOP_K) // ROW_BLOCK + N_EXPERTS
    n_rows = n_blocks * ROW_BLOCK
    blk_row0 = jnp.arange(n_blocks, dtype=I32) * ROW_BLOCK
    blk_e = jnp.minimum(jnp.sum((pend[None, :] <= blk_row0[:, None]).astype(I32), axis=1), N_EXPERTS - 1)
    n_active = (pend[-1:] // ROW_BLOCK).astype(I32)
    run_end = (pend // ROW_BLOCK).astype(I32)
    n_slots = t * TOP_K
    slot_ids = (jnp.arange(t, dtype=I32)[:, None] + jnp.arange(TOP_K, dtype=I32)[None, :] * t).reshape(-1)
    spare = n_slots + jnp.arange(ROW_BLOCK, dtype=I32)
    row_slot = jnp.tile(spare, n_blocks).at[dest].set(slot_ids, unique_indices=True, mode="promise_in_bounds")
    src_tok = (jnp.where(row_slot < n_slots, row_slot % t, 0) * PACK_TILES).reshape(n_blocks, 1, ROW_BLOCK)
    slot_prev = jnp.concatenate([spare, row_slot]).reshape(n_blocks + 1, 1, ROW_BLOCK)

    acts = _moe_gu(hn_packed, src_tok, w_gu[0], b_gu[0], blk_e, n_active, run_end)
    y_slots = _moe_down(acts, slot_prev, w_down[0], b_down[0], blk_e, n_active, run_end, n_slots)
    out = _combine(topw_w, h1, g2, y_slots)
    return out.reshape(b, s, d)
```
